```python
import jax, jax.numpy as jnp
from jax import lax
import numpy as np

D_MODEL = 1024
BATCH = 8
SEQ = 2048
DEPTH = 2

HEAD_DIM = 64
FOX_HEADS = 6
SB_HEADS = 4
ML_HEADS = 6
D_FOX = FOX_HEADS * HEAD_DIM
D_SB = SB_HEADS * HEAD_DIM
D_ML = ML_HEADS * HEAD_DIM
D_MIX = D_FOX + D_SB + D_ML
N_IN = 3 * D_FOX + FOX_HEADS + 3 * D_SB + 4 * D_ML + 2 * ML_HEADS
BLOCK_Q = 128
ML_CHUNK = 128
CONV_WIDTH = 4
MEM_LEN = 256
X_HEADS = 4
X_HEAD_DIM = 128
D_X = X_HEADS * X_HEAD_DIM
D_FF = -(-8 * D_MODEL // (3 * 256)) * 256
RMS_EPS = 1e-6

kernel_name = "hybrid_fox_stickbreak_mlstm_block"


def rms_norm(x, w):
    xf = x.astype(jnp.float32)
    y = xf * lax.rsqrt(jnp.mean(xf * xf, axis=-1, keepdims=True) + RMS_EPS)
    return (y * w.astype(jnp.float32)).astype(x.dtype)


def split_columns(u, sizes):
    idx, acc = [], 0
    for s in sizes[:-1]:
        acc += s
        idx.append(acc)
    return jnp.split(u, idx, axis=-1)


def heads(t, n_heads):
    return t.reshape(t.shape[0], t.shape[1], n_heads, HEAD_DIM)


def query_blocks(t):
    b, s = t.shape[:2]
    return jnp.moveaxis(t.reshape((b, s // BLOCK_Q, BLOCK_Q) + t.shape[2:]), 1, 0)


def unblock(o):
    o = jnp.moveaxis(o, 0, 1)
    return o.reshape((o.shape[0], o.shape[1] * o.shape[2]) + o.shape[3:])


def fox_attention(q, k, v, log_f):
    s_len = q.shape[1]
    c = lax.cumsum(log_f, axis=1)
    c_k = jnp.transpose(c, (0, 2, 1))
    kpos = jnp.arange(s_len)
    scale = HEAD_DIM ** -0.5

    def block(args):
        qb, cb, start = args
        qpos = start + jnp.arange(BLOCK_Q)
        logits = jnp.einsum('bqhd,bkhd->bhqk', qb, k).astype(jnp.float32) * scale
        logits = logits + jnp.transpose(cb, (0, 2, 1))[..., None] - c_k[:, :, None, :]
        logits = jnp.where(kpos[None, :] <= qpos[:, None], logits, -jnp.inf)
        p = jax.nn.softmax(logits, axis=-1).astype(v.dtype)
        return jnp.einsum('bhqk,bkhd->bqhd', p, v)

    starts = jnp.arange(s_len // BLOCK_Q, dtype=jnp.int32) * BLOCK_Q
    return unblock(lax.map(block, (query_blocks(q), query_blocks(c), starts)))


def stick_breaking_attention(q, k, v):
    s_len = q.shape[1]
    kpos = jnp.arange(s_len)
    scale = HEAD_DIM ** -0.5

    def block(args):
        qb, start = args
        qpos = start + jnp.arange(BLOCK_Q)
        z = jnp.einsum('bqhd,bkhd->bhqk', qb, k).astype(jnp.float32) * scale
        valid = kpos[None, :] < qpos[:, None]
        log_keep = jnp.where(valid, jax.nn.log_sigmoid(-z), 0.0)
        suffix = lax.cumsum(log_keep, axis=3, reverse=True) - log_keep
        a = jnp.where(valid, jnp.exp(jax.nn.log_sigmoid(z) + suffix), 0.0).astype(v.dtype)
        return jnp.einsum('bhqk,bkhd->bqhd', a, v)

    starts = jnp.arange(s_len // BLOCK_Q, dtype=jnp.int32) * BLOCK_Q
    return unblock(lax.map(block, (query_blocks(q), starts)))


def mlstm_chunkwise(q, k, v, i_pre, f_pre):
    b_sz, s_len, n_h, dh = q.shape
    nc = s_len // ML_CHUNK

    def to_chunks(t):
        t = t.astype(jnp.float32).reshape((b_sz, nc, ML_CHUNK, n_h) + t.shape[3:])
        return jnp.moveaxis(jnp.moveaxis(t, 3, 2), 1, 0)

    qc = to_chunks(q)
    kc = to_chunks(k) * (dh ** -0.5)
    vc = to_chunks(v)
    ic = to_chunks(i_pre)
    lfc = jax.nn.log_sigmoid(to_chunks(f_pre))
    tri = jnp.arange(ML_CHUNK)[:, None] >= jnp.arange(ML_CHUNK)[None, :]

    def body(carry, inp):
        c_st, n_st, m_st = carry
        qx, kx, vx, ix, lfx = inp
        b = lax.cumsum(lfx, axis=2)
        g = b[..., -1]
        dlog = jnp.where(tri, b[..., :, None] - b[..., None, :] + ix[..., None, :], -jnp.inf)
        inter = b + m_st[..., None]
        m_t = jnp.maximum(inter, jnp.max(dlog, axis=-1))
        d = jnp.exp(dlog - m_t[..., None])
        sc = jnp.einsum('bhtd,bhsd->bhts', qx, kx) * d
        w_inter = jnp.exp(inter - m_t)
        num = jnp.einsum('bhts,bhsd->bhtd', sc, vx) + w_inter[..., None] * jnp.einsum('bhtk,bhkv->bhtv', qx, c_st)
        den = jnp.sum(sc, axis=-1) + w_inter * jnp.einsum('bhtk,bhk->bht', qx, n_st)
        h = num / jnp.maximum(jnp.abs(den), jnp.exp(-m_t))[..., None]
        src = g[..., None] - b + ix
        m_new = jnp.maximum(g + m_st, jnp.max(src, axis=-1))
        decay = jnp.exp(g + m_st - m_new)
        w = jnp.exp(src - m_new[..., None])
        c_new = decay[..., None, None] * c_st + jnp.einsum('bhs,bhsk,bhsv->bhkv', w, kx, vx)
        n_new = decay[..., None] * n_st + jnp.einsum('bhs,bhsk->bhk', w, kx)
        return (c_new, n_new, m_new), h

    init = (jnp.zeros((b_sz, n_h, dh, dh), jnp.float32),
            jnp.zeros((b_sz, n_h, dh), jnp.float32),
            jnp.zeros((b_sz, n_h), jnp.float32))
    _, h = lax.scan(body, init, (qc, kc, vc, ic, lfc))
    h = jnp.moveaxis(jnp.moveaxis(h, 0, 1), 2, 3)
    return h.reshape(b_sz, s_len, n_h, dh)


def causal_depthwise_conv(x, w):
    return lax.conv_general_dilated(
        x, w.astype(x.dtype)[:, None, :], window_strides=(1,), padding=[(CONV_WIDTH - 1, 0)],
        dimension_numbers=('NWC', 'WIO', 'NWC'), feature_group_count=x.shape[-1])


def hybrid_mixer(xn, w_in, fox_f_b, ml_conv_w, ml_i_b, ml_f_b, ml_norm_w, w_out):
    b_sz, s_len, _ = xn.shape
    u = xn @ w_in
    (fox_q, fox_k, fox_v, fox_f, sb_q, sb_k, sb_v,
     ml_qk, ml_v, ml_o, ml_i, ml_f) = split_columns(
        u, [D_FOX, D_FOX, D_FOX, FOX_HEADS, D_SB, D_SB, D_SB, 2 * D_ML, D_ML, D_ML, ML_HEADS, ML_HEADS])

    fox_log_f = jax.nn.log_sigmoid((fox_f + fox_f_b).astype(jnp.float32))
    y_fox = fox_attention(heads(fox_q, FOX_HEADS), heads(fox_k, FOX_HEADS), heads(fox_v, FOX_HEADS), fox_log_f)

    y_sb = stick_breaking_attention(heads(sb_q, SB_HEADS), heads(sb_k, SB_HEADS), heads(sb_v, SB_HEADS))

    qk = jax.nn.silu(causal_depthwise_conv(ml_qk, ml_conv_w))
    ml_q, ml_k = jnp.split(qk, 2, axis=-1)
    h_ml = mlstm_chunkwise(heads(ml_q, ML_HEADS), heads(ml_k, ML_HEADS), heads(ml_v, ML_HEADS),
                           ml_i + ml_i_b, ml_f + ml_f_b)
    h_ml = h_ml * lax.rsqrt(jnp.mean(h_ml * h_ml, axis=-1, keepdims=True) + RMS_EPS)
    h_ml = h_ml * ml_norm_w.astype(jnp.float32).reshape(ML_HEADS, HEAD_DIM)
    y_ml = (jax.nn.sigmoid(ml_o.astype(jnp.float32)).reshape(b_sz, s_len, ML_HEADS, HEAD_DIM) * h_ml).astype(xn.dtype)

    y = jnp.concatenate([y_fox.reshape(b_sz, s_len, D_FOX),
                         y_sb.reshape(b_sz, s_len, D_SB),
                         y_ml.reshape(b_sz, s_len, D_ML)], axis=-1)
    return y @ w_out


def memory_cross_attention(hn, mem_n, wx_q, wx_kv, wx_o):
    b_sz, s_len, _ = hn.shape
    q = (hn @ wx_q).reshape(b_sz, s_len, X_HEADS, X_HEAD_DIM)
    k, v = jnp.split(mem_n @ wx_kv, 2, axis=-1)
    k = k.reshape(b_sz, -1, X_HEADS, X_HEAD_DIM)
    v = v.reshape(b_sz, -1, X_HEADS, X_HEAD_DIM)
    logits = jnp.einsum('bshd,bmhd->bhsm', q, k).astype(jnp.float32) * (X_HEAD_DIM ** -0.5)
    p = jax.nn.softmax(logits, axis=-1).astype(v.dtype)
    o = jnp.einsum('bhsm,bmhd->bshd', p, v).reshape(b_sz, s_len, D_X)
    return o @ wx_o


def swiglu_ffn(hn, w_gate, w_up, w_down):
    return (jax.nn.silu(hn @ w_gate) * (hn @ w_up)) @ w_down


def setup_inputs(seed: int = 0) -> dict:
    key = jax.random.key(seed)
    ks = jax.random.split(key, 24)

    def normal(k, shape, scale):
        return jax.random.normal(k, shape, jnp.float32) * scale

    def gain(k, shape):
        return 1.0 + 0.02 * jax.random.normal(k, shape, jnp.float32)

    fox_f_b = jnp.linspace(1.0, 4.0, FOX_HEADS, dtype=jnp.float32)[None, :] + normal(ks[3], (DEPTH, FOX_HEADS), 0.1)
    ml_f_b = jnp.linspace(3.0, 6.0, ML_HEADS, dtype=jnp.float32)[None, :] + normal(ks[6], (DEPTH, ML_HEADS), 0.1)
    return {
        "x": normal(ks[0], (BATCH, SEQ, D_MODEL), 1.0),
        "mem": normal(ks[1], (BATCH, MEM_LEN, D_MODEL), 1.0),
        "norm_mix_w": gain(ks[2], (DEPTH, D_MODEL)),
        "w_in": normal(ks[4], (DEPTH, D_MODEL, N_IN), D_MODEL ** -0.5),
        "fox_f_b": fox_f_b,
        "ml_conv_w": normal(ks[5], (DEPTH, CONV_WIDTH, 2 * D_ML), CONV_WIDTH ** -0.5),
        "ml_i_b": normal(ks[7], (DEPTH, ML_HEADS), 0.1),
        "ml_f_b": ml_f_b,
        "ml_norm_w": gain(ks[8], (DEPTH, D_ML)),
        "w_out": normal(ks[9], (DEPTH, D_MIX, D_MODEL), D_MIX ** -0.5),
        "norm_x_w": gain(ks[10], (DEPTH, D_MODEL)),
        "mem_norm_w": gain(ks[11], (DEPTH, D_MODEL)),
        "wx_q": normal(ks[12], (DEPTH, D_MODEL, D_X), D_MODEL ** -0.5),
        "wx_kv": normal(ks[13], (DEPTH, D_MODEL, 2 * D_X), D_MODEL ** -0.5),
        "wx_o": normal(ks[14], (DEPTH, D_X, D_MODEL), D_X ** -0.5),
        "norm_ffn_w": gain(ks[15], (DEPTH, D_MODEL)),
        "w_gate": normal(ks[16], (DEPTH, D_MODEL, D_FF), D_MODEL ** -0.5),
        "w_up": normal(ks[17], (DEPTH, D_MODEL, D_FF), D_MODEL ** -0.5),
        "w_down": normal(ks[18], (DEPTH, D_FF, D_MODEL), D_FF ** -0.5),
        "final_norm_w": gain(ks[19], (D_MODEL,)),
    }


def reference(x, mem, norm_mix_w, w_in, fox_f_b, ml_conv_w, ml_i_b, ml_f_b, ml_norm_w, w_out,
              norm_x_w, mem_norm_w, wx_q, wx_kv, wx_o, norm_ffn_w, w_gate, w_up, w_down, final_norm_w):
    h = x
    for l in range(DEPTH):
        xn = rms_norm(h, norm_mix_w[l])
        h = h + hybrid_mixer(xn, w_in[l], fox_f_b[l], ml_conv_w[l], ml_i_b[l], ml_f_b[l], ml_norm_w[l], w_out[l])
        hn = rms_norm(h, norm_x_w[l])
        mn = rms_norm(mem, mem_norm_w[l])
        h = h + memory_cross_attention(hn, mn, wx_q[l], wx_kv[l], wx_o[l])
        hn = rms_norm(h, norm_ffn_w[l])
        h = h + swiglu_ffn(hn, w_gate[l], w_up[l], w_down[l])
    return rms_norm(h, final_norm_w)
```

```python
import functools

import jax
import jax.numpy as jnp
from jax import lax
from jax.experimental import pallas as pl
from jax.experimental.pallas import tpu as pltpu

D_MODEL = 1024
HEAD_DIM = 64
FOX_HEADS = 6
SB_HEADS = 4
ML_HEADS = 6
D_FOX = FOX_HEADS * HEAD_DIM
D_SB = SB_HEADS * HEAD_DIM
D_ML = ML_HEADS * HEAD_DIM
ML_CHUNK = 128
CONV_WIDTH = 4
X_HEADS = 4
X_HEAD_DIM = 128
D_X = X_HEADS * X_HEAD_DIM
D_FF = 2816
RMS_EPS = 1e-6

LANES = 128
PAIR = 2 * HEAD_DIM
D_MAIN = 3 * D_FOX + 3 * D_SB + 4 * D_ML
N_GATE = LANES
GT_ROWS = 32
NEG = -1e30
VMEM_LIMIT = 56 * 1024 * 1024

BF16 = jnp.bfloat16
F32 = jnp.float32

FOX_Q, FOX_K, FOX_V = 0, 3, 6
SB_Q, SB_K, SB_V = 9, 11, 13
G_FOX, G_MLI, G_MLF = 0, 6, 12


def _cparams(*sem):
    return pltpu.CompilerParams(dimension_semantics=sem, vmem_limit_bytes=VMEM_LIMIT)


def _rms(xf, w):
    ms = jnp.mean(xf * xf, axis=-1, keepdims=True)
    return xf * lax.rsqrt(ms + RMS_EPS) * w


def _dot(a, b):
    return jnp.dot(a, b, preferred_element_type=F32)


def _dot_nt(a, b):
    return lax.dot_general(a, b, (((1,), (1,)), ((), ())), preferred_element_type=F32)


def _dot_tn(a, b):
    return lax.dot_general(a, b, (((0,), (0,)), ((), ())), preferred_element_type=F32)


def _log_sigmoid(x):
    return jnp.minimum(x, 0.0) - jnp.log(1.0 + jnp.exp(-jnp.abs(x)))


def _inproj_kernel(x_ref, nw_ref, w_ref, wg_ref, u_ref, g_ref, *, n_chunk):
    xn = _rms(x_ref[...], nw_ref[...]).astype(BF16)
    for c in range(D_MAIN // n_chunk):
        sl = slice(c * n_chunk, (c + 1) * n_chunk)
        u_ref[:, sl] = _dot(xn, w_ref[:, sl]).astype(BF16)
    g_ref[...] = _dot(xn, wg_ref[...])


def _inproj(h, nw, w_main, w_gate, tm=512):
    t = h.shape[0]
    return pl.pallas_call(
        functools.partial(_inproj_kernel, n_chunk=1152),
        grid=(t // tm,),
        in_specs=[
            pl.BlockSpec((tm, D_MODEL), lambda i: (i, 0)),
            pl.BlockSpec((1, D_MODEL), lambda i: (0, 0)),
            pl.BlockSpec((D_MODEL, D_MAIN), lambda i: (0, 0)),
            pl.BlockSpec((D_MODEL, N_GATE), lambda i: (0, 0)),
        ],
        out_specs=[
            pl.BlockSpec((tm, D_MAIN), lambda i: (i, 0)),
            pl.BlockSpec((tm, N_GATE), lambda i: (i, 0)),
        ],
        out_shape=[jax.ShapeDtypeStruct((t, D_MAIN), BF16), jax.ShapeDtypeStruct((t, N_GATE), F32)],
        compiler_params=_cparams("parallel"),
        name="inproj",
    )(h, nw, w_main, w_gate)


def _gates_kernel(g_ref, b_ref, go_ref, gt_ref, carry_ref):
    @pl.when(pl.program_id(1) == 0)
    def _():
        carry_ref[...] = jnp.zeros_like(carry_ref)

    pre = g_ref[...] + b_ref[...]
    col = lax.broadcasted_iota(jnp.int32, (ML_CHUNK, LANES), 1)
    is_i = (col >= G_MLI) & (col < G_MLF)
    x = jnp.where(is_i, 0.0, _log_sigmoid(pre))
    r = lax.broadcasted_iota(jnp.int32, (ML_CHUNK, ML_CHUNK), 0)
    c = lax.broadcasted_iota(jnp.int32, (ML_CHUNK, ML_CHUNK), 1)
    tri = jnp.where(r >= c, 1.0, 0.0).astype(BF16)
    x1 = x.astype(BF16)
    r1 = x - x1.astype(F32)
    x2 = r1.astype(BF16)
    x3 = (r1 - x2.astype(F32)).astype(BF16)
    cum = _dot(tri, x1) + _dot(tri, x2) + _dot(tri, x3)
    tot = cum + carry_ref[...]
    out = jnp.where(col < G_MLI, tot, jnp.where(is_i, pre, cum))
    carry_ref[...] = tot[ML_CHUNK - 1:ML_CHUNK, :]
    go_ref[...] = out
    gt_ref[...] = out.T[:GT_ROWS, :]


def _gates(g, bias, bsz, s):
    nc = s // ML_CHUNK
    return pl.pallas_call(
        _gates_kernel,
        grid=(bsz, nc),
        in_specs=[
            pl.BlockSpec((ML_CHUNK, N_GATE), lambda b, c: (b * nc + c, 0)),
            pl.BlockSpec((1, N_GATE), lambda b, c: (0, 0)),
        ],
        out_specs=[
            pl.BlockSpec((ML_CHUNK, N_GATE), lambda b, c: (b * nc + c, 0)),
            pl.BlockSpec((None, GT_ROWS, ML_CHUNK), lambda b, c: (b, 0, c)),
        ],
        out_shape=[jax.ShapeDtypeStruct((bsz * s, N_GATE), F32), jax.ShapeDtypeStruct((bsz, GT_ROWS, s), F32)],
        scratch_shapes=[pltpu.VMEM((1, N_GATE), F32)],
        compiler_params=_cparams("parallel", "arbitrary"),
        name="gates",
    )(g, bias)


def _head_masks(q2):
    lane = lax.broadcasted_iota(jnp.int32, q2.shape, 1)
    zero = jnp.zeros_like(q2)
    return jnp.where(lane < HEAD_DIM, q2, zero), jnp.where(lane >= HEAD_DIM, q2, zero)


def _fox_kernel(q_ref, k_ref, v_ref, g_ref, gt_ref, o_ref, *, tq):
    p = pl.program_id(1)
    qi = pl.program_id(2)
    q2 = q_ref[...] * jnp.asarray(HEAD_DIM ** -0.5, BF16)
    qm = _head_masks(q2)
    gq = g_ref[...]
    lane = lax.broadcasted_iota(jnp.int32, (tq, LANES), 1)
    ct = [jnp.sum(jnp.where(lane == G_FOX + 2 * p + hh, gq, 0.0), axis=1, keepdims=True) for hh in range(2)]

    def step(j, carry, masked):
        start = pl.multiple_of(j * tq, tq)
        k2 = k_ref[pl.ds(start, tq), :]
        v2 = v_ref[pl.ds(start, tq), :]
        new = []
        for hh in range(2):
            m, l, acc = carry[hh]
            cs = gt_ref[pl.ds(G_FOX + 2 * p + hh, 1), pl.ds(start, tq)]
            s = _dot_nt(qm[hh], k2) + (ct[hh] - cs)
            if masked:
                rq = lax.broadcasted_iota(jnp.int32, (tq, tq), 0)
                rk = lax.broadcasted_iota(jnp.int32, (tq, tq), 1)
                s = jnp.where(rk <= rq, s, NEG)
            m_new = jnp.maximum(m, jnp.max(s, axis=1, keepdims=True))
            alpha = jnp.exp(m - m_new)
            pr = jnp.exp(s - m_new)
            l = alpha * l + jnp.sum(pr, axis=1, keepdims=True)
            acc = alpha * acc + _dot(pr.astype(BF16), v2)
            new.append((m_new, l, acc))
        return tuple(new)

    init = tuple((jnp.full((tq, 1), NEG, F32), jnp.zeros((tq, 1), F32), jnp.zeros((tq, LANES), F32))
                 for _ in range(2))
    carry = lax.fori_loop(0, qi, lambda j, c: step(j, c, False), init)
    carry = step(qi, carry, True)
    o0 = carry[0][2] / carry[0][1]
    o1 = carry[1][2] / carry[1][1]
    o_ref[...] = jnp.where(lane < HEAD_DIM, o0, o1).astype(BF16)


def _fox(u, g, gt, bsz, s, tq=256):
    nq = s // tq
    return pl.pallas_call(
        functools.partial(_fox_kernel, tq=tq),
        grid=(bsz, FOX_HEADS // 2, nq),
        in_specs=[
            pl.BlockSpec((tq, PAIR), lambda b, p, i: (b * nq + i, FOX_Q + p)),
            pl.BlockSpec((s, PAIR), lambda b, p, i: (b, FOX_K + p)),
            pl.BlockSpec((s, PAIR), lambda b, p, i: (b, FOX_V + p)),
            pl.BlockSpec((tq, N_GATE), lambda b, p, i: (b * nq + i, 0)),
            pl.BlockSpec((None, 8, s), lambda b, p, i: (b, 0, 0)),
        ],
        out_specs=pl.BlockSpec((tq, PAIR), lambda b, p, i: (b * nq + i, p)),
        out_shape=jax.ShapeDtypeStruct((bsz * s, D_FOX), BF16),
        compiler_params=_cparams("parallel", "parallel", "arbitrary"),
        name="fox",
    )(u, u, u, g, gt)


def _sb_kernel(q_ref, k_ref, v_ref, o_ref, *, tq, tk):
    qi = pl.program_id(2)
    q2 = q_ref[...] * jnp.asarray(HEAD_DIM ** -0.5, BF16)
    qm = _head_masks(q2)
    lane = lax.broadcasted_iota(jnp.int32, (tq, LANES), 1)
    r = lax.broadcasted_iota(jnp.int32, (2 * tk, tk), 0)
    c = lax.broadcasted_iota(jnp.int32, (2 * tk, tk), 1)
    suffix = jnp.where(jnp.where(r >= tk, r - tk, r) >= c, 1.0, 0.0).astype(BF16)
    ratio = tq // tk

    def step(j, carry, masked):
        start = pl.multiple_of(j * tk, tk)
        k2 = k_ref[pl.ds(start, tk), :]
        v2 = v_ref[pl.ds(start, tk), :]
        if masked:
            qpos = qi * tq + lax.broadcasted_iota(jnp.int32, (tq, tk), 0)
            kpos = start + lax.broadcasted_iota(jnp.int32, (tq, tk), 1)
            valid = kpos < qpos
        new = []
        for hh in range(2):
            run, acc = carry[hh]
            z = _dot_nt(qm[hh], k2)
            lk = -(jnp.maximum(z, 0.0) + jnp.log(1.0 + jnp.exp(-jnp.abs(z))))
            if masked:
                lk = jnp.where(valid, lk, 0.0)
            hi = lk.astype(BF16)
            lo = (lk - hi.astype(F32)).astype(BF16)
            rs = _dot(jnp.concatenate([hi, lo], axis=1), suffix)
            a = jnp.exp(z + rs + run)
            if masked:
                a = jnp.where(valid, a, 0.0)
            acc = acc + _dot(a.astype(BF16), v2)
            run = run + jnp.sum(lk, axis=1, keepdims=True)
            new.append((run, acc))
        return tuple(new)

    carry = tuple((jnp.zeros((tq, 1), F32), jnp.zeros((tq, LANES), F32)) for _ in range(2))
    last = ratio * qi + ratio - 1
    for d in range(ratio):
        carry = step(last - d, carry, True)
    carry = lax.fori_loop(0, ratio * qi, lambda i, cr: step(ratio * qi - 1 - i, cr, False), carry)
    o_ref[...] = jnp.where(lane < HEAD_DIM, carry[0][1], carry[1][1]).astype(BF16)


def _sb(u, bsz, s, tq=256, tk=128):
    nq = s // tq
    return pl.pallas_call(
        functools.partial(_sb_kernel, tq=tq, tk=tk),
        grid=(bsz, SB_HEADS // 2, nq),
        in_specs=[
            pl.BlockSpec((tq, PAIR), lambda b, p, i: (b * nq + i, SB_Q + p)),
            pl.BlockSpec((s, PAIR), lambda b, p, i: (b, SB_K + p)),
            pl.BlockSpec((s, PAIR), lambda b, p, i: (b, SB_V + p)),
        ],
        out_specs=pl.BlockSpec((tq, PAIR), lambda b, p, i: (b * nq + i, p)),
        out_shape=jax.ShapeDtypeStruct((bsz * s, D_SB), BF16),
        compiler_params=_cparams("parallel", "parallel", "arbitrary"),
        name="sb",
    )(u, u, u)


def _mlstm_kernel(xq_ref, xk_ref, v_ref, og_ref, g_ref, gt_ref, cw_ref, nw_ref, o_ref, qs_ref, ks_ref, *, s):
    L = ML_CHUNK
    nc = s // L
    row = lax.broadcasted_iota(jnp.int32, (L, D_ML), 0)

    def conv_chunk(c, _):
        start = pl.multiple_of(c * L, L)
        prev_start = pl.multiple_of(jnp.maximum(c - 1, 0) * L, L)
        for x_ref, dst_ref, w0, scale in ((xq_ref, qs_ref, 0, 1.0), (xk_ref, ks_ref, D_ML, HEAD_DIM ** -0.5)):
            cur = x_ref[pl.ds(start, L), :].astype(F32)
            prev = x_ref[pl.ds(prev_start, L), :].astype(F32)
            prev = jnp.where(c > 0, prev, 0.0)
            w = cw_ref[:, w0:w0 + D_ML]
            y = cur * w[CONV_WIDTH - 1:CONV_WIDTH, :]
            for k in range(1, CONV_WIDTH):
                sh = jnp.where(row < k, pltpu.roll(prev, k, 0), pltpu.roll(cur, k, 0))
                y = y + sh * w[CONV_WIDTH - 1 - k:CONV_WIDTH - k, :]
            y = y * jax.nn.sigmoid(y) * scale
            dst_ref[pl.ds(start, L), :] = y.astype(BF16)
        return 0

    lax.fori_loop(0, nc, conv_chunk, 0)

    lane = lax.broadcasted_iota(jnp.int32, (L, LANES), 1)
    lo_lanes = lane < HEAD_DIM
    r_ll = lax.broadcasted_iota(jnp.int32, (L, L), 0)
    c_ll = lax.broadcasted_iota(jnp.int32, (L, L), 1)
    tri = r_ll >= c_ll
    blockdiag = (r_ll < HEAD_DIM) == (c_ll < HEAD_DIM)
    lane_row = lax.broadcasted_iota(jnp.int32, (1, LANES), 1) < HEAD_DIM
    sub_col = lax.broadcasted_iota(jnp.int32, (LANES, 1), 0) < HEAD_DIM

    def chunk(c, carry):
        start = pl.multiple_of(c * L, L)
        gc = g_ref[pl.ds(start, L), :]
        gtc = gt_ref[:, pl.ds(start, L)]
        new_carry = []
        for p in range(ML_HEADS // 2):
            c2, n2, m_pair = carry[p]
            sl = slice(p * PAIR, (p + 1) * PAIR)
            q2 = qs_ref[pl.ds(start, L), sl]
            k2 = ks_ref[pl.ds(start, L), sl]
            v2 = v_ref[pl.ds(start, L), sl]
            qm = _head_masks(q2)
            c2b = c2.astype(BF16)
            hs, ws, decays, m_out = [], [], [], []
            for hh in range(2):
                h = 2 * p + hh
                m_st = m_pair[hh]
                b_col = gc[:, G_MLF + h:G_MLF + h + 1]
                i_col = gc[:, G_MLI + h:G_MLI + h + 1]
                b_row = gtc[G_MLF + h:G_MLF + h + 1, :]
                i_row = gtc[G_MLI + h:G_MLI + h + 1, :]
                g_tot = b_row[:, L - 1:L]
                dlog = jnp.where(tri, b_col - b_row + i_row, -jnp.inf)
                inter = b_col + m_st
                m_t = jnp.maximum(inter, jnp.max(dlog, axis=1, keepdims=True))
                d = jnp.exp(dlog - m_t)
                sc = _dot_nt(qm[hh], k2) * d
                w_inter = jnp.exp(inter - m_t)
                num = _dot(sc.astype(BF16), v2) + w_inter * _dot(qm[hh], c2b)
                qn = jnp.sum(qm[hh].astype(F32) * n2, axis=1, keepdims=True)
                den = jnp.sum(sc, axis=1, keepdims=True) + w_inter * qn
                hs.append(num / jnp.maximum(jnp.abs(den), jnp.exp(-m_t)))
                m_new = jnp.maximum(g_tot + m_st, jnp.max(g_tot - b_row + i_row, axis=1, keepdims=True))
                decays.append(jnp.exp(g_tot + m_st - m_new))
                ws.append(jnp.exp(g_tot - b_col + i_col - m_new))
                m_out.append(m_new)
            kw = k2.astype(F32) * jnp.where(lo_lanes, ws[0], ws[1])
            upd = _dot_tn(kw.astype(BF16), v2)
            c2 = jnp.where(sub_col, decays[0], decays[1]) * c2 + jnp.where(blockdiag, upd, 0.0)
            n2 = jnp.where(lane_row, decays[0], decays[1]) * n2 + jnp.sum(kw, axis=0, keepdims=True)
            new_carry.append((c2, n2, tuple(m_out)))
            hp = jnp.where(lo_lanes, hs[0], hs[1])
            sq = hp * hp
            ms0 = jnp.sum(jnp.where(lo_lanes, sq, 0.0), axis=1, keepdims=True)
            ms1 = jnp.sum(jnp.where(lo_lanes, 0.0, sq), axis=1, keepdims=True)
            ms = jnp.where(lo_lanes, ms0, ms1) * (1.0 / HEAD_DIM)
            hp = hp * lax.rsqrt(ms + RMS_EPS) * nw_ref[:, sl]
            gate = jax.nn.sigmoid(og_ref[pl.ds(start, L), sl].astype(F32))
            o_ref[pl.ds(start, L), sl] = (gate * hp).astype(BF16)
        return tuple(new_carry)

    init = tuple((jnp.zeros((LANES, LANES), F32), jnp.zeros((1, LANES), F32),
                  (jnp.zeros((1, 1), F32), jnp.zeros((1, 1), F32))) for _ in range(ML_HEADS // 2))
    lax.fori_loop(0, nc, chunk, init)


def _mlstm(u, g, gt, conv_w, norm_w, bsz, s):
    blk = lambda j: pl.BlockSpec((s, D_ML), lambda b: (b, j))
    first = (3 * D_FOX + 3 * D_SB) // D_ML
    return pl.pallas_call(
        functools.partial(_mlstm_kernel, s=s),
        grid=(bsz,),
        in_specs=[
            blk(first), blk(first + 1), blk(first + 2), blk(first + 3),
            pl.BlockSpec((s, N_GATE), lambda b: (b, 0)),
            pl.BlockSpec((None, GT_ROWS, s), lambda b: (b, 0, 0)),
            pl.BlockSpec((CONV_WIDTH, 2 * D_ML), lambda b: (0, 0)),
            pl.BlockSpec((1, D_ML), lambda b: (0, 0)),
        ],
        out_specs=pl.BlockSpec((s, D_ML), lambda b: (b, 0)),
        out_shape=jax.ShapeDtypeStruct((bsz * s, D_ML), BF16),
        scratch_shapes=[pltpu.VMEM((s, D_ML), BF16), pltpu.VMEM((s, D_ML), BF16)],
        compiler_params=_cparams("parallel"),
        name="mlstm",
    )(u, u, u, u, g, gt, conv_w, norm_w)


def _outproj_kernel(yf_ref, ys_ref, ym_ref, wf_ref, ws_ref, wm_ref, h_ref, o_ref):
    acc = _dot(yf_ref[...], wf_ref[...]) + _dot(ys_ref[...], ws_ref[...]) + _dot(ym_ref[...], wm_ref[...])
    o_ref[...] = h_ref[...] + acc


def _outproj(yf, ys, ym, wf, ws, wm, h, tm=512):
    t = h.shape[0]
    row = lambda n: pl.BlockSpec((tm, n), lambda i: (i, 0))
    full = lambda n: pl.BlockSpec((n, D_MODEL), lambda i: (0, 0))
    return pl.pallas_call(
        _outproj_kernel,
        grid=(t // tm,),
        in_specs=[row(D_FOX), row(D_SB), row(D_ML), full(D_FOX), full(D_SB), full(D_ML), row(D_MODEL)],
        out_specs=row(D_MODEL),
        out_shape=jax.ShapeDtypeStruct((t, D_MODEL), F32),
        compiler_params=_cparams("parallel"),
        name="outproj",
    )(yf, ys, ym, wf, ws, wm, h)


def _memkv_kernel(x_ref, nw_ref, w_ref, o_ref):
    xn = _rms(x_ref[...], nw_ref[...]).astype(BF16)
    o_ref[...] = _dot(xn, w_ref[...]).astype(BF16)


def _memkv(mem2d, nw, w, tm=512):
    t = mem2d.shape[0]
    return pl.pallas_call(
        _memkv_kernel,
        grid=(t // tm,),
        in_specs=[
            pl.BlockSpec((tm, D_MODEL), lambda i: (i, 0)),
            pl.BlockSpec((1, D_MODEL), lambda i: (0, 0)),
            pl.BlockSpec((D_MODEL, 2 * D_X), lambda i: (0, 0)),
        ],
        out_specs=pl.BlockSpec((tm, 2 * D_X), lambda i: (i, 0)),
        out_shape=jax.ShapeDtypeStruct((t, 2 * D_X), BF16),
        compiler_params=_cparams("parallel"),
        name="memkv",
    )(mem2d, nw, w)


def _cross_kernel(h_ref, nw_ref, wq_ref, kv_ref, wo_ref, o_ref):
    hf = h_ref[...]
    hn = _rms(hf, nw_ref[...]).astype(BF16)
    q = _dot(hn, wq_ref[...]).astype(BF16)
    outs = []
    for hd in range(X_HEADS):
        sl = slice(hd * X_HEAD_DIM, (hd + 1) * X_HEAD_DIM)
        k = kv_ref[:, sl]
        v = kv_ref[:, D_X + hd * X_HEAD_DIM:D_X + (hd + 1) * X_HEAD_DIM]
        s = _dot_nt(q[:, sl], k) * (X_HEAD_DIM ** -0.5)
        m = jnp.max(s, axis=1, keepdims=True)
        e = jnp.exp(s - m)
        pr = e / jnp.sum(e, axis=1, keepdims=True)
        outs.append(_dot(pr.astype(BF16), v).astype(BF16))
    o = jnp.concatenate(outs, axis=1)
    o_ref[...] = hf + _dot(o, wo_ref[...])


def _cross(h, nw, wq, kv, wo, bsz, s, mlen, tm=512):
    nt = s // tm
    return pl.pallas_call(
        _cross_kernel,
        grid=(bsz, nt),
        in_specs=[
            pl.BlockSpec((tm, D_MODEL), lambda b, i: (b * nt + i, 0)),
            pl.BlockSpec((1, D_MODEL), lambda b, i: (0, 0)),
            pl.BlockSpec((D_MODEL, D_X), lambda b, i: (0, 0)),
            pl.BlockSpec((mlen, 2 * D_X), lambda b, i: (b, 0)),
            pl.BlockSpec((D_X, D_MODEL), lambda b, i: (0, 0)),
        ],
        out_specs=pl.BlockSpec((tm, D_MODEL), lambda b, i: (b * nt + i, 0)),
        out_shape=jax.ShapeDtypeStruct((bsz * s, D_MODEL), F32),
        compiler_params=_cparams("parallel", "parallel"),
        name="cross",
    )(h, nw, wq, kv, wo)


def _ffn_kernel(h_ref, nw_ref, wg_ref, wu_ref, wd_ref, fw_ref, o_ref, acc_ref, *, f_chunk, final_norm):
    hf = h_ref[...]
    hn = _rms(hf, nw_ref[...]).astype(BF16)
    for c in range(D_FF // f_chunk):
        sl = slice(c * f_chunk, (c + 1) * f_chunk)
        g = _dot(hn, wg_ref[:, sl])
        u = _dot(hn, wu_ref[:, sl])
        a = (g * jax.nn.sigmoid(g) * u).astype(BF16)
        part = _dot(a, wd_ref[sl, :])
        if c == 0:
            acc_ref[...] = part
        else:
            acc_ref[...] += part
    out = hf + acc_ref[...]
    if final_norm:
        out = _rms(out, fw_ref[...])
    o_ref[...] = out


def _ffn(h, nw, wg, wu, wd, fw, final_norm, tm=512):
    t = h.shape[0]
    const = lambda shape: pl.BlockSpec(shape, lambda i: (0, 0))
    return pl.pallas_call(
        functools.partial(_ffn_kernel, f_chunk=256, final_norm=final_norm),
        grid=(t // tm,),
        in_specs=[
            pl.BlockSpec((tm, D_MODEL), lambda i: (i, 0)),
            const((1, D_MODEL)),
            const((D_MODEL, D_FF)), const((D_MODEL, D_FF)), const((D_FF, D_MODEL)),
            const((1, D_MODEL)),
        ],
        out_specs=pl.BlockSpec((tm, D_MODEL), lambda i: (i, 0)),
        out_shape=jax.ShapeDtypeStruct((t, D_MODEL), F32),
        scratch_shapes=[pltpu.VMEM((tm, D_MODEL), F32)],
        compiler_params=_cparams("parallel"),
        name="ffn",
    )(h, nw, wg, wu, wd, fw)


def _split_w_in(w):
    o = 0
    parts = {}
    for name, n in (("fox_qkv", 3 * D_FOX), ("fox_f", FOX_HEADS), ("sb_qkv", 3 * D_SB), ("ml_main", 4 * D_ML),
                    ("ml_i", ML_HEADS), ("ml_f", ML_HEADS)):
        parts[name] = w[:, o:o + n]
        o += n
    main = jnp.concatenate([parts["fox_qkv"], parts["sb_qkv"], parts["ml_main"]], axis=1).astype(BF16)
    gate = jnp.concatenate([parts["fox_f"], parts["ml_i"], parts["ml_f"]], axis=1)
    gate = jnp.pad(gate, ((0, 0), (0, N_GATE - gate.shape[1]))).astype(BF16)
    return main, gate


def kernel(x, mem, norm_mix_w, w_in, fox_f_b, ml_conv_w, ml_i_b, ml_f_b, ml_norm_w, w_out, norm_x_w, mem_norm_w,
           wx_q, wx_kv, wx_o, norm_ffn_w, w_gate, w_up, w_down, final_norm_w):
    bsz, s, _ = x.shape
    mlen = mem.shape[1]
    depth = w_in.shape[0]
    h = x.reshape(bsz * s, D_MODEL)
    mem2d = mem.reshape(bsz * mlen, D_MODEL)
    row = lambda v: v.reshape(1, -1)
    for l in range(depth):
        w_main, w_g = _split_w_in(w_in[l])
        bias = jnp.pad(jnp.concatenate([fox_f_b[l], ml_i_b[l], ml_f_b[l]]), (0, N_GATE - 3 * FOX_HEADS)).reshape(1, -1)
        u, g_raw = _inproj(h, row(norm_mix_w[l]), w_main, w_g)
        g, gt = _gates(g_raw, bias, bsz, s)
        y_fox = _fox(u, g, gt, bsz, s)
        y_sb = _sb(u, bsz, s)
        y_ml = _mlstm(u, g, gt, ml_conv_w[l], row(ml_norm_w[l]), bsz, s)
        wo = w_out[l].astype(BF16)
        h = _outproj(y_fox, y_sb, y_ml, wo[:D_FOX], wo[D_FOX:D_FOX + D_SB], wo[D_FOX + D_SB:], h)
        kv = _memkv(mem2d, row(mem_norm_w[l]), wx_kv[l].astype(BF16), tm=min(512, bsz * mlen))
        h = _cross(h, row(norm_x_w[l]), wx_q[l].astype(BF16), kv, wx_o[l].astype(BF16), bsz, s, mlen)
        h = _ffn(h, row(norm_ffn_w[l]), w_gate[l].astype(BF16), w_up[l].astype(BF16), w_down[l].astype(BF16),
                 row(final_norm_w), final_norm=(l == depth - 1))
    return h.reshape(bsz, s, D_MODEL)
```

```python
import functools

import jax
import jax.numpy as jnp
from jax import lax
from jax.experimental import pallas as pl
from jax.experimental.pallas import tpu as pltpu

D_MODEL = 1024
HEAD_DIM = 64
FOX_HEADS = 6
SB_HEADS = 4
ML_HEADS = 6
D_FOX = FOX_HEADS * HEAD_DIM
D_SB = SB_HEADS * HEAD_DIM
D_ML = ML_HEADS * HEAD_DIM
ML_CHUNK = 128
CONV_WIDTH = 4
X_HEADS = 4
X_HEAD_DIM = 128
D_X = X_HEADS * X_HEAD_DIM
D_FF = 2816
RMS_EPS = 1e-6

LANES = 128
PAIR = 2 * HEAD_DIM
D_MAIN = 3 * D_FOX + 3 * D_SB + 4 * D_ML
N_GATE = LANES
GT_ROWS = 32
NEG = -1e30
SB_SUB = 128
SB_EXIT = -110.0
VMEM_LIMIT = 56 * 1024 * 1024

BF16 = jnp.bfloat16
F32 = jnp.float32

FOX_Q, FOX_K, FOX_V = 0, 3, 6
SB_Q, SB_K, SB_V = 9, 11, 13
G_FOX, G_MLI, G_MLF = 0, 6, 12


def _cparams(*sem):
    return pltpu.CompilerParams(dimension_semantics=sem, vmem_limit_bytes=VMEM_LIMIT)


def _rms(xf, w):
    ms = jnp.mean(xf * xf, axis=-1, keepdims=True)
    return xf * lax.rsqrt(ms + RMS_EPS) * w


def _dot(a, b):
    return jnp.dot(a, b, preferred_element_type=F32)


def _dot_nt(a, b):
    return lax.dot_general(a, b, (((1,), (1,)), ((), ())), preferred_element_type=F32)


def _dot_tn(a, b):
    return lax.dot_general(a, b, (((0,), (0,)), ((), ())), preferred_element_type=F32)


def _log_sigmoid(x):
    return jnp.minimum(x, 0.0) - jnp.log(1.0 + jnp.exp(-jnp.abs(x)))


def _inproj_kernel(x_ref, nw_ref, w_ref, wg_ref, u_ref, g_ref, *, n_chunk):
    xn = _rms(x_ref[...], nw_ref[...]).astype(BF16)
    for c in range(D_MAIN // n_chunk):
        sl = slice(c * n_chunk, (c + 1) * n_chunk)
        u_ref[:, sl] = _dot(xn, w_ref[:, sl]).astype(BF16)
    g_ref[...] = _dot(xn, wg_ref[...])


def _inproj(h, nw, w_main, w_gate, tm=512):
    t = h.shape[0]
    return pl.pallas_call(
        functools.partial(_inproj_kernel, n_chunk=1152),
        grid=(t // tm,),
        in_specs=[
            pl.BlockSpec((tm, D_MODEL), lambda i: (i, 0)),
            pl.BlockSpec((1, D_MODEL), lambda i: (0, 0)),
            pl.BlockSpec((D_MODEL, D_MAIN), lambda i: (0, 0)),
            pl.BlockSpec((D_MODEL, N_GATE), lambda i: (0, 0)),
        ],
        out_specs=[
            pl.BlockSpec((tm, D_MAIN), lambda i: (i, 0)),
            pl.BlockSpec((tm, N_GATE), lambda i: (i, 0)),
        ],
        out_shape=[jax.ShapeDtypeStruct((t, D_MAIN), BF16), jax.ShapeDtypeStruct((t, N_GATE), F32)],
        compiler_params=_cparams("parallel"),
        name="inproj",
    )(h, nw, w_main, w_gate)


def _gates_kernel(g_ref, b_ref, go_ref, gt_ref, *, s):
    L = ML_CHUNK
    col = lax.broadcasted_iota(jnp.int32, (L, LANES), 1)
    is_i = (col >= G_MLI) & (col < G_MLF)
    r = lax.broadcasted_iota(jnp.int32, (L, L), 0)
    c = lax.broadcasted_iota(jnp.int32, (L, L), 1)
    tri = jnp.where(r >= c, 1.0, 0.0).astype(BF16)
    bias = b_ref[...]

    def chunk(ci, carry):
        start = pl.multiple_of(ci * L, L)
        pre = g_ref[pl.ds(start, L), :] + bias
        x = jnp.where(is_i, 0.0, _log_sigmoid(pre))
        x1 = x.astype(BF16)
        r1 = x - x1.astype(F32)
        x2 = r1.astype(BF16)
        x3 = (r1 - x2.astype(F32)).astype(BF16)
        cum = _dot(tri, x1) + _dot(tri, x2) + _dot(tri, x3)
        tot = cum + carry
        out = jnp.where(col < G_MLI, tot, jnp.where(is_i, pre, cum))
        go_ref[pl.ds(start, L), :] = out
        gt_ref[:, pl.ds(start, L)] = out.T[:GT_ROWS, :]
        return tot[L - 1:L, :]

    lax.fori_loop(0, s // L, chunk, jnp.zeros((1, LANES), F32))


def _gates(g, bias, bsz, s):
    return pl.pallas_call(
        functools.partial(_gates_kernel, s=s),
        grid=(bsz,),
        in_specs=[
            pl.BlockSpec((s, N_GATE), lambda b: (b, 0)),
            pl.BlockSpec((1, N_GATE), lambda b: (0, 0)),
        ],
        out_specs=[
            pl.BlockSpec((s, N_GATE), lambda b: (b, 0)),
            pl.BlockSpec((None, GT_ROWS, s), lambda b: (b, 0, 0)),
        ],
        out_shape=[jax.ShapeDtypeStruct((bsz * s, N_GATE), F32), jax.ShapeDtypeStruct((bsz, GT_ROWS, s), F32)],
        compiler_params=_cparams("parallel"),
        name="gates",
    )(g, bias)


def _head_masks(q2):
    lane = lax.broadcasted_iota(jnp.int32, q2.shape, 1)
    zero = jnp.zeros_like(q2)
    return jnp.where(lane < HEAD_DIM, q2, zero), jnp.where(lane >= HEAD_DIM, q2, zero)


def _stack_heads(q2):
    return jnp.concatenate(_head_masks(q2), axis=0)


def _unstack_heads(x, rows):
    lane = lax.broadcasted_iota(jnp.int32, (rows, LANES), 1)
    return jnp.where(lane < HEAD_DIM, x[:rows], x[rows:])


def _fox_kernel(q_ref, k_ref, v_ref, g_ref, gt_ref, o_ref, m_ref, l_ref, acc_ref, *, tq):
    p = pl.program_id(1)
    qi = pl.program_id(2)
    qs = _stack_heads(q_ref[...] * jnp.asarray(HEAD_DIM ** -0.5, BF16))
    gq = g_ref[...]
    lane = lax.broadcasted_iota(jnp.int32, (tq, LANES), 1)
    ct = [jnp.sum(jnp.where(lane == G_FOX + 2 * p + hh, gq, 0.0), axis=1, keepdims=True) for hh in range(2)]
    m_ref[...] = jnp.full_like(m_ref, NEG)
    l_ref[...] = jnp.zeros_like(l_ref)
    acc_ref[...] = jnp.zeros_like(acc_ref)

    def group(j, masked):
        start = pl.multiple_of(j * tq, tq)
        s = _dot_nt(qs, k_ref[pl.ds(start, tq), :])
        halves = []
        for hh in range(2):
            cs = gt_ref[pl.ds(G_FOX + 2 * p + hh, 1), pl.ds(start, tq)]
            sh = s[hh * tq:(hh + 1) * tq] + (ct[hh] - cs)
            if masked:
                rq = lax.broadcasted_iota(jnp.int32, (tq, tq), 0)
                rk = lax.broadcasted_iota(jnp.int32, (tq, tq), 1)
                sh = jnp.where(rk <= rq, sh, NEG)
            halves.append(sh)
        s = jnp.concatenate(halves, axis=0)
        m_old = m_ref[...]
        m_new = jnp.maximum(m_old, jnp.max(s, axis=1, keepdims=True))
        alpha = jnp.exp(m_old - m_new)
        pr = jnp.exp(s - m_new)
        l_ref[...] = alpha * l_ref[...] + jnp.sum(pr, axis=1, keepdims=True)
        acc_ref[...] = alpha * acc_ref[...] + _dot(pr.astype(BF16), v_ref[pl.ds(start, tq), :])
        m_ref[...] = m_new

    def body(j, carry):
        group(j, False)
        return carry

    lax.fori_loop(0, qi, body, 0)
    group(qi, True)
    o_ref[...] = _unstack_heads(acc_ref[...] / l_ref[...], tq).astype(BF16)


def _fox(u, g, gt, bsz, s, tq=512):
    nq = s // tq
    return pl.pallas_call(
        functools.partial(_fox_kernel, tq=tq),
        grid=(bsz, FOX_HEADS // 2, nq),
        in_specs=[
            pl.BlockSpec((tq, PAIR), lambda b, p, i: (b * nq + i, FOX_Q + p)),
            pl.BlockSpec((s, PAIR), lambda b, p, i: (b, FOX_K + p)),
            pl.BlockSpec((s, PAIR), lambda b, p, i: (b, FOX_V + p)),
            pl.BlockSpec((tq, N_GATE), lambda b, p, i: (b * nq + i, 0)),
            pl.BlockSpec((None, 8, s), lambda b, p, i: (b, 0, 0)),
        ],
        out_specs=pl.BlockSpec((tq, PAIR), lambda b, p, i: (b * nq + i, p)),
        out_shape=jax.ShapeDtypeStruct((bsz * s, D_FOX), BF16),
        scratch_shapes=[pltpu.VMEM((2 * tq, 1), F32), pltpu.VMEM((2 * tq, 1), F32), pltpu.VMEM((2 * tq, LANES), F32)],
        compiler_params=_cparams("parallel", "parallel", "arbitrary"),
        name="fox",
    )(u, u, u, g, gt)


def _sb_kernel(q_ref, k_ref, v_ref, o_ref, run_ref, acc_ref, *, tq):
    qi = pl.program_id(2)
    qs = _stack_heads(q_ref[...] * jnp.asarray(HEAD_DIM ** -0.5, BF16))
    r = lax.broadcasted_iota(jnp.int32, (2 * SB_SUB, SB_SUB), 0)
    c = lax.broadcasted_iota(jnp.int32, (2 * SB_SUB, SB_SUB), 1)
    suffix = jnp.where(jnp.where(r >= SB_SUB, r - SB_SUB, r) >= c, 1.0, 0.0).astype(BF16)
    run_ref[...] = jnp.zeros_like(run_ref)
    acc_ref[...] = jnp.zeros_like(acc_ref)

    def group(g, masked):
        start = pl.multiple_of((qi - g) * tq, tq)
        z = _dot_nt(qs, k_ref[pl.ds(start, tq), :])
        lk = -(jnp.maximum(z, 0.0) + jnp.log(1.0 + jnp.exp(-jnp.abs(z))))
        if masked:
            rq = lax.broadcasted_iota(jnp.int32, (2 * tq, tq), 0)
            rk = lax.broadcasted_iota(jnp.int32, (2 * tq, tq), 1)
            valid = rk < jnp.where(rq >= tq, rq - tq, rq)
            lk = jnp.where(valid, lk, 0.0)
        hi = lk.astype(BF16)
        lo = (lk - hi.astype(F32)).astype(BF16)
        offs = run_ref[...]
        parts = [None] * (tq // SB_SUB)
        for sb in reversed(range(tq // SB_SUB)):
            sl = slice(sb * SB_SUB, (sb + 1) * SB_SUB)
            rs = _dot(jnp.concatenate([hi[:, sl], lo[:, sl]], axis=1), suffix)
            a = jnp.exp(z[:, sl] + rs + offs)
            if masked:
                a = jnp.where(valid[:, sl], a, 0.0)
            parts[sb] = a.astype(BF16)
            offs = offs + jnp.sum(lk[:, sl], axis=1, keepdims=True)
        run_ref[...] = offs
        acc_ref[...] += _dot(jnp.concatenate(parts, axis=1), v_ref[pl.ds(start, tq), :])
        return jnp.max(offs)

    top = group(0, True)

    def cond(carry):
        g, top = carry
        return (g <= qi) & (top > SB_EXIT)

    def body(carry):
        g, _ = carry
        return g + 1, group(g, False)

    lax.while_loop(cond, body, (jnp.int32(1), top))
    o_ref[...] = _unstack_heads(acc_ref[...], tq).astype(BF16)


def _sb(u, bsz, s, tq=512):
    nq = s // tq
    return pl.pallas_call(
        functools.partial(_sb_kernel, tq=tq),
        grid=(bsz, SB_HEADS // 2, nq),
        in_specs=[
            pl.BlockSpec((tq, PAIR), lambda b, p, i: (b * nq + i, SB_Q + p)),
            pl.BlockSpec((s, PAIR), lambda b, p, i: (b, SB_K + p)),
            pl.BlockSpec((s, PAIR), lambda b, p, i: (b, SB_V + p)),
        ],
        out_specs=pl.BlockSpec((tq, PAIR), lambda b, p, i: (b * nq + i, p)),
        out_shape=jax.ShapeDtypeStruct((bsz * s, D_SB), BF16),
        scratch_shapes=[pltpu.VMEM((2 * tq, 1), F32), pltpu.VMEM((2 * tq, LANES), F32)],
        compiler_params=_cparams("parallel", "parallel", "arbitrary"),
        name="sb",
    )(u, u, u)


def _mlstm_kernel(xq_ref, xk_ref, v_ref, og_ref, g_ref, gt_ref, cw_ref, nw_ref, o_ref, qs_ref, ks_ref, *, s):
    L = ML_CHUNK
    nc = s // L
    row = lax.broadcasted_iota(jnp.int32, (L, D_ML), 0)

    def conv_chunk(c, _):
        start = pl.multiple_of(c * L, L)
        prev_start = pl.multiple_of(jnp.maximum(c - 1, 0) * L, L)
        for x_ref, dst_ref, w0, scale in ((xq_ref, qs_ref, 0, 1.0), (xk_ref, ks_ref, D_ML, HEAD_DIM ** -0.5)):
            cur = x_ref[pl.ds(start, L), :].astype(F32)
            prev = x_ref[pl.ds(prev_start, L), :].astype(F32)
            prev = jnp.where(c > 0, prev, 0.0)
            w = cw_ref[:, w0:w0 + D_ML]
            y = cur * w[CONV_WIDTH - 1:CONV_WIDTH, :]
            for k in range(1, CONV_WIDTH):
                sh = jnp.where(row < k, pltpu.roll(prev, k, 0), pltpu.roll(cur, k, 0))
                y = y + sh * w[CONV_WIDTH - 1 - k:CONV_WIDTH - k, :]
            y = y * jax.nn.sigmoid(y) * scale
            dst_ref[pl.ds(start, L), :] = y.astype(BF16)
        return 0

    lax.fori_loop(0, nc, conv_chunk, 0)

    lane = lax.broadcasted_iota(jnp.int32, (L, LANES), 1)
    lo_lanes = lane < HEAD_DIM
    r_ll = lax.broadcasted_iota(jnp.int32, (L, L), 0)
    c_ll = lax.broadcasted_iota(jnp.int32, (L, L), 1)
    tri = r_ll >= c_ll
    blockdiag = (r_ll < HEAD_DIM) == (c_ll < HEAD_DIM)
    lane_row = lax.broadcasted_iota(jnp.int32, (1, LANES), 1) < HEAD_DIM
    ones = jnp.ones((L, LANES), BF16)

    def chunk(c, carry):
        start = pl.multiple_of(c * L, L)
        gc = g_ref[pl.ds(start, L), :]
        gtc = gt_ref[:, pl.ds(start, L)]
        new_carry = []
        for p in range(ML_HEADS // 2):
            c2, nm, m2 = carry[p]
            sl = slice(p * PAIR, (p + 1) * PAIR)
            q2 = qs_ref[pl.ds(start, L), sl]
            k2 = ks_ref[pl.ds(start, L), sl]
            v2 = v_ref[pl.ds(start, L), sl]
            qk = _dot_nt(_stack_heads(q2), k2)
            sc, m_loc, den_loc, b_col, w_col, g_tot, m_src = [], [], [], [], [], [], []
            for hh in range(2):
                h = 2 * p + hh
                bc = gc[:, G_MLF + h:G_MLF + h + 1]
                ic = gc[:, G_MLI + h:G_MLI + h + 1]
                br = gtc[G_MLF + h:G_MLF + h + 1, :]
                ir = gtc[G_MLI + h:G_MLI + h + 1, :]
                gt_ = br[:, L - 1:L]
                dlog = jnp.where(tri, bc - br + ir, -jnp.inf)
                ml = jnp.max(dlog, axis=1, keepdims=True)
                sch = qk[hh * L:(hh + 1) * L] * jnp.exp(dlog - ml)
                ms = jnp.max(gt_ - br + ir, axis=1, keepdims=True)
                sc.append(sch)
                m_loc.append(ml)
                den_loc.append(jnp.sum(sch, axis=1, keepdims=True))
                b_col.append(bc)
                w_col.append(jnp.exp(gt_ - bc + ic - ms))
                g_tot.append(gt_)
                m_src.append(ms)
            pv = _dot(jnp.concatenate(sc, axis=0).astype(BF16), v2)
            num_loc = jnp.where(lo_lanes, pv[:L], pv[L:])
            pair = lambda xs: jnp.where(lo_lanes, xs[0], xs[1])
            pair_row = lambda xs: jnp.where(lane_row, xs[0], xs[1])
            m_loc2, den2, b2, w2 = pair(m_loc), pair(den_loc), pair(b_col), pair(w_col)
            g2, m_src2 = pair_row(g_tot), pair_row(m_src)
            inter = _dot(q2, jnp.concatenate([c2, nm], axis=1).astype(BF16))
            it2 = b2 + m2
            m_t = jnp.maximum(it2, m_loc2)
            f_loc = jnp.exp(m_loc2 - m_t)
            w_int = jnp.exp(it2 - m_t)
            num = num_loc * f_loc + w_int * inter[:, :LANES]
            den = den2 * f_loc + w_int * inter[:, LANES:]
            hp = num / jnp.maximum(jnp.abs(den), jnp.exp(-m_t))
            sq = hp * hp
            ms0 = jnp.sum(jnp.where(lo_lanes, sq, 0.0), axis=1, keepdims=True)
            ms1 = jnp.sum(jnp.where(lo_lanes, 0.0, sq), axis=1, keepdims=True)
            msq = jnp.where(lo_lanes, ms0, ms1) * (1.0 / HEAD_DIM)
            hp = hp * lax.rsqrt(msq + RMS_EPS) * nw_ref[:, sl]
            gate = jax.nn.sigmoid(og_ref[pl.ds(start, L), sl].astype(F32))
            o_ref[pl.ds(start, L), sl] = (gate * hp).astype(BF16)
            kw = (k2.astype(F32) * w2).astype(BF16)
            upd = _dot_tn(kw, jnp.concatenate([v2, ones], axis=1))
            m_new = jnp.maximum(g2 + m2, m_src2)
            decay = jnp.exp(g2 + m2 - m_new)
            scale = jnp.exp(m_src2 - m_new)
            c2 = decay * c2 + scale * jnp.where(blockdiag, upd[:, :LANES], 0.0)
            nm = decay * nm + scale * jnp.where(blockdiag, upd[:, LANES:], 0.0)
            new_carry.append((c2, nm, m_new))
        return tuple(new_carry)

    init = tuple((jnp.zeros((LANES, LANES), F32), jnp.zeros((LANES, LANES), F32), jnp.zeros((1, LANES), F32))
                 for _ in range(ML_HEADS // 2))
    lax.fori_loop(0, nc, chunk, init)


def _mlstm(u, g, gt, conv_w, norm_w, bsz, s):
    blk = lambda j: pl.BlockSpec((s, D_ML), lambda b: (b, j))
    first = (3 * D_FOX + 3 * D_SB) // D_ML
    return pl.pallas_call(
        functools.partial(_mlstm_kernel, s=s),
        grid=(bsz,),
        in_specs=[
            blk(first), blk(first + 1), blk(first + 2), blk(first + 3),
            pl.BlockSpec((s, N_GATE), lambda b: (b, 0)),
            pl.BlockSpec((None, GT_ROWS, s), lambda b: (b, 0, 0)),
            pl.BlockSpec((CONV_WIDTH, 2 * D_ML), lambda b: (0, 0)),
            pl.BlockSpec((1, D_ML), lambda b: (0, 0)),
        ],
        out_specs=pl.BlockSpec((s, D_ML), lambda b: (b, 0)),
        out_shape=jax.ShapeDtypeStruct((bsz * s, D_ML), BF16),
        scratch_shapes=[pltpu.VMEM((s, D_ML), BF16), pltpu.VMEM((s, D_ML), BF16)],
        compiler_params=_cparams("parallel"),
        name="mlstm",
    )(u, u, u, u, g, gt, conv_w, norm_w)


def _outproj_kernel(yf_ref, ys_ref, ym_ref, wf_ref, ws_ref, wm_ref, h_ref, o_ref):
    acc = _dot(yf_ref[...], wf_ref[...]) + _dot(ys_ref[...], ws_ref[...]) + _dot(ym_ref[...], wm_ref[...])
    o_ref[...] = h_ref[...] + acc


def _outproj(yf, ys, ym, wf, ws, wm, h, tm=512):
    t = h.shape[0]
    row = lambda n: pl.BlockSpec((tm, n), lambda i: (i, 0))
    full = lambda n: pl.BlockSpec((n, D_MODEL), lambda i: (0, 0))
    return pl.pallas_call(
        _outproj_kernel,
        grid=(t // tm,),
        in_specs=[row(D_FOX), row(D_SB), row(D_ML), full(D_FOX), full(D_SB), full(D_ML), row(D_MODEL)],
        out_specs=row(D_MODEL),
        out_shape=jax.ShapeDtypeStruct((t, D_MODEL), F32),
        compiler_params=_cparams("parallel"),
        name="outproj",
    )(yf, ys, ym, wf, ws, wm, h)


def _memkv_kernel(x_ref, nw_ref, w_ref, o_ref):
    xn = _rms(x_ref[...], nw_ref[...]).astype(BF16)
    o_ref[...] = _dot(xn, w_ref[...]).astype(BF16)


def _memkv(mem2d, nw, w, tm=512):
    t = mem2d.shape[0]
    return pl.pallas_call(
        _memkv_kernel,
        grid=(t // tm,),
        in_specs=[
            pl.BlockSpec((tm, D_MODEL), lambda i: (i, 0)),
            pl.BlockSpec((1, D_MODEL), lambda i: (0, 0)),
            pl.BlockSpec((D_MODEL, 2 * D_X), lambda i: (0, 0)),
        ],
        out_specs=pl.BlockSpec((tm, 2 * D_X), lambda i: (i, 0)),
        out_shape=jax.ShapeDtypeStruct((t, 2 * D_X), BF16),
        compiler_params=_cparams("parallel"),
        name="memkv",
    )(mem2d, nw, w)


def _cross_kernel(h_ref, nw_ref, wq_ref, kv_ref, wo_ref, o_ref):
    hf = h_ref[...]
    hn = _rms(hf, nw_ref[...]).astype(BF16)
    q = _dot(hn, wq_ref[...]).astype(BF16)
    outs = []
    for hd in range(X_HEADS):
        sl = slice(hd * X_HEAD_DIM, (hd + 1) * X_HEAD_DIM)
        k = kv_ref[:, sl]
        v = kv_ref[:, D_X + hd * X_HEAD_DIM:D_X + (hd + 1) * X_HEAD_DIM]
        s = _dot_nt(q[:, sl], k) * (X_HEAD_DIM ** -0.5)
        m = jnp.max(s, axis=1, keepdims=True)
        e = jnp.exp(s - m)
        pr = e / jnp.sum(e, axis=1, keepdims=True)
        outs.append(_dot(pr.astype(BF16), v).astype(BF16))
    o = jnp.concatenate(outs, axis=1)
    o_ref[...] = hf + _dot(o, wo_ref[...])


def _cross(h, nw, wq, kv, wo, bsz, s, mlen, tm=512):
    nt = s // tm
    return pl.pallas_call(
        _cross_kernel,
        grid=(bsz, nt),
        in_specs=[
            pl.BlockSpec((tm, D_MODEL), lambda b, i: (b * nt + i, 0)),
            pl.BlockSpec((1, D_MODEL), lambda b, i: (0, 0)),
            pl.BlockSpec((D_MODEL, D_X), lambda b, i: (0, 0)),
            pl.BlockSpec((mlen, 2 * D_X), lambda b, i: (b, 0)),
            pl.BlockSpec((D_X, D_MODEL), lambda b, i: (0, 0)),
        ],
        out_specs=pl.BlockSpec((tm, D_MODEL), lambda b, i: (b * nt + i, 0)),
        out_shape=jax.ShapeDtypeStruct((bsz * s, D_MODEL), F32),
        compiler_params=_cparams("parallel", "parallel"),
        name="cross",
    )(h, nw, wq, kv, wo)


def _ffn_kernel(h_ref, nw_ref, wg_ref, wu_ref, wd_ref, fw_ref, o_ref, acc_ref, *, f_chunk, final_norm):
    hf = h_ref[...]
    hn = _rms(hf, nw_ref[...]).astype(BF16)
    for c in range(D_FF // f_chunk):
        sl = slice(c * f_chunk, (c + 1) * f_chunk)
        g = _dot(hn, wg_ref[:, sl])
        u = _dot(hn, wu_ref[:, sl])
        a = (g * jax.nn.sigmoid(g) * u).astype(BF16)
        part = _dot(a, wd_ref[sl, :])
        if c == 0:
            acc_ref[...] = part
        else:
            acc_ref[...] += part
    out = hf + acc_ref[...]
    if final_norm:
        out = _rms(out, fw_ref[...])
    o_ref[...] = out


def _ffn(h, nw, wg, wu, wd, fw, final_norm, tm=512):
    t = h.shape[0]
    const = lambda shape: pl.BlockSpec(shape, lambda i: (0, 0))
    return pl.pallas_call(
        functools.partial(_ffn_kernel, f_chunk=256, final_norm=final_norm),
        grid=(t // tm,),
        in_specs=[
            pl.BlockSpec((tm, D_MODEL), lambda i: (i, 0)),
            const((1, D_MODEL)),
            const((D_MODEL, D_FF)), const((D_MODEL, D_FF)), const((D_FF, D_MODEL)),
            const((1, D_MODEL)),
        ],
        out_specs=pl.BlockSpec((tm, D_MODEL), lambda i: (i, 0)),
        out_shape=jax.ShapeDtypeStruct((t, D_MODEL), F32),
        scratch_shapes=[pltpu.VMEM((tm, D_MODEL), F32)],
        compiler_params=_cparams("parallel"),
        name="ffn",
    )(h, nw, wg, wu, wd, fw)


N_FOX_QKV = 3 * D_FOX
N_REST = 3 * D_SB + 4 * D_ML


def _regroup_kernel(w_ref, o_ref):
    o_ref[:, :N_FOX_QKV] = w_ref[:, :N_FOX_QKV].astype(BF16)
    rest = w_ref[:, N_FOX_QKV + FOX_HEADS:N_FOX_QKV + FOX_HEADS + N_REST]
    o_ref[:, N_FOX_QKV:] = rest.astype(BF16)


def _regroup_w_in(w, tr=256):
    depth, rows, n_in = w.shape
    return pl.pallas_call(
        _regroup_kernel,
        grid=(depth, rows // tr),
        in_specs=[pl.BlockSpec((None, tr, n_in), lambda l, i: (l, i, 0))],
        out_specs=pl.BlockSpec((None, tr, D_MAIN), lambda l, i: (l, i, 0)),
        out_shape=jax.ShapeDtypeStruct((depth, rows, D_MAIN), BF16),
        compiler_params=_cparams("parallel", "parallel"),
        name="regroup",
    )(w)


def _gate_w_in(w):
    a = N_FOX_QKV
    b = a + FOX_HEADS + N_REST
    gate = jnp.concatenate([w[:, :, a:a + FOX_HEADS], w[:, :, b:b + 2 * ML_HEADS]], axis=2)
    return jnp.pad(gate, ((0, 0), (0, 0), (0, N_GATE - gate.shape[2]))).astype(BF16)


def kernel(x, mem, norm_mix_w, w_in, fox_f_b, ml_conv_w, ml_i_b, ml_f_b, ml_norm_w, w_out, norm_x_w, mem_norm_w,
           wx_q, wx_kv, wx_o, norm_ffn_w, w_gate, w_up, w_down, final_norm_w):
    bsz, s, _ = x.shape
    mlen = mem.shape[1]
    depth = w_in.shape[0]
    h = x.reshape(bsz * s, D_MODEL)
    mem2d = mem.reshape(bsz * mlen, D_MODEL)
    row = lambda v: v.reshape(1, -1)
    w_main = _regroup_w_in(w_in)
    w_g = _gate_w_in(w_in)
    bias = jnp.concatenate([fox_f_b, ml_i_b, ml_f_b, jnp.zeros((depth, N_GATE - 3 * FOX_HEADS), F32)], axis=1)
    for l in range(depth):
        u, g_raw = _inproj(h, row(norm_mix_w[l]), w_main[l], w_g[l])
        g, gt = _gates(g_raw, row(bias[l]), bsz, s)
        y_fox = _fox(u, g, gt, bsz, s)
        y_sb = _sb(u, bsz, s)
        y_ml = _mlstm(u, g, gt, ml_conv_w[l], row(ml_norm_w[l]), bsz, s)
        wo = w_out[l].astype(BF16)
        h = _outproj(y_fox, y_sb, y_ml, wo[:D_FOX], wo[D_FOX:D_FOX + D_SB], wo[D_FOX + D_SB:], h)
        kv = _memkv(mem2d, row(mem_norm_w[l]), wx_kv[l].astype(BF16), tm=min(512, bsz * mlen))
        h = _cross(h, row(norm_x_w[l]), wx_q[l].astype(BF16), kv, wx_o[l].astype(BF16), bsz, s, mlen)
        h = _ffn(h, row(norm_ffn_w[l]), w_gate[l].astype(BF16), w_up[l].astype(BF16), w_down[l].astype(BF16),
                 row(final_norm_w), final_norm=(l == depth - 1))
    return h.reshape(bsz, s, D_MODEL)
```

```python
import functools

import jax
import jax.numpy as jnp
from jax import lax
from jax.experimental import pallas as pl
from jax.experimental.pallas import tpu as pltpu

D_MODEL = 1024
HEAD_DIM = 64
FOX_HEADS = 6
SB_HEADS = 4
ML_HEADS = 6
D_FOX = FOX_HEADS * HEAD_DIM
D_SB = SB_HEADS * HEAD_DIM
D_ML = ML_HEADS * HEAD_DIM
ML_CHUNK = 128
CONV_WIDTH = 4
X_HEADS = 4
X_HEAD_DIM = 128
D_X = X_HEADS * X_HEAD_DIM
D_FF = 2816
RMS_EPS = 1e-6

LANES = 128
PAIR = 2 * HEAD_DIM
D_MAIN = 3 * D_FOX + 3 * D_SB + 4 * D_ML
N_GATE = LANES
GT_ROWS = 32
NEG = -1e30
SB_SUB = 128
SB_EXIT = -110.0
VMEM_LIMIT = 56 * 1024 * 1024

BF16 = jnp.bfloat16
F32 = jnp.float32

FOX_Q, FOX_K, FOX_V = 0, 3, 6
SB_Q, SB_K, SB_V = 9, 11, 13
G_FOX, G_MLA, G_MLB, G_MLP = 0, 6, 12, 18
EXP_B = ML_HEADS * LANES
EXP_A = EXP_B + D_ML
N_EXP = EXP_A + D_ML


def _cparams(*sem):
    return pltpu.CompilerParams(dimension_semantics=sem, vmem_limit_bytes=VMEM_LIMIT)


def _rms(xf, w):
    ms = jnp.mean(xf * xf, axis=-1, keepdims=True)
    return xf * lax.rsqrt(ms + RMS_EPS) * w


def _dot(a, b):
    return jnp.dot(a, b, preferred_element_type=F32)


def _dot_nt(a, b):
    return lax.dot_general(a, b, (((1,), (1,)), ((), ())), preferred_element_type=F32)


def _dot_tn(a, b):
    return lax.dot_general(a, b, (((0,), (0,)), ((), ())), preferred_element_type=F32)


def _log_sigmoid(x):
    return jnp.minimum(x, 0.0) - jnp.log(1.0 + jnp.exp(-jnp.abs(x)))


def _split3(x):
    x1 = x.astype(BF16)
    r1 = x - x1.astype(F32)
    x2 = r1.astype(BF16)
    return x1, x2, (r1 - x2.astype(F32)).astype(BF16)


def _dot_exact(a, b, lhs_exact):
    if lhs_exact:
        x1, x2, x3 = _split3(b)
        return _dot(a, x1) + _dot(a, x2) + _dot(a, x3)
    x1, x2, x3 = _split3(a)
    return _dot(x1, b) + _dot(x2, b) + _dot(x3, b)


def _inproj_kernel(x_ref, nw_ref, w_ref, wg_ref, u_ref, g_ref, *, n_chunk):
    xn = _rms(x_ref[...], nw_ref[...]).astype(BF16)
    for c in range(D_MAIN // n_chunk):
        sl = slice(c * n_chunk, (c + 1) * n_chunk)
        u_ref[:, sl] = _dot(xn, w_ref[:, sl]).astype(BF16)
    g_ref[...] = _dot(xn, wg_ref[...])


def _inproj(h, nw, w_main, w_gate, tm=512):
    t = h.shape[0]
    return pl.pallas_call(
        functools.partial(_inproj_kernel, n_chunk=1152),
        grid=(t // tm,),
        in_specs=[
            pl.BlockSpec((tm, D_MODEL), lambda i: (i, 0)),
            pl.BlockSpec((1, D_MODEL), lambda i: (0, 0)),
            pl.BlockSpec((D_MODEL, D_MAIN), lambda i: (0, 0)),
            pl.BlockSpec((D_MODEL, N_GATE), lambda i: (0, 0)),
        ],
        out_specs=[
            pl.BlockSpec((tm, D_MAIN), lambda i: (i, 0)),
            pl.BlockSpec((tm, N_GATE), lambda i: (i, 0)),
        ],
        out_shape=[jax.ShapeDtypeStruct((t, D_MAIN), BF16), jax.ShapeDtypeStruct((t, N_GATE), F32)],
        compiler_params=_cparams("parallel"),
        name="inproj",
    )(h, nw, w_main, w_gate)


def _gates_kernel(g_ref, b_ref, go_ref, gt_ref, *, s):
    L = ML_CHUNK
    col = lax.broadcasted_iota(jnp.int32, (L, LANES), 1)
    row = lax.broadcasted_iota(jnp.int32, (L, LANES), 0)
    is_a = (col >= G_MLA) & (col < G_MLB)
    r = lax.broadcasted_iota(jnp.int32, (L, L), 0)
    c = lax.broadcasted_iota(jnp.int32, (L, L), 1)
    tri = jnp.where(r >= c, 1.0, 0.0).astype(BF16)
    bias = b_ref[...]

    def chunk(ci, carry):
        start = pl.multiple_of(ci * L, L)
        pre = g_ref[pl.ds(start, L), :] + bias
        cum = _dot_exact(tri, jnp.where(is_a, 0.0, _log_sigmoid(pre)), lhs_exact=True)
        tot = cum + carry
        a = pre - pltpu.roll(cum, LANES - (G_MLB - G_MLA), 1)
        pm = jnp.where(is_a, a, -jnp.inf)
        k = 1
        while k < L:
            pm = jnp.maximum(pm, jnp.where(row >= k, pltpu.roll(pm, k, 0), -jnp.inf))
            k *= 2
        pm = pltpu.roll(pm, G_MLP - G_MLA, 1)
        out = jnp.where(col < G_MLA, tot,
                        jnp.where(is_a, a, jnp.where(col < G_MLP, cum, jnp.where(col < G_MLP + ML_HEADS, pm, 0.0))))
        go_ref[pl.ds(start, L), :] = out
        gt_ref[:, pl.ds(start, L)] = out.T[:GT_ROWS, :]
        return tot[L - 1:L, :]

    lax.fori_loop(0, s // L, chunk, jnp.zeros((1, LANES), F32))


def _gates(g, bias, bsz, s):
    return pl.pallas_call(
        functools.partial(_gates_kernel, s=s),
        grid=(bsz,),
        in_specs=[
            pl.BlockSpec((s, N_GATE), lambda b: (b, 0)),
            pl.BlockSpec((1, N_GATE), lambda b: (0, 0)),
        ],
        out_specs=[
            pl.BlockSpec((s, N_GATE), lambda b: (b, 0)),
            pl.BlockSpec((None, GT_ROWS, s), lambda b: (b, 0, 0)),
        ],
        out_shape=[jax.ShapeDtypeStruct((bsz * s, N_GATE), F32), jax.ShapeDtypeStruct((bsz, GT_ROWS, s), F32)],
        compiler_params=_cparams("parallel"),
        name="gates",
    )(g, bias)


def _head_masks(q2):
    lane = lax.broadcasted_iota(jnp.int32, q2.shape, 1)
    zero = jnp.zeros_like(q2)
    return jnp.where(lane < HEAD_DIM, q2, zero), jnp.where(lane >= HEAD_DIM, q2, zero)


def _stack_heads(q2):
    return jnp.concatenate(_head_masks(q2), axis=0)


def _unstack_heads(x, rows):
    lane = lax.broadcasted_iota(jnp.int32, (rows, LANES), 1)
    return jnp.where(lane < HEAD_DIM, x[:rows], x[rows:])


def _fox_kernel(q_ref, k_ref, v_ref, g_ref, gt_ref, o_ref, ct_ref, m_ref, acc_ref, *, tq):
    p = pl.program_id(1)
    qi = pl.program_id(2)
    qs = _stack_heads(q_ref[...] * jnp.asarray(HEAD_DIM ** -0.5, BF16))
    gq = g_ref[...]
    lane = lax.broadcasted_iota(jnp.int32, (tq, LANES), 1)
    lo_lanes = lane < HEAD_DIM
    for hh in range(2):
        ct = jnp.sum(jnp.where(lane == G_FOX + 2 * p + hh, gq, 0.0), axis=1, keepdims=True)
        ct_ref[hh * tq:(hh + 1) * tq, :] = jnp.broadcast_to(ct, (tq, LANES))
    m_ref[...] = jnp.full_like(m_ref, NEG)
    acc_ref[...] = jnp.zeros_like(acc_ref)
    reps = tq // LANES

    def group(j, masked):
        start = pl.multiple_of(j * tq, tq)
        s = _dot_nt(qs, k_ref[pl.ds(start, tq), :])
        v2 = v_ref[pl.ds(start, tq), :]
        one = jnp.ones_like(v2)
        v_ones = (jnp.where(lo_lanes, v2, one), jnp.where(lo_lanes, one, v2))
        for hh in range(2):
            rows = slice(hh * tq, (hh + 1) * tq)
            cs = gt_ref[pl.ds(G_FOX + 2 * p + hh, 1), pl.ds(start, tq)]
            sh = s[rows] + (pltpu.repeat(ct_ref[rows, :], reps, axis=1) - cs)
            if masked:
                rq = lax.broadcasted_iota(jnp.int32, (tq, tq), 0)
                rk = lax.broadcasted_iota(jnp.int32, (tq, tq), 1)
                sh = jnp.where(rk <= rq, sh, NEG)
            m_prev = m_ref[rows, :]
            m_next = jnp.maximum(m_prev, jnp.max(sh, axis=1, keepdims=True))
            pr = jnp.exp(sh - pltpu.repeat(m_next, reps, axis=1))
            alpha = jnp.exp(m_prev - m_next)
            acc_ref[rows, :] = alpha * acc_ref[rows, :] + _dot(pr.astype(BF16), v_ones[hh])
            m_ref[rows, :] = m_next

    def body(j, carry):
        group(j, False)
        return carry

    lax.fori_loop(0, qi, body, 0)
    group(qi, True)
    acc = acc_ref[...]
    o_ref[...] = _unstack_heads(acc / pltpu.roll(acc, HEAD_DIM, 1), tq).astype(BF16)


def _fox(u, g, gt, bsz, s, tq=512):
    nq = s // tq
    stat = pltpu.VMEM((2 * tq, LANES), F32)
    return pl.pallas_call(
        functools.partial(_fox_kernel, tq=tq),
        grid=(bsz, FOX_HEADS // 2, nq),
        in_specs=[
            pl.BlockSpec((tq, PAIR), lambda b, p, i: (b * nq + i, FOX_Q + p)),
            pl.BlockSpec((s, PAIR), lambda b, p, i: (b, FOX_K + p)),
            pl.BlockSpec((s, PAIR), lambda b, p, i: (b, FOX_V + p)),
            pl.BlockSpec((tq, N_GATE), lambda b, p, i: (b * nq + i, 0)),
            pl.BlockSpec((None, 8, s), lambda b, p, i: (b, 0, 0)),
        ],
        out_specs=pl.BlockSpec((tq, PAIR), lambda b, p, i: (b * nq + i, p)),
        out_shape=jax.ShapeDtypeStruct((bsz * s, D_FOX), BF16),
        scratch_shapes=[stat, stat, stat],
        compiler_params=_cparams("parallel", "parallel", "arbitrary"),
        name="fox",
    )(u, u, u, g, gt)


def _sb_kernel(q_ref, k_ref, v_ref, o_ref, run_ref, acc_ref, *, tq):
    qi = pl.program_id(2)
    qs = _stack_heads(q_ref[...] * jnp.asarray(HEAD_DIM ** -0.5, BF16))
    r = lax.broadcasted_iota(jnp.int32, (2 * SB_SUB, 2 * SB_SUB), 0)
    c = lax.broadcasted_iota(jnp.int32, (2 * SB_SUB, 2 * SB_SUB), 1)
    suffix = jnp.where((c >= SB_SUB) | (jnp.where(r >= SB_SUB, r - SB_SUB, r) >= c), 1.0, 0.0).astype(BF16)
    run_ref[...] = jnp.zeros_like(run_ref)
    acc_ref[...] = jnp.zeros_like(acc_ref)

    def group(g, masked):
        start = pl.multiple_of((qi - g) * tq, tq)
        z = _dot_nt(qs, k_ref[pl.ds(start, tq), :])
        lk = -(jnp.maximum(z, 0.0) + jnp.log(1.0 + jnp.exp(-jnp.abs(z))))
        if masked:
            rq = lax.broadcasted_iota(jnp.int32, (2 * tq, tq), 0)
            rk = lax.broadcasted_iota(jnp.int32, (2 * tq, tq), 1)
            valid = rk < jnp.where(rq >= tq, rq - tq, rq)
            lk = jnp.where(valid, lk, 0.0)
        hi = lk.astype(BF16)
        lo = (lk - hi.astype(F32)).astype(BF16)
        offs = run_ref[...]
        parts = [None] * (tq // SB_SUB)
        for sb in reversed(range(tq // SB_SUB)):
            sl = slice(sb * SB_SUB, (sb + 1) * SB_SUB)
            rs = _dot(jnp.concatenate([hi[:, sl], lo[:, sl]], axis=1), suffix)
            a = jnp.exp(z[:, sl] + rs[:, :SB_SUB] + offs)
            if masked:
                a = jnp.where(valid[:, sl], a, 0.0)
            parts[sb] = a.astype(BF16)
            offs = offs + rs[:, SB_SUB:]
        run_ref[...] = offs
        acc_ref[...] += _dot(jnp.concatenate(parts, axis=1), v_ref[pl.ds(start, tq), :])
        return jnp.max(offs)

    top = group(0, True)

    def cond(carry):
        g, top = carry
        return (g <= qi) & (top > SB_EXIT)

    def body(carry):
        g, _ = carry
        return g + 1, group(g, False)

    lax.while_loop(cond, body, (jnp.int32(1), top))
    o_ref[...] = _unstack_heads(acc_ref[...], tq).astype(BF16)


def _sb(u, bsz, s, tq=512):
    nq = s // tq
    stat = pltpu.VMEM((2 * tq, LANES), F32)
    return pl.pallas_call(
        functools.partial(_sb_kernel, tq=tq),
        grid=(bsz, SB_HEADS // 2, nq),
        in_specs=[
            pl.BlockSpec((tq, PAIR), lambda b, p, i: (b * nq + i, SB_Q + p)),
            pl.BlockSpec((s, PAIR), lambda b, p, i: (b, SB_K + p)),
            pl.BlockSpec((s, PAIR), lambda b, p, i: (b, SB_V + p)),
        ],
        out_specs=pl.BlockSpec((tq, PAIR), lambda b, p, i: (b * nq + i, p)),
        out_shape=jax.ShapeDtypeStruct((bsz * s, D_SB), BF16),
        scratch_shapes=[stat, stat],
        compiler_params=_cparams("parallel", "parallel", "arbitrary"),
        name="sb",
    )(u, u, u)


def _mlstm_kernel(xq_ref, xk_ref, v_ref, og_ref, g_ref, gt_ref, cw_ref, nw_ref, o_ref, qs_ref, ks_ref, e_ref, *, s):
    L = ML_CHUNK
    nc = s // L
    row = lax.broadcasted_iota(jnp.int32, (L, D_ML), 0)

    def conv_chunk(c, _):
        start = pl.multiple_of(c * L, L)
        prev_start = pl.multiple_of(jnp.maximum(c - 1, 0) * L, L)
        for x_ref, dst_ref, w0, scale in ((xq_ref, qs_ref, 0, 1.0), (xk_ref, ks_ref, D_ML, HEAD_DIM ** -0.5)):
            cur = x_ref[pl.ds(start, L), :].astype(F32)
            prev = x_ref[pl.ds(prev_start, L), :].astype(F32)
            prev = jnp.where(c > 0, prev, 0.0)
            w = cw_ref[:, w0:w0 + D_ML]
            y = cur * w[CONV_WIDTH - 1:CONV_WIDTH, :]
            for k in range(1, CONV_WIDTH):
                sh = jnp.where(row < k, pltpu.roll(prev, k, 0), pltpu.roll(cur, k, 0))
                y = y + sh * w[CONV_WIDTH - 1 - k:CONV_WIDTH - k, :]
            y = y * jax.nn.sigmoid(y) * scale
            dst_ref[pl.ds(start, L), :] = y.astype(BF16)
        return 0

    lax.fori_loop(0, nc, conv_chunk, 0)

    ci = lax.broadcasted_iota(jnp.int32, (LANES, N_EXP), 0)
    cj = lax.broadcasted_iota(jnp.int32, (LANES, N_EXP), 1)
    src_col = jnp.where(cj < EXP_B, G_MLP + lax.shift_right_logical(cj, 7),
                        jnp.where(cj < EXP_A, G_MLB + lax.shift_right_logical(cj - EXP_B, 6),
                                  G_MLA + lax.shift_right_logical(cj - EXP_A, 6)))
    e_ref[...] = jnp.where(ci == src_col, 1.0, 0.0).astype(BF16)

    lane = lax.broadcasted_iota(jnp.int32, (L, LANES), 1)
    lo_lanes = lane < HEAD_DIM
    r_ll = lax.broadcasted_iota(jnp.int32, (L, L), 0)
    c_ll = lax.broadcasted_iota(jnp.int32, (L, L), 1)
    tri = r_ll >= c_ll
    blockdiag = (r_ll < HEAD_DIM) == (c_ll < HEAD_DIM)
    bd_ones = jnp.where(blockdiag, 1.0, 0.0).astype(BF16)
    ones = jnp.ones((L, LANES), BF16)

    def chunk(c, carry):
        start = pl.multiple_of(c * L, L)
        rep = _dot_exact(g_ref[pl.ds(start, L), :], e_ref[...], lhs_exact=False)
        gtc = gt_ref[:, pl.ds(start, L)]
        new_carry = []
        for p in range(ML_HEADS // 2):
            c2, nm, m2 = carry[p]
            sl = slice(p * PAIR, (p + 1) * PAIR)
            q2 = qs_ref[pl.ds(start, L), sl]
            k2 = ks_ref[pl.ds(start, L), sl]
            v_one = jnp.concatenate([v_ref[pl.ds(start, L), sl], ones], axis=1)
            b2 = rep[:, EXP_B + p * PAIR:EXP_B + (p + 1) * PAIR]
            a2 = rep[:, EXP_A + p * PAIR:EXP_A + (p + 1) * PAIR]
            pm = [rep[:, (2 * p + hh) * LANES:(2 * p + hh + 1) * LANES] for hh in range(2)]
            pm2 = jnp.where(lo_lanes, pm[0], pm[1])
            qk = _dot_nt(_stack_heads(q2), k2)
            sc = []
            for hh in range(2):
                a_row = gtc[G_MLA + 2 * p + hh:G_MLA + 2 * p + hh + 1, :]
                sc.append(qk[hh * L:(hh + 1) * L] * jnp.exp(jnp.where(tri, a_row - pm[hh], -jnp.inf)))
            r = _dot(jnp.concatenate(sc, axis=0).astype(BF16), v_one)
            num_loc = jnp.where(lo_lanes, r[:L, :LANES], r[L:, :LANES])
            den_loc = jnp.where(lo_lanes, r[:L, LANES:], r[L:, LANES:])
            m_loc = b2 + pm2
            inter = _dot(q2, jnp.concatenate([c2, nm], axis=1).astype(BF16))
            it2 = b2 + m2
            m_t = jnp.maximum(it2, m_loc)
            f_loc = jnp.exp(m_loc - m_t)
            w_int = jnp.exp(it2 - m_t)
            num = num_loc * f_loc + w_int * inter[:, :LANES]
            den = den_loc * f_loc + w_int * inter[:, LANES:]
            hp = num / jnp.maximum(jnp.abs(den), jnp.exp(-m_t))
            sq = hp * hp
            sq_hi = sq.astype(BF16)
            sq_lo = (sq - sq_hi.astype(F32)).astype(BF16)
            msq = (_dot(sq_hi, bd_ones) + _dot(sq_lo, bd_ones)) * (1.0 / HEAD_DIM)
            hp = hp * lax.rsqrt(msq + RMS_EPS) * nw_ref[:, sl]
            gate = jax.nn.sigmoid(og_ref[pl.ds(start, L), sl].astype(F32))
            o_ref[pl.ds(start, L), sl] = (gate * hp).astype(BF16)
            g2 = b2[L - 1:L, :]
            pl2 = pm2[L - 1:L, :]
            kw = (k2.astype(F32) * jnp.exp(a2 - pl2)).astype(BF16)
            upd = _dot_tn(kw, v_one)
            m_new = jnp.maximum(g2 + m2, g2 + pl2)
            decay = jnp.exp(g2 + m2 - m_new)
            scale = jnp.exp(g2 + pl2 - m_new)
            c2 = decay * c2 + scale * jnp.where(blockdiag, upd[:, :LANES], 0.0)
            nm = decay * nm + scale * jnp.where(blockdiag, upd[:, LANES:], 0.0)
            new_carry.append((c2, nm, m_new))
        return tuple(new_carry)

    init = tuple((jnp.zeros((LANES, LANES), F32), jnp.zeros((LANES, LANES), F32), jnp.zeros((1, LANES), F32))
                 for _ in range(ML_HEADS // 2))
    lax.fori_loop(0, nc, chunk, init)


def _mlstm(u, g, gt, conv_w, norm_w, bsz, s):
    blk = lambda j: pl.BlockSpec((s, D_ML), lambda b: (b, j))
    first = (3 * D_FOX + 3 * D_SB) // D_ML
    return pl.pallas_call(
        functools.partial(_mlstm_kernel, s=s),
        grid=(bsz,),
        in_specs=[
            blk(first), blk(first + 1), blk(first + 2), blk(first + 3),
            pl.BlockSpec((s, N_GATE), lambda b: (b, 0)),
            pl.BlockSpec((None, GT_ROWS, s), lambda b: (b, 0, 0)),
            pl.BlockSpec((CONV_WIDTH, 2 * D_ML), lambda b: (0, 0)),
            pl.BlockSpec((1, D_ML), lambda b: (0, 0)),
        ],
        out_specs=pl.BlockSpec((s, D_ML), lambda b: (b, 0)),
        out_shape=jax.ShapeDtypeStruct((bsz * s, D_ML), BF16),
        scratch_shapes=[pltpu.VMEM((s, D_ML), BF16), pltpu.VMEM((s, D_ML), BF16), pltpu.VMEM((LANES, N_EXP), BF16)],
        compiler_params=_cparams("parallel"),
        name="mlstm",
    )(u, u, u, u, g, gt, conv_w, norm_w)


def _outproj_kernel(yf_ref, ys_ref, ym_ref, wf_ref, ws_ref, wm_ref, h_ref, o_ref):
    acc = _dot(yf_ref[...], wf_ref[...]) + _dot(ys_ref[...], ws_ref[...]) + _dot(ym_ref[...], wm_ref[...])
    o_ref[...] = h_ref[...] + acc


def _outproj(yf, ys, ym, wf, ws, wm, h, tm=512):
    t = h.shape[0]
    row = lambda n: pl.BlockSpec((tm, n), lambda i: (i, 0))
    full = lambda n: pl.BlockSpec((n, D_MODEL), lambda i: (0, 0))
    return pl.pallas_call(
        _outproj_kernel,
        grid=(t // tm,),
        in_specs=[row(D_FOX), row(D_SB), row(D_ML), full(D_FOX), full(D_SB), full(D_ML), row(D_MODEL)],
        out_specs=row(D_MODEL),
        out_shape=jax.ShapeDtypeStruct((t, D_MODEL), F32),
        compiler_params=_cparams("parallel"),
        name="outproj",
    )(yf, ys, ym, wf, ws, wm, h)


def _memkv_kernel(x_ref, nw_ref, w_ref, o_ref):
    xn = _rms(x_ref[...], nw_ref[...]).astype(BF16)
    o_ref[...] = _dot(xn, w_ref[...]).astype(BF16)


def _memkv(mem2d, nw, w, tm=512):
    t = mem2d.shape[0]
    return pl.pallas_call(
        _memkv_kernel,
        grid=(t // tm,),
        in_specs=[
            pl.BlockSpec((tm, D_MODEL), lambda i: (i, 0)),
            pl.BlockSpec((1, D_MODEL), lambda i: (0, 0)),
            pl.BlockSpec((D_MODEL, 2 * D_X), lambda i: (0, 0)),
        ],
        out_specs=pl.BlockSpec((tm, 2 * D_X), lambda i: (i, 0)),
        out_shape=jax.ShapeDtypeStruct((t, 2 * D_X), BF16),
        compiler_params=_cparams("parallel"),
        name="memkv",
    )(mem2d, nw, w)


def _cross_kernel(h_ref, nw_ref, wq_ref, kv_ref, wo_ref, o_ref):
    hf = h_ref[...]
    hn = _rms(hf, nw_ref[...]).astype(BF16)
    q = _dot(hn, wq_ref[...]).astype(BF16)
    outs = []
    for hd in range(X_HEADS):
        sl = slice(hd * X_HEAD_DIM, (hd + 1) * X_HEAD_DIM)
        k = kv_ref[:, sl]
        v = kv_ref[:, D_X + hd * X_HEAD_DIM:D_X + (hd + 1) * X_HEAD_DIM]
        s = _dot_nt(q[:, sl], k) * (X_HEAD_DIM ** -0.5)
        m = jnp.max(s, axis=1, keepdims=True)
        e = jnp.exp(s - m)
        pr = e / jnp.sum(e, axis=1, keepdims=True)
        outs.append(_dot(pr.astype(BF16), v).astype(BF16))
    o = jnp.concatenate(outs, axis=1)
    o_ref[...] = hf + _dot(o, wo_ref[...])


def _cross(h, nw, wq, kv, wo, bsz, s, mlen, tm=512):
    nt = s // tm
    return pl.pallas_call(
        _cross_kernel,
        grid=(bsz, nt),
        in_specs=[
            pl.BlockSpec((tm, D_MODEL), lambda b, i: (b * nt + i, 0)),
            pl.BlockSpec((1, D_MODEL), lambda b, i: (0, 0)),
            pl.BlockSpec((D_MODEL, D_X), lambda b, i: (0, 0)),
            pl.BlockSpec((mlen, 2 * D_X), lambda b, i: (b, 0)),
            pl.BlockSpec((D_X, D_MODEL), lambda b, i: (0, 0)),
        ],
        out_specs=pl.BlockSpec((tm, D_MODEL), lambda b, i: (b * nt + i, 0)),
        out_shape=jax.ShapeDtypeStruct((bsz * s, D_MODEL), F32),
        compiler_params=_cparams("parallel", "parallel"),
        name="cross",
    )(h, nw, wq, kv, wo)


def _ffn_kernel(h_ref, nw_ref, wg_ref, wu_ref, wd_ref, fw_ref, o_ref, acc_ref, *, f_chunk, final_norm):
    hf = h_ref[...]
    hn = _rms(hf, nw_ref[...]).astype(BF16)
    for c in range(D_FF // f_chunk):
        sl = slice(c * f_chunk, (c + 1) * f_chunk)
        g = _dot(hn, wg_ref[:, sl])
        u = _dot(hn, wu_ref[:, sl])
        a = (g * jax.nn.sigmoid(g) * u).astype(BF16)
        part = _dot(a, wd_ref[sl, :])
        if c == 0:
            acc_ref[...] = part
        else:
            acc_ref[...] += part
    out = hf + acc_ref[...]
    if final_norm:
        out = _rms(out, fw_ref[...])
    o_ref[...] = out


def _ffn(h, nw, wg, wu, wd, fw, final_norm, tm=512):
    t = h.shape[0]
    const = lambda shape: pl.BlockSpec(shape, lambda i: (0, 0))
    return pl.pallas_call(
        functools.partial(_ffn_kernel, f_chunk=256, final_norm=final_norm),
        grid=(t // tm,),
        in_specs=[
            pl.BlockSpec((tm, D_MODEL), lambda i: (i, 0)),
            const((1, D_MODEL)),
            const((D_MODEL, D_FF)), const((D_MODEL, D_FF)), const((D_FF, D_MODEL)),
            const((1, D_MODEL)),
        ],
        out_specs=pl.BlockSpec((tm, D_MODEL), lambda i: (i, 0)),
        out_shape=jax.ShapeDtypeStruct((t, D_MODEL), F32),
        scratch_shapes=[pltpu.VMEM((tm, D_MODEL), F32)],
        compiler_params=_cparams("parallel"),
        name="ffn",
    )(h, nw, wg, wu, wd, fw)


N_FOX_QKV = 3 * D_FOX
N_REST = 3 * D_SB + 4 * D_ML


def _regroup_kernel(w_ref, o_ref):
    o_ref[:, :N_FOX_QKV] = w_ref[:, :N_FOX_QKV].astype(BF16)
    rest = w_ref[:, N_FOX_QKV + FOX_HEADS:N_FOX_QKV + FOX_HEADS + N_REST]
    o_ref[:, N_FOX_QKV:] = rest.astype(BF16)


def _regroup_w_in(w, tr=256):
    depth, rows, n_in = w.shape
    return pl.pallas_call(
        _regroup_kernel,
        grid=(depth, rows // tr),
        in_specs=[pl.BlockSpec((None, tr, n_in), lambda l, i: (l, i, 0))],
        out_specs=pl.BlockSpec((None, tr, D_MAIN), lambda l, i: (l, i, 0)),
        out_shape=jax.ShapeDtypeStruct((depth, rows, D_MAIN), BF16),
        compiler_params=_cparams("parallel", "parallel"),
        name="regroup",
    )(w)


def _gate_w_in(w):
    a = N_FOX_QKV
    b = a + FOX_HEADS + N_REST
    gate = jnp.concatenate([w[:, :, a:a + FOX_HEADS], w[:, :, b:b + 2 * ML_HEADS]], axis=2)
    return jnp.pad(gate, ((0, 0), (0, 0), (0, N_GATE - gate.shape[2]))).astype(BF16)


def kernel(x, mem, norm_mix_w, w_in, fox_f_b, ml_conv_w, ml_i_b, ml_f_b, ml_norm_w, w_out, norm_x_w, mem_norm_w,
           wx_q, wx_kv, wx_o, norm_ffn_w, w_gate, w_up, w_down, final_norm_w):
    bsz, s, _ = x.shape
    mlen = mem.shape[1]
    depth = w_in.shape[0]
    h = x.reshape(bsz * s, D_MODEL)
    mem2d = mem.reshape(bsz * mlen, D_MODEL)
    row = lambda v: v.reshape(1, -1)
    w_main = _regroup_w_in(w_in)
    w_g = _gate_w_in(w_in)
    bias = jnp.concatenate([fox_f_b, ml_i_b, ml_f_b, jnp.zeros((depth, N_GATE - 3 * FOX_HEADS), F32)], axis=1)
    for l in range(depth):
        u, g_raw = _inproj(h, row(norm_mix_w[l]), w_main[l], w_g[l])
        g, gt = _gates(g_raw, row(bias[l]), bsz, s)
        y_fox = _fox(u, g, gt, bsz, s)
        y_sb = _sb(u, bsz, s)
        y_ml = _mlstm(u, g, gt, ml_conv_w[l], row(ml_norm_w[l]), bsz, s)
        wo = w_out[l].astype(BF16)
        h = _outproj(y_fox, y_sb, y_ml, wo[:D_FOX], wo[D_FOX:D_FOX + D_SB], wo[D_FOX + D_SB:], h)
        kv = _memkv(mem2d, row(mem_norm_w[l]), wx_kv[l].astype(BF16), tm=min(512, bsz * mlen))
        h = _cross(h, row(norm_x_w[l]), wx_q[l].astype(BF16), kv, wx_o[l].astype(BF16), bsz, s, mlen)
        h = _ffn(h, row(norm_ffn_w[l]), w_gate[l].astype(BF16), w_up[l].astype(BF16), w_down[l].astype(BF16),
                 row(final_norm_w), final_norm=(l == depth - 1))
    return h.reshape(bsz, s, D_MODEL)
```

```python
import functools

import jax
import jax.numpy as jnp
from jax import lax
from jax.experimental import pallas as pl
from jax.experimental.pallas import tpu as pltpu

D_MODEL = 1024
HEAD_DIM = 64
FOX_HEADS = 6
SB_HEADS = 4
ML_HEADS = 6
D_FOX = FOX_HEADS * HEAD_DIM
D_SB = SB_HEADS * HEAD_DIM
D_ML = ML_HEADS * HEAD_DIM
ML_CHUNK = 128
CONV_WIDTH = 4
X_HEADS = 4
X_HEAD_DIM = 128
D_X = X_HEADS * X_HEAD_DIM
D_FF = 2816
RMS_EPS = 1e-6

LANES = 128
PAIR = 2 * HEAD_DIM
D_MAIN = 3 * D_FOX + 3 * D_SB + 4 * D_ML
N_GATE = LANES
GT_ROWS = 32
NEG = -1e30
SB_SUB = 128
SB_EXIT = 110.0
LOG2E = 1.4426950408889634
VMEM_LIMIT = 56 * 1024 * 1024

BF16 = jnp.bfloat16
F32 = jnp.float32

FOX_Q, FOX_K, FOX_V = 0, 3, 6
SB_Q, SB_K, SB_V = 9, 11, 13
G_FOX, G_MLA, G_MLB, G_MLP = 0, 6, 12, 18
EXP_B = ML_HEADS * LANES
EXP_A = EXP_B + D_ML
N_EXP = EXP_A + D_ML


def _cparams(*sem):
    return pltpu.CompilerParams(dimension_semantics=sem, vmem_limit_bytes=VMEM_LIMIT)


def _rms(xf, w):
    ms = jnp.mean(xf * xf, axis=-1, keepdims=True)
    return xf * lax.rsqrt(ms + RMS_EPS) * w


def _dot(a, b):
    return jnp.dot(a, b, preferred_element_type=F32)


def _dot_nt(a, b):
    return lax.dot_general(a, b, (((1,), (1,)), ((), ())), preferred_element_type=F32)


def _dot_tn(a, b):
    return lax.dot_general(a, b, (((0,), (0,)), ((), ())), preferred_element_type=F32)


def _log_sigmoid(x):
    return jnp.minimum(x, 0.0) - jnp.log(1.0 + jnp.exp(-jnp.abs(x)))


def _split3(x):
    x1 = x.astype(BF16)
    r1 = x - x1.astype(F32)
    x2 = r1.astype(BF16)
    return x1, x2, (r1 - x2.astype(F32)).astype(BF16)


def _dot_exact(a, b, lhs_exact):
    if lhs_exact:
        x1, x2, x3 = _split3(b)
        return _dot(a, x1) + _dot(a, x2) + _dot(a, x3)
    x1, x2, x3 = _split3(a)
    return _dot(x1, b) + _dot(x2, b) + _dot(x3, b)


def _inproj_kernel(x_ref, nw_ref, w_ref, wg_ref, u_ref, g_ref, *, n_chunk):
    xn = _rms(x_ref[...], nw_ref[...]).astype(BF16)
    for c in range(D_MAIN // n_chunk):
        sl = slice(c * n_chunk, (c + 1) * n_chunk)
        u_ref[:, sl] = _dot(xn, w_ref[:, sl]).astype(BF16)
    g_ref[...] = _dot(xn, wg_ref[...])


def _inproj(h, nw, w_main, w_gate, tm=512):
    t = h.shape[0]
    return pl.pallas_call(
        functools.partial(_inproj_kernel, n_chunk=1152),
        grid=(t // tm,),
        in_specs=[
            pl.BlockSpec((tm, D_MODEL), lambda i: (i, 0)),
            pl.BlockSpec((1, D_MODEL), lambda i: (0, 0)),
            pl.BlockSpec((D_MODEL, D_MAIN), lambda i: (0, 0)),
            pl.BlockSpec((D_MODEL, N_GATE), lambda i: (0, 0)),
        ],
        out_specs=[
            pl.BlockSpec((tm, D_MAIN), lambda i: (i, 0)),
            pl.BlockSpec((tm, N_GATE), lambda i: (i, 0)),
        ],
        out_shape=[jax.ShapeDtypeStruct((t, D_MAIN), BF16), jax.ShapeDtypeStruct((t, N_GATE), F32)],
        compiler_params=_cparams("parallel"),
        name="inproj",
    )(h, nw, w_main, w_gate)


def _gates_kernel(g_ref, b_ref, go_ref, gt_ref, *, s):
    L = ML_CHUNK
    col = lax.broadcasted_iota(jnp.int32, (L, LANES), 1)
    row = lax.broadcasted_iota(jnp.int32, (L, LANES), 0)
    is_a = (col >= G_MLA) & (col < G_MLB)
    r = lax.broadcasted_iota(jnp.int32, (L, L), 0)
    c = lax.broadcasted_iota(jnp.int32, (L, L), 1)
    tri = jnp.where(r >= c, 1.0, 0.0).astype(BF16)
    bias = b_ref[...]

    def chunk(ci, carry):
        start = pl.multiple_of(ci * L, L)
        pre = g_ref[pl.ds(start, L), :] + bias
        cum = _dot_exact(tri, jnp.where(is_a, 0.0, _log_sigmoid(pre)), lhs_exact=True)
        tot = cum + carry
        a = pre - pltpu.roll(cum, LANES - (G_MLB - G_MLA), 1)
        pm = jnp.where(is_a, a, -jnp.inf)
        k = 1
        while k < L:
            pm = jnp.maximum(pm, jnp.where(row >= k, pltpu.roll(pm, k, 0), -jnp.inf))
            k *= 2
        pm = pltpu.roll(pm, G_MLP - G_MLA, 1)
        out = jnp.where(col < G_MLA, tot,
                        jnp.where(is_a, a, jnp.where(col < G_MLP, cum, jnp.where(col < G_MLP + ML_HEADS, pm, 0.0))))
        go_ref[pl.ds(start, L), :] = out
        gt_ref[:, pl.ds(start, L)] = out.T[:GT_ROWS, :]
        return tot[L - 1:L, :]

    lax.fori_loop(0, s // L, chunk, jnp.zeros((1, LANES), F32), unroll=4)


def _gates(g, bias, bsz, s):
    return pl.pallas_call(
        functools.partial(_gates_kernel, s=s),
        grid=(bsz,),
        in_specs=[
            pl.BlockSpec((s, N_GATE), lambda b: (b, 0)),
            pl.BlockSpec((1, N_GATE), lambda b: (0, 0)),
        ],
        out_specs=[
            pl.BlockSpec((s, N_GATE), lambda b: (b, 0)),
            pl.BlockSpec((None, GT_ROWS, s), lambda b: (b, 0, 0)),
        ],
        out_shape=[jax.ShapeDtypeStruct((bsz * s, N_GATE), F32), jax.ShapeDtypeStruct((bsz, GT_ROWS, s), F32)],
        compiler_params=_cparams("parallel"),
        name="gates",
    )(g, bias)


def _head_masks(q2):
    lane = lax.broadcasted_iota(jnp.int32, q2.shape, 1)
    zero = jnp.zeros_like(q2)
    return jnp.where(lane < HEAD_DIM, q2, zero), jnp.where(lane >= HEAD_DIM, q2, zero)


def _stack_heads(q2):
    return jnp.concatenate(_head_masks(q2), axis=0)


def _unstack_heads(x, rows):
    lane = lax.broadcasted_iota(jnp.int32, (rows, LANES), 1)
    return jnp.where(lane < HEAD_DIM, x[:rows], x[rows:])


def _fox_kernel(q_ref, k_ref, v_ref, g_ref, gt_ref, o_ref, ct_ref, m_ref, acc_ref, *, tq):
    p = pl.program_id(1)
    qi = pl.program_id(2)
    qs = _stack_heads(q_ref[...] * jnp.asarray(HEAD_DIM ** -0.5, BF16))
    gq = g_ref[...]
    lane = lax.broadcasted_iota(jnp.int32, (tq, LANES), 1)
    lo_lanes = lane < HEAD_DIM
    for hh in range(2):
        ct = jnp.sum(jnp.where(lane == G_FOX + 2 * p + hh, gq, 0.0), axis=1, keepdims=True)
        ct_ref[hh * tq:(hh + 1) * tq, :] = jnp.broadcast_to(ct * LOG2E, (tq, LANES))
    m_ref[...] = jnp.full_like(m_ref, NEG)
    acc_ref[...] = jnp.zeros_like(acc_ref)
    reps = tq // LANES

    def group(j, masked):
        start = pl.multiple_of(j * tq, tq)
        s = _dot_nt(qs, k_ref[pl.ds(start, tq), :])
        v2 = v_ref[pl.ds(start, tq), :]
        one = jnp.ones_like(v2)
        v_ones = (jnp.where(lo_lanes, v2, one), jnp.where(lo_lanes, one, v2))
        if masked:
            rq = lax.broadcasted_iota(jnp.int32, (tq, tq), 0)
            rk = lax.broadcasted_iota(jnp.int32, (tq, tq), 1)
            causal = jnp.where(rk <= rq, 0.0, NEG)
        for hh in range(2):
            rows = slice(hh * tq, (hh + 1) * tq)
            cs = gt_ref[pl.ds(G_FOX + 2 * p + hh, 1), pl.ds(start, tq)] * LOG2E
            r = s[rows] * LOG2E - cs
            if masked:
                r = r + causal
            ct = ct_ref[rows, :]
            m_prev = m_ref[rows, :]
            m_next = jnp.maximum(m_prev, jnp.max(r, axis=1, keepdims=True) + ct)
            pr = jnp.exp2(r - jnp.tile(m_next - ct, (1, reps)))
            alpha = jnp.exp2(m_prev - m_next)
            acc_ref[rows, :] = alpha * acc_ref[rows, :] + _dot(pr.astype(BF16), v_ones[hh])
            m_ref[rows, :] = m_next

    def body(j, carry):
        group(j, False)
        return carry

    lax.fori_loop(0, qi, body, 0)
    group(qi, True)
    acc = acc_ref[...]
    o_ref[...] = _unstack_heads(acc / pltpu.roll(acc, HEAD_DIM, 1), tq).astype(BF16)


def _fox(u, g, gt, bsz, s, tq=512):
    nq = s // tq
    stat = pltpu.VMEM((2 * tq, LANES), F32)
    return pl.pallas_call(
        functools.partial(_fox_kernel, tq=tq),
        grid=(bsz, FOX_HEADS // 2, nq),
        in_specs=[
            pl.BlockSpec((tq, PAIR), lambda b, p, i: (b * nq + i, FOX_Q + p)),
            pl.BlockSpec((s, PAIR), lambda b, p, i: (b, FOX_K + p)),
            pl.BlockSpec((s, PAIR), lambda b, p, i: (b, FOX_V + p)),
            pl.BlockSpec((tq, N_GATE), lambda b, p, i: (b * nq + i, 0)),
            pl.BlockSpec((None, 8, s), lambda b, p, i: (b, 0, 0)),
        ],
        out_specs=pl.BlockSpec((tq, PAIR), lambda b, p, i: (b * nq + i, p)),
        out_shape=jax.ShapeDtypeStruct((bsz * s, D_FOX), BF16),
        scratch_shapes=[stat, stat, stat],
        compiler_params=_cparams("parallel", "parallel", "arbitrary"),
        name="fox",
    )(u, u, u, g, gt)


def _sb_kernel(q_ref, k_ref, v_ref, o_ref, run_ref, acc_ref, *, tq, tg):
    qi = pl.program_id(2)
    qs = _stack_heads(q_ref[...] * jnp.asarray(HEAD_DIM ** -0.5, BF16))
    r = lax.broadcasted_iota(jnp.int32, (2 * SB_SUB, 2 * SB_SUB), 0)
    c = lax.broadcasted_iota(jnp.int32, (2 * SB_SUB, 2 * SB_SUB), 1)
    suffix = jnp.where((c >= SB_SUB) | (jnp.where(r >= SB_SUB, r - SB_SUB, r) >= c), 1.0, 0.0).astype(BF16)
    run_ref[...] = jnp.zeros_like(run_ref)
    acc_ref[...] = jnp.zeros_like(acc_ref)
    n_diag = tq // tg

    def group(g, masked):
        start = pl.multiple_of((qi + 1) * tq - (g + 1) * tg, tg)
        z = _dot_nt(qs, k_ref[pl.ds(start, tg), :])
        sp = jnp.maximum(z, 0.0) + jnp.log(1.0 + jnp.exp(-jnp.abs(z)))
        if masked:
            rq = lax.broadcasted_iota(jnp.int32, (2 * tq, tg), 0)
            rk = lax.broadcasted_iota(jnp.int32, (2 * tq, tg), 1)
            valid = rk + (tq - (g + 1) * tg) < jnp.where(rq >= tq, rq - tq, rq)
            sp = jnp.where(valid, sp, 0.0)
        hi = sp.astype(BF16)
        lo = (sp - hi.astype(F32)).astype(BF16)
        offs = run_ref[...]
        parts = [None] * (tg // SB_SUB)
        for sb in reversed(range(tg // SB_SUB)):
            sl = slice(sb * SB_SUB, (sb + 1) * SB_SUB)
            rs = _dot(jnp.concatenate([hi[:, sl], lo[:, sl]], axis=1), suffix)
            a = jnp.exp(z[:, sl] - rs[:, :SB_SUB] - offs)
            if masked:
                a = jnp.where(valid[:, sl], a, 0.0)
            parts[sb] = a.astype(BF16)
            offs = offs + rs[:, SB_SUB:]
        run_ref[...] = offs
        acc_ref[...] += _dot(jnp.concatenate(parts, axis=1), v_ref[pl.ds(start, tg), :])
        return jnp.min(offs)

    for g in range(n_diag):
        low = group(g, True)

    def cond(carry):
        g, low = carry
        return (g < (qi + 1) * n_diag) & (low < SB_EXIT)

    def body(carry):
        g, _ = carry
        return g + 1, group(g, False)

    lax.while_loop(cond, body, (jnp.int32(n_diag), low))
    o_ref[...] = _unstack_heads(acc_ref[...], tq).astype(BF16)


def _sb(u, bsz, s, tq=512, tg=256):
    nq = s // tq
    stat = pltpu.VMEM((2 * tq, LANES), F32)
    return pl.pallas_call(
        functools.partial(_sb_kernel, tq=tq, tg=tg),
        grid=(bsz, SB_HEADS // 2, nq),
        in_specs=[
            pl.BlockSpec((tq, PAIR), lambda b, p, i: (b * nq + i, SB_Q + p)),
            pl.BlockSpec((s, PAIR), lambda b, p, i: (b, SB_K + p)),
            pl.BlockSpec((s, PAIR), lambda b, p, i: (b, SB_V + p)),
        ],
        out_specs=pl.BlockSpec((tq, PAIR), lambda b, p, i: (b * nq + i, p)),
        out_shape=jax.ShapeDtypeStruct((bsz * s, D_SB), BF16),
        scratch_shapes=[stat, stat],
        compiler_params=_cparams("parallel", "parallel", "arbitrary"),
        name="sb",
    )(u, u, u)


def _mlstm_kernel(xq_ref, xk_ref, v_ref, og_ref, g_ref, gt_ref, cw_ref, nw_ref, o_ref, qs_ref, ks_ref, e_ref, *, s):
    L = ML_CHUNK
    nc = s // L
    TAIL = 16
    row = lax.broadcasted_iota(jnp.int32, (TAIL, D_ML), 0)

    def conv_chunk(c, _):
        start = pl.multiple_of(c * L, L)
        prev_start = pl.multiple_of(jnp.maximum(c - 1, 0) * L, L)
        for x_ref, dst_ref, w0, scale in ((xq_ref, qs_ref, 0, 1.0), (xk_ref, ks_ref, D_ML, HEAD_DIM ** -0.5)):
            cur = x_ref[pl.ds(start, L), :].astype(F32)
            tail = x_ref[pl.ds(prev_start + L - TAIL, TAIL), :].astype(F32)
            tail = jnp.where(c > 0, tail, 0.0)
            w = cw_ref[:, w0:w0 + D_ML]
            y = cur * w[CONV_WIDTH - 1:CONV_WIDTH, :]
            for k in range(1, CONV_WIDTH):
                sh = pltpu.roll(cur, k, 0)
                head = jnp.where(row < k, pltpu.roll(tail, k, 0), sh[:TAIL])
                sh = jnp.concatenate([head, sh[TAIL:]], axis=0)
                y = y + sh * w[CONV_WIDTH - 1 - k:CONV_WIDTH - k, :]
            y = y * jax.nn.sigmoid(y) * scale
            dst_ref[pl.ds(start, L), :] = y.astype(BF16)
        return 0

    lax.fori_loop(0, nc, conv_chunk, 0)

    ci = lax.broadcasted_iota(jnp.int32, (LANES, N_EXP), 0)
    cj = lax.broadcasted_iota(jnp.int32, (LANES, N_EXP), 1)
    src_col = jnp.where(cj < EXP_B, G_MLP + lax.shift_right_logical(cj, 7),
                        jnp.where(cj < EXP_A, G_MLB + lax.shift_right_logical(cj - EXP_B, 6),
                                  G_MLA + lax.shift_right_logical(cj - EXP_A, 6)))
    e_ref[...] = jnp.where(ci == src_col, 1.0, 0.0).astype(BF16)

    lane = lax.broadcasted_iota(jnp.int32, (L, LANES), 1)
    lo_lanes = lane < HEAD_DIM
    r_ll = lax.broadcasted_iota(jnp.int32, (L, L), 0)
    c_ll = lax.broadcasted_iota(jnp.int32, (L, L), 1)
    tri = r_ll >= c_ll
    blockdiag = (r_ll < HEAD_DIM) == (c_ll < HEAD_DIM)
    bd_ones = jnp.where(blockdiag, 1.0, 0.0).astype(BF16)
    ones = jnp.ones((L, LANES), BF16)

    def chunk(c, carry):
        start = pl.multiple_of(c * L, L)
        rep = _dot_exact(g_ref[pl.ds(start, L), :], e_ref[...], lhs_exact=False)
        gtc = gt_ref[:, pl.ds(start, L)]
        new_carry = []
        for p in range(ML_HEADS // 2):
            c2, nm, m2 = carry[p]
            sl = slice(p * PAIR, (p + 1) * PAIR)
            q2 = qs_ref[pl.ds(start, L), sl]
            k2 = ks_ref[pl.ds(start, L), sl]
            v_one = jnp.concatenate([v_ref[pl.ds(start, L), sl], ones], axis=1)
            b2 = rep[:, EXP_B + p * PAIR:EXP_B + (p + 1) * PAIR]
            a2 = rep[:, EXP_A + p * PAIR:EXP_A + (p + 1) * PAIR]
            pm = [rep[:, (2 * p + hh) * LANES:(2 * p + hh + 1) * LANES] for hh in range(2)]
            pm2 = jnp.where(lo_lanes, pm[0], pm[1])
            qk = _dot_nt(_stack_heads(q2), k2)
            sc = []
            for hh in range(2):
                a_row = gtc[G_MLA + 2 * p + hh:G_MLA + 2 * p + hh + 1, :]
                sc.append(qk[hh * L:(hh + 1) * L] * jnp.exp(jnp.where(tri, a_row - pm[hh], -jnp.inf)))
            r = _dot(jnp.concatenate(sc, axis=0).astype(BF16), v_one)
            num_loc = jnp.where(lo_lanes, r[:L, :LANES], r[L:, :LANES])
            den_loc = jnp.where(lo_lanes, r[:L, LANES:], r[L:, LANES:])
            m_loc = b2 + pm2
            inter = _dot(q2, jnp.concatenate([c2, nm], axis=1).astype(BF16))
            it2 = b2 + m2
            m_t = jnp.maximum(it2, m_loc)
            f_loc = jnp.exp(m_loc - m_t)
            w_int = jnp.exp(it2 - m_t)
            num = num_loc * f_loc + w_int * inter[:, :LANES]
            den = den_loc * f_loc + w_int * inter[:, LANES:]
            hp = num / jnp.maximum(jnp.abs(den), jnp.exp(-m_t))
            sq = hp * hp
            sq_hi = sq.astype(BF16)
            sq_lo = (sq - sq_hi.astype(F32)).astype(BF16)
            msq = (_dot(sq_hi, bd_ones) + _dot(sq_lo, bd_ones)) * (1.0 / HEAD_DIM)
            hp = hp * lax.rsqrt(msq + RMS_EPS) * nw_ref[:, sl]
            gate = jax.nn.sigmoid(og_ref[pl.ds(start, L), sl].astype(F32))
            o_ref[pl.ds(start, L), sl] = (gate * hp).astype(BF16)
            g2 = b2[L - 1:L, :]
            pl2 = pm2[L - 1:L, :]
            kw = (k2.astype(F32) * jnp.exp(a2 - pl2)).astype(BF16)
            upd = _dot_tn(kw, v_one)
            m_new = jnp.maximum(g2 + m2, g2 + pl2)
            decay = jnp.exp(g2 + m2 - m_new)
            scale = jnp.exp(g2 + pl2 - m_new)
            c2 = decay * c2 + scale * jnp.where(blockdiag, upd[:, :LANES], 0.0)
            nm = decay * nm + scale * jnp.where(blockdiag, upd[:, LANES:], 0.0)
            new_carry.append((c2, nm, m_new))
        return tuple(new_carry)

    init = tuple((jnp.zeros((LANES, LANES), F32), jnp.zeros((LANES, LANES), F32), jnp.zeros((1, LANES), F32))
                 for _ in range(ML_HEADS // 2))
    lax.fori_loop(0, nc, chunk, init)


def _mlstm(u, g, gt, conv_w, norm_w, bsz, s):
    blk = lambda j: pl.BlockSpec((s, D_ML), lambda b: (b, j))
    first = (3 * D_FOX + 3 * D_SB) // D_ML
    return pl.pallas_call(
        functools.partial(_mlstm_kernel, s=s),
        grid=(bsz,),
        in_specs=[
            blk(first), blk(first + 1), blk(first + 2), blk(first + 3),
            pl.BlockSpec((s, N_GATE), lambda b: (b, 0)),
            pl.BlockSpec((None, GT_ROWS, s), lambda b: (b, 0, 0)),
            pl.BlockSpec((CONV_WIDTH, 2 * D_ML), lambda b: (0, 0)),
            pl.BlockSpec((1, D_ML), lambda b: (0, 0)),
        ],
        out_specs=pl.BlockSpec((s, D_ML), lambda b: (b, 0)),
        out_shape=jax.ShapeDtypeStruct((bsz * s, D_ML), BF16),
        scratch_shapes=[pltpu.VMEM((s, D_ML), BF16), pltpu.VMEM((s, D_ML), BF16), pltpu.VMEM((LANES, N_EXP), BF16)],
        compiler_params=_cparams("parallel"),
        name="mlstm",
    )(u, u, u, u, g, gt, conv_w, norm_w)


def _outproj_kernel(yf_ref, ys_ref, ym_ref, wf_ref, ws_ref, wm_ref, h_ref, o_ref):
    acc = _dot(yf_ref[...], wf_ref[...]) + _dot(ys_ref[...], ws_ref[...]) + _dot(ym_ref[...], wm_ref[...])
    o_ref[...] = h_ref[...] + acc


def _outproj(yf, ys, ym, wf, ws, wm, h, tm=512):
    t = h.shape[0]
    row = lambda n: pl.BlockSpec((tm, n), lambda i: (i, 0))
    full = lambda n: pl.BlockSpec((n, D_MODEL), lambda i: (0, 0))
    return pl.pallas_call(
        _outproj_kernel,
        grid=(t // tm,),
        in_specs=[row(D_FOX), row(D_SB), row(D_ML), full(D_FOX), full(D_SB), full(D_ML), row(D_MODEL)],
        out_specs=row(D_MODEL),
        out_shape=jax.ShapeDtypeStruct((t, D_MODEL), F32),
        compiler_params=_cparams("parallel"),
        name="outproj",
    )(yf, ys, ym, wf, ws, wm, h)


def _memkv_kernel(x_ref, nw_ref, w_ref, o_ref):
    xn = _rms(x_ref[...], nw_ref[...]).astype(BF16)
    o_ref[...] = _dot(xn, w_ref[...]).astype(BF16)


def _memkv(mem2d, nw, w, tm=512):
    t = mem2d.shape[0]
    return pl.pallas_call(
        _memkv_kernel,
        grid=(t // tm,),
        in_specs=[
            pl.BlockSpec((tm, D_MODEL), lambda i: (i, 0)),
            pl.BlockSpec((1, D_MODEL), lambda i: (0, 0)),
            pl.BlockSpec((D_MODEL, 2 * D_X), lambda i: (0, 0)),
        ],
        out_specs=pl.BlockSpec((tm, 2 * D_X), lambda i: (i, 0)),
        out_shape=jax.ShapeDtypeStruct((t, 2 * D_X), BF16),
        compiler_params=_cparams("parallel"),
        name="memkv",
    )(mem2d, nw, w)


def _cross_kernel(h_ref, nw_ref, wq_ref, kv_ref, wo_ref, o_ref):
    hf = h_ref[...]
    hn = _rms(hf, nw_ref[...]).astype(BF16)
    q = _dot(hn, wq_ref[...]).astype(BF16)
    outs = []
    for hd in range(X_HEADS):
        sl = slice(hd * X_HEAD_DIM, (hd + 1) * X_HEAD_DIM)
        k = kv_ref[:, sl]
        v = kv_ref[:, D_X + hd * X_HEAD_DIM:D_X + (hd + 1) * X_HEAD_DIM]
        s = _dot_nt(q[:, sl], k) * (X_HEAD_DIM ** -0.5)
        m = jnp.max(s, axis=1, keepdims=True)
        e = jnp.exp(s - m)
        pr = e / jnp.sum(e, axis=1, keepdims=True)
        outs.append(_dot(pr.astype(BF16), v).astype(BF16))
    o = jnp.concatenate(outs, axis=1)
    o_ref[...] = hf + _dot(o, wo_ref[...])


def _cross(h, nw, wq, kv, wo, bsz, s, mlen, tm=512):
    nt = s // tm
    return pl.pallas_call(
        _cross_kernel,
        grid=(bsz, nt),
        in_specs=[
            pl.BlockSpec((tm, D_MODEL), lambda b, i: (b * nt + i, 0)),
            pl.BlockSpec((1, D_MODEL), lambda b, i: (0, 0)),
            pl.BlockSpec((D_MODEL, D_X), lambda b, i: (0, 0)),
            pl.BlockSpec((mlen, 2 * D_X), lambda b, i: (b, 0)),
            pl.BlockSpec((D_X, D_MODEL), lambda b, i: (0, 0)),
        ],
        out_specs=pl.BlockSpec((tm, D_MODEL), lambda b, i: (b * nt + i, 0)),
        out_shape=jax.ShapeDtypeStruct((bsz * s, D_MODEL), F32),
        compiler_params=_cparams("parallel", "parallel"),
        name="cross",
    )(h, nw, wq, kv, wo)


def _ffn_kernel(h_ref, nw_ref, wg_ref, wu_ref, wd_ref, fw_ref, o_ref, acc_ref, *, f_chunk, final_norm):
    hf = h_ref[...]
    hn = _rms(hf, nw_ref[...]).astype(BF16)
    for c in range(D_FF // f_chunk):
        sl = slice(c * f_chunk, (c + 1) * f_chunk)
        g = _dot(hn, wg_ref[:, sl])
        u = _dot(hn, wu_ref[:, sl])
        a = (g * jax.nn.sigmoid(g) * u).astype(BF16)
        part = _dot(a, wd_ref[sl, :])
        if c == 0:
            acc_ref[...] = part
        else:
            acc_ref[...] += part
    out = hf + acc_ref[...]
    if final_norm:
        out = _rms(out, fw_ref[...])
    o_ref[...] = out


def _ffn(h, nw, wg, wu, wd, fw, final_norm, tm=512):
    t = h.shape[0]
    const = lambda shape: pl.BlockSpec(shape, lambda i: (0, 0))
    return pl.pallas_call(
        functools.partial(_ffn_kernel, f_chunk=256, final_norm=final_norm),
        grid=(t // tm,),
        in_specs=[
            pl.BlockSpec((tm, D_MODEL), lambda i: (i, 0)),
            const((1, D_MODEL)),
            const((D_MODEL, D_FF)), const((D_MODEL, D_FF)), const((D_FF, D_MODEL)),
            const((1, D_MODEL)),
        ],
        out_specs=pl.BlockSpec((tm, D_MODEL), lambda i: (i, 0)),
        out_shape=jax.ShapeDtypeStruct((t, D_MODEL), F32),
        scratch_shapes=[pltpu.VMEM((tm, D_MODEL), F32)],
        compiler_params=_cparams("parallel"),
        name="ffn",
    )(h, nw, wg, wu, wd, fw)


N_FOX_QKV = 3 * D_FOX
N_REST = 3 * D_SB + 4 * D_ML


N_USED_GATES = FOX_HEADS + 2 * ML_HEADS


def _regroup_kernel(w_ref, o_ref, g_ref):
    o_ref[:, :N_FOX_QKV] = w_ref[:, :N_FOX_QKV].astype(BF16)
    rest = w_ref[:, N_FOX_QKV + FOX_HEADS:N_FOX_QKV + FOX_HEADS + N_REST]
    o_ref[:, N_FOX_QKV:] = rest.astype(BF16)
    ml0 = N_FOX_QKV + N_REST
    lane = lax.broadcasted_iota(jnp.int32, (w_ref.shape[0], N_USED_GATES), 1)
    gates = jnp.where(lane < FOX_HEADS, w_ref[:, N_FOX_QKV:N_FOX_QKV + N_USED_GATES], w_ref[:, ml0:ml0 + N_USED_GATES])
    g_ref[...] = jnp.zeros_like(g_ref)
    g_ref[:, :N_USED_GATES] = gates.astype(BF16)


def _regroup_w_in(w, tr=256):
    depth, rows, n_in = w.shape
    assert (N_FOX_QKV + N_REST) % LANES == 0 and n_in == N_FOX_QKV + N_REST + N_USED_GATES
    return pl.pallas_call(
        _regroup_kernel,
        grid=(depth, rows // tr),
        in_specs=[pl.BlockSpec((None, tr, n_in), lambda l, i: (l, i, 0))],
        out_specs=[pl.BlockSpec((None, tr, D_MAIN), lambda l, i: (l, i, 0)),
                   pl.BlockSpec((None, tr, N_GATE), lambda l, i: (l, i, 0))],
        out_shape=[jax.ShapeDtypeStruct((depth, rows, D_MAIN), BF16),
                   jax.ShapeDtypeStruct((depth, rows, N_GATE), BF16)],
        compiler_params=_cparams("parallel", "parallel"),
        name="regroup",
    )(w)


def kernel(x, mem, norm_mix_w, w_in, fox_f_b, ml_conv_w, ml_i_b, ml_f_b, ml_norm_w, w_out, norm_x_w, mem_norm_w,
           wx_q, wx_kv, wx_o, norm_ffn_w, w_gate, w_up, w_down, final_norm_w):
    bsz, s, _ = x.shape
    mlen = mem.shape[1]
    depth = w_in.shape[0]
    h = x.reshape(bsz * s, D_MODEL)
    mem2d = mem.reshape(bsz * mlen, D_MODEL)
    row = lambda v: v.reshape(1, -1)
    w_main, w_g = _regroup_w_in(w_in)
    bias = jnp.concatenate([fox_f_b, ml_i_b, ml_f_b, jnp.zeros((depth, N_GATE - 3 * FOX_HEADS), F32)], axis=1)
    for l in range(depth):
        u, g_raw = _inproj(h, row(norm_mix_w[l]), w_main[l], w_g[l])
        g, gt = _gates(g_raw, row(bias[l]), bsz, s)
        y_fox = _fox(u, g, gt, bsz, s)
        y_sb = _sb(u, bsz, s)
        y_ml = _mlstm(u, g, gt, ml_conv_w[l], row(ml_norm_w[l]), bsz, s)
        wo = w_out[l].astype(BF16)
        h = _outproj(y_fox, y_sb, y_ml, wo[:D_FOX], wo[D_FOX:D_FOX + D_SB], wo[D_FOX + D_SB:], h)
        kv = _memkv(mem2d, row(mem_norm_w[l]), wx_kv[l].astype(BF16), tm=min(512, bsz * mlen))
        h = _cross(h, row(norm_x_w[l]), wx_q[l].astype(BF16), kv, wx_o[l].astype(BF16), bsz, s, mlen)
        h = _ffn(h, row(norm_ffn_w[l]), w_gate[l].astype(BF16), w_up[l].astype(BF16), w_down[l].astype(BF16),
                 row(final_norm_w), final_norm=(l == depth - 1))
    return h.reshape(bsz, s, D_MODEL)
```

```python
import functools

import jax
import jax.numpy as jnp
from jax import lax
from jax.experimental import pallas as pl
from jax.experimental.pallas import tpu as pltpu

D_MODEL = 1024
HEAD_DIM = 64
FOX_HEADS = 6
SB_HEADS = 4
ML_HEADS = 6
D_FOX = FOX_HEADS * HEAD_DIM
D_SB = SB_HEADS * HEAD_DIM
D_ML = ML_HEADS * HEAD_DIM
ML_CHUNK = 128
CONV_WIDTH = 4
X_HEADS = 4
X_HEAD_DIM = 128
D_X = X_HEADS * X_HEAD_DIM
D_FF = 2816
RMS_EPS = 1e-6

LANES = 128
PAIR = 2 * HEAD_DIM
D_MAIN = 3 * D_FOX + 3 * D_SB + 4 * D_ML
N_GATE = LANES
GT_ROWS = 32
NEG = -1e30
SB_SUB = 128
SB_EXIT = 110.0
LOG2E = 1.4426950408889634
VMEM_LIMIT = 56 * 1024 * 1024

BF16 = jnp.bfloat16
F32 = jnp.float32

FOX_Q, FOX_K, FOX_V = 0, 3, 6
SB_Q, SB_K, SB_V = 9, 11, 13
G_FOX, G_MLA, G_MLB, G_MLP = 0, 6, 12, 18
EXP_B = ML_HEADS * LANES
EXP_A = EXP_B + D_ML
N_EXP = EXP_A + D_ML


def _cparams(*sem):
    return pltpu.CompilerParams(dimension_semantics=sem, vmem_limit_bytes=VMEM_LIMIT)


def _rms(xf, w):
    ms = jnp.mean(xf * xf, axis=-1, keepdims=True)
    return xf * lax.rsqrt(ms + RMS_EPS) * w


def _dot(a, b):
    return jnp.dot(a, b, preferred_element_type=F32)


def _dot_nt(a, b):
    return lax.dot_general(a, b, (((1,), (1,)), ((), ())), preferred_element_type=F32)


def _dot_tn(a, b):
    return lax.dot_general(a, b, (((0,), (0,)), ((), ())), preferred_element_type=F32)


def _log_sigmoid(x):
    return jnp.minimum(x, 0.0) - jnp.log(1.0 + jnp.exp(-jnp.abs(x)))


def _split3(x):
    x1 = x.astype(BF16)
    r1 = x - x1.astype(F32)
    x2 = r1.astype(BF16)
    return x1, x2, (r1 - x2.astype(F32)).astype(BF16)


def _dot_exact(a, b, lhs_exact):
    if lhs_exact:
        x1, x2, x3 = _split3(b)
        return _dot(a, x1) + _dot(a, x2) + _dot(a, x3)
    x1, x2, x3 = _split3(a)
    return _dot(x1, b) + _dot(x2, b) + _dot(x3, b)


def _inproj_kernel(x_ref, nw_ref, w_ref, u_ref, g_ref, *, n_chunk):
    xn = _rms(x_ref[...], nw_ref[...]).astype(BF16)
    n_all = D_MAIN + N_GATE
    for c in range(n_all // n_chunk):
        lo = c * n_chunk
        res = _dot(xn, w_ref[:, lo:lo + n_chunk])
        if lo + n_chunk <= D_MAIN:
            u_ref[:, lo:lo + n_chunk] = res.astype(BF16)
        else:
            u_ref[:, lo:D_MAIN] = res[:, :D_MAIN - lo].astype(BF16)
            g_ref[...] = res[:, D_MAIN - lo:]


def _inproj(h, nw, w_all, layer, tm=512):
    t = h.shape[0]
    return pl.pallas_call(
        functools.partial(_inproj_kernel, n_chunk=512),
        grid=(t // tm,),
        in_specs=[
            pl.BlockSpec((tm, D_MODEL), lambda i: (i, 0)),
            pl.BlockSpec((1, D_MODEL), lambda i: (0, 0)),
            pl.BlockSpec((None, D_MODEL, D_MAIN + N_GATE), lambda i: (layer, 0, 0)),
        ],
        out_specs=[
            pl.BlockSpec((tm, D_MAIN), lambda i: (i, 0)),
            pl.BlockSpec((tm, N_GATE), lambda i: (i, 0)),
        ],
        out_shape=[jax.ShapeDtypeStruct((t, D_MAIN), BF16), jax.ShapeDtypeStruct((t, N_GATE), F32)],
        compiler_params=_cparams("parallel"),
        name="inproj",
    )(h, nw, w_all)


def _gates_kernel(g_ref, b_ref, go_ref, gt_ref, *, s):
    L = ML_CHUNK
    col = lax.broadcasted_iota(jnp.int32, (L, LANES), 1)
    row = lax.broadcasted_iota(jnp.int32, (L, LANES), 0)
    is_a = (col >= G_MLA) & (col < G_MLB)
    r = lax.broadcasted_iota(jnp.int32, (L, L), 0)
    c = lax.broadcasted_iota(jnp.int32, (L, L), 1)
    tri = jnp.where(r >= c, 1.0, 0.0).astype(BF16)
    bias = b_ref[...]

    def chunk(ci, carry):
        start = pl.multiple_of(ci * L, L)
        pre = g_ref[pl.ds(start, L), :] + bias
        cum = _dot_exact(tri, jnp.where(is_a, 0.0, _log_sigmoid(pre)), lhs_exact=True)
        tot = cum + carry
        a = pre - pltpu.roll(cum, LANES - (G_MLB - G_MLA), 1)
        pm = jnp.where(is_a, a, -jnp.inf)
        k = 1
        while k < L:
            pm = jnp.maximum(pm, jnp.where(row >= k, pltpu.roll(pm, k, 0), -jnp.inf))
            k *= 2
        pm = pltpu.roll(pm, G_MLP - G_MLA, 1)
        out = jnp.where(col < G_MLA, tot,
                        jnp.where(is_a, a, jnp.where(col < G_MLP, cum, jnp.where(col < G_MLP + ML_HEADS, pm, 0.0))))
        go_ref[pl.ds(start, L), :] = out
        gt_ref[:, pl.ds(start, L)] = out.T[:GT_ROWS, :]
        return tot[L - 1:L, :]

    lax.fori_loop(0, s // L, chunk, jnp.zeros((1, LANES), F32), unroll=4)


def _gates(g, bias, bsz, s):
    return pl.pallas_call(
        functools.partial(_gates_kernel, s=s),
        grid=(bsz,),
        in_specs=[
            pl.BlockSpec((s, N_GATE), lambda b: (b, 0)),
            pl.BlockSpec((1, N_GATE), lambda b: (0, 0)),
        ],
        out_specs=[
            pl.BlockSpec((s, N_GATE), lambda b: (b, 0)),
            pl.BlockSpec((None, GT_ROWS, s), lambda b: (b, 0, 0)),
        ],
        out_shape=[jax.ShapeDtypeStruct((bsz * s, N_GATE), F32), jax.ShapeDtypeStruct((bsz, GT_ROWS, s), F32)],
        compiler_params=_cparams("parallel"),
        name="gates",
    )(g, bias)


def _head_masks(q2):
    lane = lax.broadcasted_iota(jnp.int32, q2.shape, 1)
    zero = jnp.zeros_like(q2)
    return jnp.where(lane < HEAD_DIM, q2, zero), jnp.where(lane >= HEAD_DIM, q2, zero)


def _stack_heads(q2):
    return jnp.concatenate(_head_masks(q2), axis=0)


def _unstack_heads(x, rows):
    lane = lax.broadcasted_iota(jnp.int32, (rows, LANES), 1)
    return jnp.where(lane < HEAD_DIM, x[:rows], x[rows:])


def _fox_kernel(q_ref, k_ref, v_ref, g_ref, gt_ref, o_ref, ct_ref, m_ref, acc_ref, *, tq):
    p = pl.program_id(1)
    qi = pl.program_id(2)
    qs = _stack_heads(q_ref[...] * jnp.asarray(HEAD_DIM ** -0.5, BF16))
    gq = g_ref[...]
    lane = lax.broadcasted_iota(jnp.int32, (tq, LANES), 1)
    lo_lanes = lane < HEAD_DIM
    for hh in range(2):
        ct = jnp.sum(jnp.where(lane == G_FOX + 2 * p + hh, gq, 0.0), axis=1, keepdims=True)
        ct_ref[hh * tq:(hh + 1) * tq, :] = jnp.broadcast_to(ct * LOG2E, (tq, LANES))
    m_ref[...] = jnp.full_like(m_ref, NEG)
    acc_ref[...] = jnp.zeros_like(acc_ref)
    reps = tq // LANES

    def group(j, masked):
        start = pl.multiple_of(j * tq, tq)
        s = _dot_nt(qs, k_ref[pl.ds(start, tq), :])
        v2 = v_ref[pl.ds(start, tq), :]
        one = jnp.ones_like(v2)
        v_ones = (jnp.where(lo_lanes, v2, one), jnp.where(lo_lanes, one, v2))
        if masked:
            rq = lax.broadcasted_iota(jnp.int32, (tq, tq), 0)
            rk = lax.broadcasted_iota(jnp.int32, (tq, tq), 1)
            causal = jnp.where(rk <= rq, 0.0, NEG)
        for hh in range(2):
            rows = slice(hh * tq, (hh + 1) * tq)
            cs = gt_ref[pl.ds(G_FOX + 2 * p + hh, 1), pl.ds(start, tq)] * LOG2E
            r = s[rows] * LOG2E - cs
            if masked:
                r = r + causal
            ct = ct_ref[rows, :]
            m_prev = m_ref[rows, :]
            m_next = jnp.maximum(m_prev, jnp.max(r, axis=1, keepdims=True) + ct)
            pr = jnp.exp2(r - jnp.tile(m_next - ct, (1, reps)))
            alpha = jnp.exp2(m_prev - m_next)
            acc_ref[rows, :] = alpha * acc_ref[rows, :] + _dot(pr.astype(BF16), v_ones[hh])
            m_ref[rows, :] = m_next

    def body(j, carry):
        group(j, False)
        return carry

    lax.fori_loop(0, qi, body, 0)
    group(qi, True)
    acc = acc_ref[...]
    o_ref[...] = _unstack_heads(acc / pltpu.roll(acc, HEAD_DIM, 1), tq).astype(BF16)


def _fox(u, g, gt, bsz, s, tq=512):
    nq = s // tq
    stat = pltpu.VMEM((2 * tq, LANES), F32)
    return pl.pallas_call(
        functools.partial(_fox_kernel, tq=tq),
        grid=(bsz, FOX_HEADS // 2, nq),
        in_specs=[
            pl.BlockSpec((tq, PAIR), lambda b, p, i: (b * nq + i, FOX_Q + p)),
            pl.BlockSpec((s, PAIR), lambda b, p, i: (b, FOX_K + p)),
            pl.BlockSpec((s, PAIR), lambda b, p, i: (b, FOX_V + p)),
            pl.BlockSpec((tq, N_GATE), lambda b, p, i: (b * nq + i, 0)),
            pl.BlockSpec((None, 8, s), lambda b, p, i: (b, 0, 0)),
        ],
        out_specs=pl.BlockSpec((tq, PAIR), lambda b, p, i: (b * nq + i, p)),
        out_shape=jax.ShapeDtypeStruct((bsz * s, D_FOX), BF16),
        scratch_shapes=[stat, stat, stat],
        compiler_params=_cparams("parallel", "parallel", "arbitrary"),
        name="fox",
    )(u, u, u, g, gt)


def _sb_kernel(q_ref, k_ref, v_ref, o_ref, run_ref, acc_ref, *, tq, tg):
    qi = pl.program_id(2)
    qs = _stack_heads(q_ref[...] * jnp.asarray(HEAD_DIM ** -0.5, BF16))
    r = lax.broadcasted_iota(jnp.int32, (2 * SB_SUB, 2 * SB_SUB), 0)
    c = lax.broadcasted_iota(jnp.int32, (2 * SB_SUB, 2 * SB_SUB), 1)
    suffix = jnp.where((c >= SB_SUB) | (jnp.where(r >= SB_SUB, r - SB_SUB, r) >= c), 1.0, 0.0).astype(BF16)
    run_ref[...] = jnp.zeros_like(run_ref)
    acc_ref[...] = jnp.zeros_like(acc_ref)
    n_diag = tq // tg

    def group(g, masked):
        start = pl.multiple_of((qi + 1) * tq - (g + 1) * tg, tg)
        z = _dot_nt(qs, k_ref[pl.ds(start, tg), :])
        sp = jnp.maximum(z, 0.0) + jnp.log(1.0 + jnp.exp(-jnp.abs(z)))
        if masked:
            rq = lax.broadcasted_iota(jnp.int32, (2 * tq, tg), 0)
            rk = lax.broadcasted_iota(jnp.int32, (2 * tq, tg), 1)
            valid = rk + (tq - (g + 1) * tg) < jnp.where(rq >= tq, rq - tq, rq)
            sp = jnp.where(valid, sp, 0.0)
        hi = sp.astype(BF16)
        lo = (sp - hi.astype(F32)).astype(BF16)
        offs = run_ref[...]
        parts = [None] * (tg // SB_SUB)
        for sb in reversed(range(tg // SB_SUB)):
            sl = slice(sb * SB_SUB, (sb + 1) * SB_SUB)
            rs = _dot(jnp.concatenate([hi[:, sl], lo[:, sl]], axis=1), suffix)
            a = jnp.exp(z[:, sl] - rs[:, :SB_SUB] - offs)
            if masked:
                a = jnp.where(valid[:, sl], a, 0.0)
            parts[sb] = a.astype(BF16)
            offs = offs + rs[:, SB_SUB:]
        run_ref[...] = offs
        acc_ref[...] += _dot(jnp.concatenate(parts, axis=1), v_ref[pl.ds(start, tg), :])
        return jnp.min(offs)

    for g in range(n_diag):
        low = group(g, True)

    def cond(carry):
        g, low = carry
        return (g < (qi + 1) * n_diag) & (low < SB_EXIT)

    def body(carry):
        g, _ = carry
        return g + 1, group(g, False)

    lax.while_loop(cond, body, (jnp.int32(n_diag), low))
    o_ref[...] = _unstack_heads(acc_ref[...], tq).astype(BF16)


def _sb(u, bsz, s, tq=512, tg=256):
    nq = s // tq
    stat = pltpu.VMEM((2 * tq, LANES), F32)
    return pl.pallas_call(
        functools.partial(_sb_kernel, tq=tq, tg=tg),
        grid=(bsz, SB_HEADS // 2, nq),
        in_specs=[
            pl.BlockSpec((tq, PAIR), lambda b, p, i: (b * nq + i, SB_Q + p)),
            pl.BlockSpec((s, PAIR), lambda b, p, i: (b, SB_K + p)),
            pl.BlockSpec((s, PAIR), lambda b, p, i: (b, SB_V + p)),
        ],
        out_specs=pl.BlockSpec((tq, PAIR), lambda b, p, i: (b * nq + i, p)),
        out_shape=jax.ShapeDtypeStruct((bsz * s, D_SB), BF16),
        scratch_shapes=[stat, stat],
        compiler_params=_cparams("parallel", "parallel", "arbitrary"),
        name="sb",
    )(u, u, u)


def _mlstm_kernel(xq_ref, xk_ref, v_ref, og_ref, g_ref, gt_ref, cw_ref, nw_ref, o_ref, qs_ref, ks_ref, e_ref, *, s):
    L = ML_CHUNK
    nc = s // L
    TAIL = 16
    row = lax.broadcasted_iota(jnp.int32, (TAIL, D_ML), 0)

    def conv_chunk(c, _):
        start = pl.multiple_of(c * L, L)
        prev_start = pl.multiple_of(jnp.maximum(c - 1, 0) * L, L)
        for x_ref, dst_ref, w0, scale in ((xq_ref, qs_ref, 0, 1.0), (xk_ref, ks_ref, D_ML, HEAD_DIM ** -0.5)):
            cur = x_ref[pl.ds(start, L), :].astype(F32)
            tail = x_ref[pl.ds(prev_start + L - TAIL, TAIL), :].astype(F32)
            tail = jnp.where(c > 0, tail, 0.0)
            w = cw_ref[:, w0:w0 + D_ML]
            y = cur * w[CONV_WIDTH - 1:CONV_WIDTH, :]
            for k in range(1, CONV_WIDTH):
                sh = pltpu.roll(cur, k, 0)
                head = jnp.where(row < k, pltpu.roll(tail, k, 0), sh[:TAIL])
                sh = jnp.concatenate([head, sh[TAIL:]], axis=0)
                y = y + sh * w[CONV_WIDTH - 1 - k:CONV_WIDTH - k, :]
            y = y * jax.nn.sigmoid(y) * scale
            dst_ref[pl.ds(start, L), :] = y.astype(BF16)
        return 0

    lax.fori_loop(0, nc, conv_chunk, 0)

    ci = lax.broadcasted_iota(jnp.int32, (LANES, N_EXP), 0)
    cj = lax.broadcasted_iota(jnp.int32, (LANES, N_EXP), 1)
    src_col = jnp.where(cj < EXP_B, G_MLP + lax.shift_right_logical(cj, 7),
                        jnp.where(cj < EXP_A, G_MLB + lax.shift_right_logical(cj - EXP_B, 6),
                                  G_MLA + lax.shift_right_logical(cj - EXP_A, 6)))
    e_ref[...] = jnp.where(ci == src_col, 1.0, 0.0).astype(BF16)

    lane = lax.broadcasted_iota(jnp.int32, (L, LANES), 1)
    lo_lanes = lane < HEAD_DIM
    r_ll = lax.broadcasted_iota(jnp.int32, (L, L), 0)
    c_ll = lax.broadcasted_iota(jnp.int32, (L, L), 1)
    tri = r_ll >= c_ll
    blockdiag = (r_ll < HEAD_DIM) == (c_ll < HEAD_DIM)
    bd_ones = jnp.where(blockdiag, 1.0, 0.0).astype(BF16)
    ones = jnp.ones((L, LANES), BF16)

    def chunk(c, carry):
        start = pl.multiple_of(c * L, L)
        rep = _dot_exact(g_ref[pl.ds(start, L), :], e_ref[...], lhs_exact=False)
        gtc = gt_ref[:, pl.ds(start, L)]
        new_carry = []
        for p in range(ML_HEADS // 2):
            c2, nm, m2 = carry[p]
            sl = slice(p * PAIR, (p + 1) * PAIR)
            q2 = qs_ref[pl.ds(start, L), sl]
            k2 = ks_ref[pl.ds(start, L), sl]
            v_one = jnp.concatenate([v_ref[pl.ds(start, L), sl], ones], axis=1)
            b2 = rep[:, EXP_B + p * PAIR:EXP_B + (p + 1) * PAIR]
            a2 = rep[:, EXP_A + p * PAIR:EXP_A + (p + 1) * PAIR]
            pm = [rep[:, (2 * p + hh) * LANES:(2 * p + hh + 1) * LANES] for hh in range(2)]
            pm2 = jnp.where(lo_lanes, pm[0], pm[1])
            qk = _dot_nt(_stack_heads(q2), k2)
            sc = []
            for hh in range(2):
                a_row = gtc[G_MLA + 2 * p + hh:G_MLA + 2 * p + hh + 1, :]
                sc.append(qk[hh * L:(hh + 1) * L] * jnp.exp(jnp.where(tri, a_row - pm[hh], -jnp.inf)))
            r = _dot(jnp.concatenate(sc, axis=0).astype(BF16), v_one)
            num_loc = jnp.where(lo_lanes, r[:L, :LANES], r[L:, :LANES])
            den_loc = jnp.where(lo_lanes, r[:L, LANES:], r[L:, LANES:])
            m_loc = b2 + pm2
            inter = _dot(q2, jnp.concatenate([c2, nm], axis=1).astype(BF16))
            it2 = b2 + m2
            m_t = jnp.maximum(it2, m_loc)
            f_loc = jnp.exp(m_loc - m_t)
            w_int = jnp.exp(it2 - m_t)
            num = num_loc * f_loc + w_int * inter[:, :LANES]
            den = den_loc * f_loc + w_int * inter[:, LANES:]
            hp = num / jnp.maximum(jnp.abs(den), jnp.exp(-m_t))
            sq = hp * hp
            sq_hi = sq.astype(BF16)
            sq_lo = (sq - sq_hi.astype(F32)).astype(BF16)
            msq = (_dot(sq_hi, bd_ones) + _dot(sq_lo, bd_ones)) * (1.0 / HEAD_DIM)
            hp = hp * lax.rsqrt(msq + RMS_EPS) * nw_ref[:, sl]
            gate = jax.nn.sigmoid(og_ref[pl.ds(start, L), sl].astype(F32))
            o_ref[pl.ds(start, L), sl] = (gate * hp).astype(BF16)
            g2 = b2[L - 1:L, :]
            pl2 = pm2[L - 1:L, :]
            kw = (k2.astype(F32) * jnp.exp(a2 - pl2)).astype(BF16)
            upd = _dot_tn(kw, v_one)
            m_new = jnp.maximum(g2 + m2, g2 + pl2)
            decay = jnp.exp(g2 + m2 - m_new)
            scale = jnp.exp(g2 + pl2 - m_new)
            c2 = decay * c2 + scale * jnp.where(blockdiag, upd[:, :LANES], 0.0)
            nm = decay * nm + scale * jnp.where(blockdiag, upd[:, LANES:], 0.0)
            new_carry.append((c2, nm, m_new))
        return tuple(new_carry)

    init = tuple((jnp.zeros((LANES, LANES), F32), jnp.zeros((LANES, LANES), F32), jnp.zeros((1, LANES), F32))
                 for _ in range(ML_HEADS // 2))
    lax.fori_loop(0, nc, chunk, init)


def _mlstm(u, g, gt, conv_w, norm_w, bsz, s):
    blk = lambda j: pl.BlockSpec((s, D_ML), lambda b: (b, j))
    first = (3 * D_FOX + 3 * D_SB) // D_ML
    return pl.pallas_call(
        functools.partial(_mlstm_kernel, s=s),
        grid=(bsz,),
        in_specs=[
            blk(first), blk(first + 1), blk(first + 2), blk(first + 3),
            pl.BlockSpec((s, N_GATE), lambda b: (b, 0)),
            pl.BlockSpec((None, GT_ROWS, s), lambda b: (b, 0, 0)),
            pl.BlockSpec((CONV_WIDTH, 2 * D_ML), lambda b: (0, 0)),
            pl.BlockSpec((1, D_ML), lambda b: (0, 0)),
        ],
        out_specs=pl.BlockSpec((s, D_ML), lambda b: (b, 0)),
        out_shape=jax.ShapeDtypeStruct((bsz * s, D_ML), BF16),
        scratch_shapes=[pltpu.VMEM((s, D_ML), BF16), pltpu.VMEM((s, D_ML), BF16), pltpu.VMEM((LANES, N_EXP), BF16)],
        compiler_params=_cparams("parallel"),
        name="mlstm",
    )(u, u, u, u, g, gt, conv_w, norm_w)


def _memkv_kernel(x_ref, nw_ref, w_ref, o_ref):
    xn = _rms(x_ref[...], nw_ref[...]).astype(BF16)
    o_ref[...] = _dot(xn, w_ref[...]).astype(BF16)


def _memkv(mem2d, nw, w, layer, tm=512):
    t = mem2d.shape[0]
    return pl.pallas_call(
        _memkv_kernel,
        grid=(t // tm,),
        in_specs=[
            pl.BlockSpec((tm, D_MODEL), lambda i: (i, 0)),
            pl.BlockSpec((1, D_MODEL), lambda i: (0, 0)),
            pl.BlockSpec((None, D_MODEL, 2 * D_X), lambda i: (layer, 0, 0)),
        ],
        out_specs=pl.BlockSpec((tm, 2 * D_X), lambda i: (i, 0)),
        out_shape=jax.ShapeDtypeStruct((t, 2 * D_X), BF16),
        compiler_params=_cparams("parallel"),
        name="memkv",
    )(mem2d, nw, w)


def _cross_kernel(yf_ref, ys_ref, ym_ref, wo_ref, h_ref, nw_ref, wq_ref, kv_ref, wxo_ref, o_ref):
    hf = (h_ref[...] + _dot(yf_ref[...], wo_ref[:D_FOX, :]) + _dot(ys_ref[...], wo_ref[D_FOX:D_FOX + D_SB, :])
          + _dot(ym_ref[...], wo_ref[D_FOX + D_SB:, :]))
    hn = _rms(hf, nw_ref[...]).astype(BF16)
    q = _dot(hn, wq_ref[...]).astype(BF16)
    outs = []
    for hd in range(X_HEADS):
        sl = slice(hd * X_HEAD_DIM, (hd + 1) * X_HEAD_DIM)
        k = kv_ref[:, sl]
        v = kv_ref[:, D_X + hd * X_HEAD_DIM:D_X + (hd + 1) * X_HEAD_DIM]
        s = _dot_nt(q[:, sl], k) * (X_HEAD_DIM ** -0.5)
        m = jnp.max(s, axis=1, keepdims=True)
        e = jnp.exp(s - m)
        pr = e / jnp.sum(e, axis=1, keepdims=True)
        outs.append(_dot(pr.astype(BF16), v).astype(BF16))
    o = jnp.concatenate(outs, axis=1)
    o_ref[...] = hf + _dot(o, wxo_ref[...])


def _cross(yf, ys, ym, wo, h, nw, wq, kv, wxo, layer, bsz, s, mlen, tm=512):
    nt = s // tm
    row = lambda n: pl.BlockSpec((tm, n), lambda b, i: (b * nt + i, 0))
    wgt = lambda r, c: pl.BlockSpec((None, r, c), lambda b, i: (layer, 0, 0))
    return pl.pallas_call(
        _cross_kernel,
        grid=(bsz, nt),
        in_specs=[
            row(D_FOX), row(D_SB), row(D_ML), wgt(D_MODEL, D_MODEL), row(D_MODEL),
            pl.BlockSpec((1, D_MODEL), lambda b, i: (0, 0)),
            wgt(D_MODEL, D_X),
            pl.BlockSpec((mlen, 2 * D_X), lambda b, i: (b, 0)),
            wgt(D_X, D_MODEL),
        ],
        out_specs=row(D_MODEL),
        out_shape=jax.ShapeDtypeStruct((bsz * s, D_MODEL), F32),
        compiler_params=_cparams("parallel", "parallel"),
        name="cross",
    )(yf, ys, ym, wo, h, nw, wq, kv, wxo)


def _ffn_kernel(h_ref, nw_ref, wg_ref, wu_ref, wd_ref, fw_ref, o_ref, acc_ref, *, f_chunk, final_norm):
    hf = h_ref[...]
    hn = _rms(hf, nw_ref[...]).astype(BF16)
    for c in range(D_FF // f_chunk):
        sl = slice(c * f_chunk, (c + 1) * f_chunk)
        g = _dot(hn, wg_ref[:, sl])
        u = _dot(hn, wu_ref[:, sl])
        a = (g * jax.nn.sigmoid(g) * u).astype(BF16)
        part = _dot(a, wd_ref[sl, :])
        if c == 0:
            acc_ref[...] = part
        else:
            acc_ref[...] += part
    out = hf + acc_ref[...]
    if final_norm:
        out = _rms(out, fw_ref[...])
    o_ref[...] = out


def _ffn(h, nw, wg, wu, wd, fw, layer, final_norm, tm=512):
    t = h.shape[0]
    const = lambda shape: pl.BlockSpec(shape, lambda i: (0, 0))
    wgt = lambda r, c: pl.BlockSpec((None, r, c), lambda i: (layer, 0, 0))
    return pl.pallas_call(
        functools.partial(_ffn_kernel, f_chunk=256, final_norm=final_norm),
        grid=(t // tm,),
        in_specs=[
            pl.BlockSpec((tm, D_MODEL), lambda i: (i, 0)),
            const((1, D_MODEL)),
            wgt(D_MODEL, D_FF), wgt(D_MODEL, D_FF), wgt(D_FF, D_MODEL),
            const((1, D_MODEL)),
        ],
        out_specs=pl.BlockSpec((tm, D_MODEL), lambda i: (i, 0)),
        out_shape=jax.ShapeDtypeStruct((t, D_MODEL), F32),
        scratch_shapes=[pltpu.VMEM((tm, D_MODEL), F32)],
        compiler_params=_cparams("parallel"),
        name="ffn",
    )(h, nw, wg, wu, wd, fw)


N_FOX_QKV = 3 * D_FOX
N_REST = 3 * D_SB + 4 * D_ML


N_USED_GATES = FOX_HEADS + 2 * ML_HEADS


def _regroup_kernel(w_ref, o_ref):
    o_ref[:, :N_FOX_QKV] = w_ref[:, :N_FOX_QKV].astype(BF16)
    rest = w_ref[:, N_FOX_QKV + FOX_HEADS:N_FOX_QKV + FOX_HEADS + N_REST]
    o_ref[:, N_FOX_QKV:D_MAIN] = rest.astype(BF16)
    lane = lax.broadcasted_iota(jnp.int32, (w_ref.shape[0], N_USED_GATES), 1)
    gates = jnp.where(lane < FOX_HEADS, w_ref[:, N_FOX_QKV:N_FOX_QKV + N_USED_GATES], w_ref[:, D_MAIN:D_MAIN + N_USED_GATES])
    o_ref[:, D_MAIN:] = jnp.zeros((w_ref.shape[0], N_GATE), BF16)
    o_ref[:, D_MAIN:D_MAIN + N_USED_GATES] = gates.astype(BF16)


def _regroup_w_in(w, tr=256):
    depth, rows, n_in = w.shape
    assert N_FOX_QKV + N_REST == D_MAIN and D_MAIN % LANES == 0 and n_in == D_MAIN + N_USED_GATES
    return pl.pallas_call(
        _regroup_kernel,
        grid=(depth, rows // tr),
        in_specs=[pl.BlockSpec((None, tr, n_in), lambda l, i: (l, i, 0))],
        out_specs=pl.BlockSpec((None, tr, D_MAIN + N_GATE), lambda l, i: (l, i, 0)),
        out_shape=jax.ShapeDtypeStruct((depth, rows, D_MAIN + N_GATE), BF16),
        compiler_params=_cparams("parallel", "parallel"),
        name="regroup",
    )(w)


def kernel(x, mem, norm_mix_w, w_in, fox_f_b, ml_conv_w, ml_i_b, ml_f_b, ml_norm_w, w_out, norm_x_w, mem_norm_w,
           wx_q, wx_kv, wx_o, norm_ffn_w, w_gate, w_up, w_down, final_norm_w):
    bsz, s, _ = x.shape
    mlen = mem.shape[1]
    depth = w_in.shape[0]
    h = x.reshape(bsz * s, D_MODEL)
    mem2d = mem.reshape(bsz * mlen, D_MODEL)
    row = lambda v: v.reshape(1, -1)
    w_all = _regroup_w_in(w_in)
    bias = jnp.concatenate([fox_f_b, ml_i_b, ml_f_b, jnp.zeros((depth, N_GATE - 3 * FOX_HEADS), F32)], axis=1)
    wo, wq, wkv, wxo = (w.astype(BF16) for w in (w_out, wx_q, wx_kv, wx_o))
    wg, wu, wd = (w.astype(BF16) for w in (w_gate, w_up, w_down))
    for l in range(depth):
        u, g_raw = _inproj(h, row(norm_mix_w[l]), w_all, l)
        g, gt = _gates(g_raw, row(bias[l]), bsz, s)
        y_fox = _fox(u, g, gt, bsz, s)
        y_sb = _sb(u, bsz, s)
        y_ml = _mlstm(u, g, gt, ml_conv_w[l], row(ml_norm_w[l]), bsz, s)
        kv = _memkv(mem2d, row(mem_norm_w[l]), wkv, l, tm=min(512, bsz * mlen))
        h = _cross(y_fox, y_sb, y_ml, wo, h, row(norm_x_w[l]), wq, kv, wxo, l, bsz, s, mlen)
        h = _ffn(h, row(norm_ffn_w[l]), wg, wu, wd, row(final_norm_w), l, final_norm=(l == depth - 1))
    return h.reshape(bsz, s, D_MODEL)
```

```python
import functools

import jax
import jax.numpy as jnp
from jax import lax
from jax.experimental import pallas as pl
from jax.experimental.pallas import tpu as pltpu

D_MODEL = 1024
HEAD_DIM = 64
FOX_HEADS = 6
SB_HEADS = 4
ML_HEADS = 6
D_FOX = FOX_HEADS * HEAD_DIM
D_SB = SB_HEADS * HEAD_DIM
D_ML = ML_HEADS * HEAD_DIM
ML_CHUNK = 128
CONV_WIDTH = 4
X_HEADS = 4
X_HEAD_DIM = 128
D_X = X_HEADS * X_HEAD_DIM
D_FF = 2816
RMS_EPS = 1e-6

LANES = 128
PAIR = 2 * HEAD_DIM
D_MAIN = 3 * D_FOX + 3 * D_SB + 4 * D_ML
N_GATE = LANES
GT_ROWS = 32
NEG = -1e30
SB_SUB = 128
SB_EXIT = 110.0
LOG2E = 1.4426950408889634
VMEM_LIMIT = 56 * 1024 * 1024

BF16 = jnp.bfloat16
F32 = jnp.float32

FOX_Q, FOX_K, FOX_V = 0, 3, 6
SB_Q, SB_K, SB_V = 9, 11, 13
G_FOX, G_MLA, G_MLB, G_MLP = 0, 6, 12, 18
EXP_B = ML_HEADS * LANES
EXP_A = EXP_B + D_ML
N_EXP = EXP_A + D_ML


def _cparams(*sem):
    return pltpu.CompilerParams(dimension_semantics=sem, vmem_limit_bytes=VMEM_LIMIT)


def _rms(xf, w):
    ms = jnp.mean(xf * xf, axis=-1, keepdims=True)
    return xf * lax.rsqrt(ms + RMS_EPS) * w


def _dot(a, b):
    return jnp.dot(a, b, preferred_element_type=F32)


def _dot_nt(a, b):
    return lax.dot_general(a, b, (((1,), (1,)), ((), ())), preferred_element_type=F32)


def _dot_tn(a, b):
    return lax.dot_general(a, b, (((0,), (0,)), ((), ())), preferred_element_type=F32)


def _log_sigmoid(x):
    return jnp.minimum(x, 0.0) - jnp.log(1.0 + jnp.exp(-jnp.abs(x)))


def _split3(x):
    x1 = x.astype(BF16)
    r1 = x - x1.astype(F32)
    x2 = r1.astype(BF16)
    return x1, x2, (r1 - x2.astype(F32)).astype(BF16)


def _dot_exact(a, b, lhs_exact):
    if lhs_exact:
        x1, x2, x3 = _split3(b)
        return _dot(a, x1) + _dot(a, x2) + _dot(a, x3)
    x1, x2, x3 = _split3(a)
    return _dot(x1, b) + _dot(x2, b) + _dot(x3, b)


def _inproj_kernel(x_ref, nw_ref, w_ref, u_ref, g_ref, *, n_chunk):
    xn = _rms(x_ref[...], nw_ref[...]).astype(BF16)
    n_all = D_MAIN + N_GATE
    for c in range(n_all // n_chunk):
        lo = c * n_chunk
        res = _dot(xn, w_ref[:, lo:lo + n_chunk])
        if lo + n_chunk <= D_MAIN:
            u_ref[:, lo:lo + n_chunk] = res.astype(BF16)
        else:
            u_ref[:, lo:D_MAIN] = res[:, :D_MAIN - lo].astype(BF16)
            g_ref[...] = res[:, D_MAIN - lo:]


def _inproj(h, nw, w_all, layer, tm=512):
    t = h.shape[0]
    return pl.pallas_call(
        functools.partial(_inproj_kernel, n_chunk=512),
        grid=(t // tm,),
        in_specs=[
            pl.BlockSpec((tm, D_MODEL), lambda i: (i, 0)),
            pl.BlockSpec((1, D_MODEL), lambda i: (0, 0)),
            pl.BlockSpec((None, D_MODEL, D_MAIN + N_GATE), lambda i: (layer, 0, 0)),
        ],
        out_specs=[
            pl.BlockSpec((tm, D_MAIN), lambda i: (i, 0)),
            pl.BlockSpec((tm, N_GATE), lambda i: (i, 0)),
        ],
        out_shape=[jax.ShapeDtypeStruct((t, D_MAIN), BF16), jax.ShapeDtypeStruct((t, N_GATE), F32)],
        compiler_params=_cparams("parallel"),
        name="inproj",
    )(h, nw, w_all)


def _gates_kernel(g_ref, b_ref, go_ref, gt_ref, *, s):
    L = ML_CHUNK
    col = lax.broadcasted_iota(jnp.int32, (L, LANES), 1)
    row = lax.broadcasted_iota(jnp.int32, (L, LANES), 0)
    is_a = (col >= G_MLA) & (col < G_MLB)
    r = lax.broadcasted_iota(jnp.int32, (L, L), 0)
    c = lax.broadcasted_iota(jnp.int32, (L, L), 1)
    tri = jnp.where(r >= c, 1.0, 0.0).astype(BF16)
    bias = b_ref[...]

    def chunk(ci, carry):
        start = pl.multiple_of(ci * L, L)
        pre = g_ref[pl.ds(start, L), :] + bias
        cum = _dot_exact(tri, jnp.where(is_a, 0.0, _log_sigmoid(pre)), lhs_exact=True)
        tot = cum + carry
        a = pre - pltpu.roll(cum, LANES - (G_MLB - G_MLA), 1)
        pm = jnp.where(is_a, a, -jnp.inf)
        k = 1
        while k < L:
            pm = jnp.maximum(pm, jnp.where(row >= k, pltpu.roll(pm, k, 0), -jnp.inf))
            k *= 2
        pm = pltpu.roll(pm, G_MLP - G_MLA, 1)
        out = jnp.where(col < G_MLA, tot,
                        jnp.where(is_a, a, jnp.where(col < G_MLP, cum, jnp.where(col < G_MLP + ML_HEADS, pm, 0.0))))
        go_ref[pl.ds(start, L), :] = out
        gt_ref[:, pl.ds(start, L)] = out.T[:GT_ROWS, :]
        return tot[L - 1:L, :]

    lax.fori_loop(0, s // L, chunk, jnp.zeros((1, LANES), F32), unroll=4)


def _gates(g, bias, bsz, s):
    return pl.pallas_call(
        functools.partial(_gates_kernel, s=s),
        grid=(bsz,),
        in_specs=[
            pl.BlockSpec((s, N_GATE), lambda b: (b, 0)),
            pl.BlockSpec((1, N_GATE), lambda b: (0, 0)),
        ],
        out_specs=[
            pl.BlockSpec((s, N_GATE), lambda b: (b, 0)),
            pl.BlockSpec((None, GT_ROWS, s), lambda b: (b, 0, 0)),
        ],
        out_shape=[jax.ShapeDtypeStruct((bsz * s, N_GATE), F32), jax.ShapeDtypeStruct((bsz, GT_ROWS, s), F32)],
        compiler_params=_cparams("parallel"),
        name="gates",
    )(g, bias)


def _head_masks(q2):
    lane = lax.broadcasted_iota(jnp.int32, q2.shape, 1)
    zero = jnp.zeros_like(q2)
    return jnp.where(lane < HEAD_DIM, q2, zero), jnp.where(lane >= HEAD_DIM, q2, zero)


def _stack_heads(q2):
    return jnp.concatenate(_head_masks(q2), axis=0)


def _unstack_heads(x, rows):
    lane = lax.broadcasted_iota(jnp.int32, (rows, LANES), 1)
    return jnp.where(lane < HEAD_DIM, x[:rows], x[rows:])


def _fox_kernel(q_ref, k_ref, v_ref, g_ref, gt_ref, o_ref, ct_ref, m_ref, acc_ref, *, tq):
    p = pl.program_id(1)
    qi = pl.program_id(2)
    qs = _stack_heads(q_ref[...] * jnp.asarray(HEAD_DIM ** -0.5, BF16))
    gq = g_ref[...]
    lane = lax.broadcasted_iota(jnp.int32, (tq, LANES), 1)
    for hh in range(2):
        ct = jnp.sum(jnp.where(lane == G_FOX + 2 * p + hh, gq, 0.0), axis=1, keepdims=True)
        ct_ref[hh * tq:(hh + 1) * tq, :] = jnp.broadcast_to(ct * LOG2E, (tq, LANES))
    m_ref[...] = jnp.full_like(m_ref, NEG)
    acc_ref[...] = jnp.zeros_like(acc_ref)

    def group(start, nk, row0, masked):
        nr = tq - row0
        qsub = qs if row0 == 0 else jnp.concatenate([qs[row0:tq], qs[tq + row0:]], axis=0)
        s = _dot_nt(qsub, k_ref[pl.ds(start, nk), :])
        v2 = v_ref[pl.ds(start, nk), :]
        one = jnp.ones_like(v2)
        lo_lanes = lax.broadcasted_iota(jnp.int32, (nk, LANES), 1) < HEAD_DIM
        v_ones = (jnp.where(lo_lanes, v2, one), jnp.where(lo_lanes, one, v2))
        if masked:
            rq = lax.broadcasted_iota(jnp.int32, (nr, nk), 0)
            rk = lax.broadcasted_iota(jnp.int32, (nr, nk), 1)
            causal = jnp.where(rk <= rq, 0.0, NEG)
        for hh in range(2):
            rows = slice(hh * tq + row0, (hh + 1) * tq)
            cs = gt_ref[pl.ds(G_FOX + 2 * p + hh, 1), pl.ds(start, nk)] * LOG2E
            r = s[hh * nr:(hh + 1) * nr] * LOG2E - cs
            if masked:
                r = r + causal
            ct = ct_ref[rows, :]
            m_prev = m_ref[rows, :]
            m_next = jnp.maximum(m_prev, jnp.max(r, axis=1, keepdims=True) + ct)
            pr = jnp.exp2(r - jnp.tile(m_next - ct, (1, nk // LANES)))
            alpha = jnp.exp2(m_prev - m_next)
            acc_ref[rows, :] = alpha * acc_ref[rows, :] + _dot(pr.astype(BF16), v_ones[hh])
            m_ref[rows, :] = m_next

    def body(j, carry):
        group(pl.multiple_of(j * tq, tq), tq, 0, False)
        return carry

    lax.fori_loop(0, qi, body, 0)
    half = tq // 2
    group(pl.multiple_of(qi * tq, tq), half, 0, True)
    group(pl.multiple_of(qi * tq + half, half), half, half, True)
    acc = acc_ref[...]
    o_ref[...] = _unstack_heads(acc / pltpu.roll(acc, HEAD_DIM, 1), tq).astype(BF16)


def _fox(u, g, gt, bsz, s, tq=512):
    nq = s // tq
    stat = pltpu.VMEM((2 * tq, LANES), F32)
    return pl.pallas_call(
        functools.partial(_fox_kernel, tq=tq),
        grid=(bsz, FOX_HEADS // 2, nq),
        in_specs=[
            pl.BlockSpec((tq, PAIR), lambda b, p, i: (b * nq + i, FOX_Q + p)),
            pl.BlockSpec((s, PAIR), lambda b, p, i: (b, FOX_K + p)),
            pl.BlockSpec((s, PAIR), lambda b, p, i: (b, FOX_V + p)),
            pl.BlockSpec((tq, N_GATE), lambda b, p, i: (b * nq + i, 0)),
            pl.BlockSpec((None, 8, s), lambda b, p, i: (b, 0, 0)),
        ],
        out_specs=pl.BlockSpec((tq, PAIR), lambda b, p, i: (b * nq + i, p)),
        out_shape=jax.ShapeDtypeStruct((bsz * s, D_FOX), BF16),
        scratch_shapes=[stat, stat, stat],
        compiler_params=_cparams("parallel", "parallel", "arbitrary"),
        name="fox",
    )(u, u, u, g, gt)


def _sb_kernel(q_ref, k_ref, v_ref, o_ref, qs_ref, run_ref, acc_ref, *, tq, tg):
    qi = pl.program_id(2)
    qs_ref[...] = _stack_heads(q_ref[...] * jnp.asarray(HEAD_DIM ** -0.5, BF16))
    r = lax.broadcasted_iota(jnp.int32, (2 * SB_SUB, 2 * SB_SUB), 0)
    c = lax.broadcasted_iota(jnp.int32, (2 * SB_SUB, 2 * SB_SUB), 1)
    suffix = jnp.where((c >= SB_SUB) | (jnp.where(r >= SB_SUB, r - SB_SUB, r) >= c), 1.0, 0.0).astype(BF16)
    run_ref[...] = jnp.zeros_like(run_ref)
    acc_ref[...] = jnp.zeros_like(acc_ref)
    n_diag = tq // tg

    def group(g, diag):
        row0 = tq - (g + 1) * tg if diag else 0
        nr = tq - row0
        heads = lambda ref: ref[...] if row0 == 0 else jnp.concatenate([ref[row0:tq, :], ref[tq + row0:, :]], axis=0)
        start = pl.multiple_of((qi + 1) * tq - (g + 1) * tg, tg)
        z = _dot_nt(heads(qs_ref), k_ref[pl.ds(start, tg), :])
        sp = jnp.maximum(z, 0.0) + jnp.log(1.0 + jnp.exp(-jnp.abs(z)))
        if diag:
            rq = lax.broadcasted_iota(jnp.int32, (2 * nr, tg), 0)
            rk = lax.broadcasted_iota(jnp.int32, (2 * nr, tg), 1)
            valid = rk < jnp.where(rq >= nr, rq - nr, rq)
            sp = jnp.where(valid, sp, 0.0)
        hi = sp.astype(BF16)
        lo = (sp - hi.astype(F32)).astype(BF16)
        offs = heads(run_ref)
        parts = [None] * (tg // SB_SUB)
        for sb in reversed(range(tg // SB_SUB)):
            sl = slice(sb * SB_SUB, (sb + 1) * SB_SUB)
            rs = _dot(jnp.concatenate([hi[:, sl], lo[:, sl]], axis=1), suffix)
            a = jnp.exp(z[:, sl] - rs[:, :SB_SUB] - offs)
            if diag:
                a = jnp.where(valid[:, sl], a, 0.0)
            parts[sb] = a.astype(BF16)
            offs = offs + rs[:, SB_SUB:]
        pv = _dot(jnp.concatenate(parts, axis=1), v_ref[pl.ds(start, tg), :])
        for hh in range(2):
            rows = slice(hh * tq + row0, (hh + 1) * tq)
            run_ref[rows, :] = offs[hh * nr:(hh + 1) * nr]
            acc_ref[rows, :] += pv[hh * nr:(hh + 1) * nr]
        return jnp.min(offs)

    for g in range(n_diag):
        low = group(g, True)

    def cond(carry):
        g, low = carry
        return (g < (qi + 1) * n_diag) & (low < SB_EXIT)

    def body(carry):
        g, _ = carry
        return g + 1, group(g, False)

    lax.while_loop(cond, body, (jnp.int32(n_diag), low))
    o_ref[...] = _unstack_heads(acc_ref[...], tq).astype(BF16)


def _sb(u, bsz, s, tq=512, tg=256):
    nq = s // tq
    stat = pltpu.VMEM((2 * tq, LANES), F32)
    return pl.pallas_call(
        functools.partial(_sb_kernel, tq=tq, tg=tg),
        grid=(bsz, SB_HEADS // 2, nq),
        in_specs=[
            pl.BlockSpec((tq, PAIR), lambda b, p, i: (b * nq + i, SB_Q + p)),
            pl.BlockSpec((s, PAIR), lambda b, p, i: (b, SB_K + p)),
            pl.BlockSpec((s, PAIR), lambda b, p, i: (b, SB_V + p)),
        ],
        out_specs=pl.BlockSpec((tq, PAIR), lambda b, p, i: (b * nq + i, p)),
        out_shape=jax.ShapeDtypeStruct((bsz * s, D_SB), BF16),
        scratch_shapes=[pltpu.VMEM((2 * tq, PAIR), BF16), stat, stat],
        compiler_params=_cparams("parallel", "parallel", "arbitrary"),
        name="sb",
    )(u, u, u)


def _mlstm_kernel(xq_ref, xk_ref, v_ref, og_ref, g_ref, gt_ref, cw_ref, nw_ref, o_ref, qs_ref, ks_ref, e_ref, *, s):
    L = ML_CHUNK
    nc = s // L
    TAIL = 16
    row = lax.broadcasted_iota(jnp.int32, (TAIL, D_ML), 0)

    def conv_chunk(c, _):
        start = pl.multiple_of(c * L, L)
        prev_start = pl.multiple_of(jnp.maximum(c - 1, 0) * L, L)
        for x_ref, dst_ref, w0, scale in ((xq_ref, qs_ref, 0, 1.0), (xk_ref, ks_ref, D_ML, HEAD_DIM ** -0.5)):
            cur = x_ref[pl.ds(start, L), :].astype(F32)
            tail = x_ref[pl.ds(prev_start + L - TAIL, TAIL), :].astype(F32)
            tail = jnp.where(c > 0, tail, 0.0)
            w = cw_ref[:, w0:w0 + D_ML]
            y = cur * w[CONV_WIDTH - 1:CONV_WIDTH, :]
            for k in range(1, CONV_WIDTH):
                sh = pltpu.roll(cur, k, 0)
                head = jnp.where(row < k, pltpu.roll(tail, k, 0), sh[:TAIL])
                sh = jnp.concatenate([head, sh[TAIL:]], axis=0)
                y = y + sh * w[CONV_WIDTH - 1 - k:CONV_WIDTH - k, :]
            y = y * jax.nn.sigmoid(y) * scale
            dst_ref[pl.ds(start, L), :] = y.astype(BF16)
        return 0

    lax.fori_loop(0, nc, conv_chunk, 0)

    ci = lax.broadcasted_iota(jnp.int32, (LANES, N_EXP), 0)
    cj = lax.broadcasted_iota(jnp.int32, (LANES, N_EXP), 1)
    src_col = jnp.where(cj < EXP_B, G_MLP + lax.shift_right_logical(cj, 7),
                        jnp.where(cj < EXP_A, G_MLB + lax.shift_right_logical(cj - EXP_B, 6),
                                  G_MLA + lax.shift_right_logical(cj - EXP_A, 6)))
    e_ref[...] = jnp.where(ci == src_col, 1.0, 0.0).astype(BF16)

    lane = lax.broadcasted_iota(jnp.int32, (L, LANES), 1)
    lo_lanes = lane < HEAD_DIM
    r_ll = lax.broadcasted_iota(jnp.int32, (L, L), 0)
    c_ll = lax.broadcasted_iota(jnp.int32, (L, L), 1)
    tri = r_ll >= c_ll
    blockdiag = (r_ll < HEAD_DIM) == (c_ll < HEAD_DIM)
    bd_ones = jnp.where(blockdiag, 1.0, 0.0).astype(BF16)
    ones = jnp.ones((L, LANES), BF16)

    def chunk(c, carry):
        start = pl.multiple_of(c * L, L)
        rep = _dot_exact(g_ref[pl.ds(start, L), :], e_ref[...], lhs_exact=False)
        gtc = gt_ref[:, pl.ds(start, L)]
        new_carry = []
        for p in range(ML_HEADS // 2):
            c2, nm, m2 = carry[p]
            sl = slice(p * PAIR, (p + 1) * PAIR)
            q2 = qs_ref[pl.ds(start, L), sl]
            k2 = ks_ref[pl.ds(start, L), sl]
            v_one = jnp.concatenate([v_ref[pl.ds(start, L), sl], ones], axis=1)
            b2 = rep[:, EXP_B + p * PAIR:EXP_B + (p + 1) * PAIR]
            a2 = rep[:, EXP_A + p * PAIR:EXP_A + (p + 1) * PAIR]
            pm = [rep[:, (2 * p + hh) * LANES:(2 * p + hh + 1) * LANES] for hh in range(2)]
            pm2 = jnp.where(lo_lanes, pm[0], pm[1])
            qk = _dot_nt(_stack_heads(q2), k2)
            sc = []
            for hh in range(2):
                a_row = gtc[G_MLA + 2 * p + hh:G_MLA + 2 * p + hh + 1, :]
                sc.append(qk[hh * L:(hh + 1) * L] * jnp.exp(jnp.where(tri, a_row - pm[hh], -jnp.inf)))
            r = _dot(jnp.concatenate(sc, axis=0).astype(BF16), v_one)
            num_loc = jnp.where(lo_lanes, r[:L, :LANES], r[L:, :LANES])
            den_loc = jnp.where(lo_lanes, r[:L, LANES:], r[L:, LANES:])
            m_loc = b2 + pm2
            inter = _dot(q2, jnp.concatenate([c2, nm], axis=1).astype(BF16))
            it2 = b2 + m2
            m_t = jnp.maximum(it2, m_loc)
            f_loc = jnp.exp(m_loc - m_t)
            w_int = jnp.exp(it2 - m_t)
            num = num_loc * f_loc + w_int * inter[:, :LANES]
            den = den_loc * f_loc + w_int * inter[:, LANES:]
            hp = num / jnp.maximum(jnp.abs(den), jnp.exp(-m_t))
            sq = hp * hp
            sq_hi = sq.astype(BF16)
            sq_lo = (sq - sq_hi.astype(F32)).astype(BF16)
            msq = (_dot(sq_hi, bd_ones) + _dot(sq_lo, bd_ones)) * (1.0 / HEAD_DIM)
            hp = hp * lax.rsqrt(msq + RMS_EPS) * nw_ref[:, sl]
            gate = jax.nn.sigmoid(og_ref[pl.ds(start, L), sl].astype(F32))
            o_ref[pl.ds(start, L), sl] = (gate * hp).astype(BF16)
            g2 = b2[L - 1:L, :]
            pl2 = pm2[L - 1:L, :]
            kw = (k2.astype(F32) * jnp.exp(a2 - pl2)).astype(BF16)
            upd = _dot_tn(kw, v_one)
            m_new = jnp.maximum(g2 + m2, g2 + pl2)
            decay = jnp.exp(g2 + m2 - m_new)
            scale = jnp.exp(g2 + pl2 - m_new)
            c2 = decay * c2 + scale * jnp.where(blockdiag, upd[:, :LANES], 0.0)
            nm = decay * nm + scale * jnp.where(blockdiag, upd[:, LANES:], 0.0)
            new_carry.append((c2, nm, m_new))
        return tuple(new_carry)

    init = tuple((jnp.zeros((LANES, LANES), F32), jnp.zeros((LANES, LANES), F32), jnp.zeros((1, LANES), F32))
                 for _ in range(ML_HEADS // 2))
    lax.fori_loop(0, nc, chunk, init)


def _mlstm(u, g, gt, conv_w, norm_w, bsz, s):
    blk = lambda j: pl.BlockSpec((s, D_ML), lambda b: (b, j))
    first = (3 * D_FOX + 3 * D_SB) // D_ML
    return pl.pallas_call(
        functools.partial(_mlstm_kernel, s=s),
        grid=(bsz,),
        in_specs=[
            blk(first), blk(first + 1), blk(first + 2), blk(first + 3),
            pl.BlockSpec((s, N_GATE), lambda b: (b, 0)),
            pl.BlockSpec((None, GT_ROWS, s), lambda b: (b, 0, 0)),
            pl.BlockSpec((CONV_WIDTH, 2 * D_ML), lambda b: (0, 0)),
            pl.BlockSpec((1, D_ML), lambda b: (0, 0)),
        ],
        out_specs=pl.BlockSpec((s, D_ML), lambda b: (b, 0)),
        out_shape=jax.ShapeDtypeStruct((bsz * s, D_ML), BF16),
        scratch_shapes=[pltpu.VMEM((s, D_ML), BF16), pltpu.VMEM((s, D_ML), BF16), pltpu.VMEM((LANES, N_EXP), BF16)],
        compiler_params=_cparams("parallel"),
        name="mlstm",
    )(u, u, u, u, g, gt, conv_w, norm_w)


def _memkv_kernel(x_ref, nw_ref, w_ref, o_ref):
    xn = _rms(x_ref[...], nw_ref[...]).astype(BF16)
    o_ref[...] = _dot(xn, w_ref[...]).astype(BF16)


def _memkv(mem2d, nw, w, layer, tm=512):
    t = mem2d.shape[0]
    return pl.pallas_call(
        _memkv_kernel,
        grid=(t // tm,),
        in_specs=[
            pl.BlockSpec((tm, D_MODEL), lambda i: (i, 0)),
            pl.BlockSpec((1, D_MODEL), lambda i: (0, 0)),
            pl.BlockSpec((None, D_MODEL, 2 * D_X), lambda i: (layer, 0, 0)),
        ],
        out_specs=pl.BlockSpec((tm, 2 * D_X), lambda i: (i, 0)),
        out_shape=jax.ShapeDtypeStruct((t, 2 * D_X), BF16),
        compiler_params=_cparams("parallel"),
        name="memkv",
    )(mem2d, nw, w)


def _cross_kernel(yf_ref, ys_ref, ym_ref, wo_ref, h_ref, nw_ref, wq_ref, kv_ref, wxo_ref, o_ref):
    hf = (h_ref[...] + _dot(yf_ref[...], wo_ref[:D_FOX, :]) + _dot(ys_ref[...], wo_ref[D_FOX:D_FOX + D_SB, :])
          + _dot(ym_ref[...], wo_ref[D_FOX + D_SB:, :]))
    hn = _rms(hf, nw_ref[...]).astype(BF16)
    q = _dot(hn, wq_ref[...]).astype(BF16)
    outs = []
    for hd in range(X_HEADS):
        sl = slice(hd * X_HEAD_DIM, (hd + 1) * X_HEAD_DIM)
        k = kv_ref[:, sl]
        v = kv_ref[:, D_X + hd * X_HEAD_DIM:D_X + (hd + 1) * X_HEAD_DIM]
        s = _dot_nt(q[:, sl], k) * (X_HEAD_DIM ** -0.5)
        m = jnp.max(s, axis=1, keepdims=True)
        e = jnp.exp(s - m)
        pr = e / jnp.sum(e, axis=1, keepdims=True)
        outs.append(_dot(pr.astype(BF16), v).astype(BF16))
    o = jnp.concatenate(outs, axis=1)
    o_ref[...] = hf + _dot(o, wxo_ref[...])


def _cross(yf, ys, ym, wo, h, nw, wq, kv, wxo, layer, bsz, s, mlen, tm=512):
    nt = s // tm
    row = lambda n: pl.BlockSpec((tm, n), lambda b, i: (b * nt + i, 0))
    wgt = lambda r, c: pl.BlockSpec((None, r, c), lambda b, i: (layer, 0, 0))
    return pl.pallas_call(
        _cross_kernel,
        grid=(bsz, nt),
        in_specs=[
            row(D_FOX), row(D_SB), row(D_ML), wgt(D_MODEL, D_MODEL), row(D_MODEL),
            pl.BlockSpec((1, D_MODEL), lambda b, i: (0, 0)),
            wgt(D_MODEL, D_X),
            pl.BlockSpec((mlen, 2 * D_X), lambda b, i: (b, 0)),
            wgt(D_X, D_MODEL),
        ],
        out_specs=row(D_MODEL),
        out_shape=jax.ShapeDtypeStruct((bsz * s, D_MODEL), F32),
        compiler_params=_cparams("parallel", "parallel"),
        name="cross",
    )(yf, ys, ym, wo, h, nw, wq, kv, wxo)


def _ffn_kernel(h_ref, nw_ref, wg_ref, wu_ref, wd_ref, fw_ref, o_ref, acc_ref, *, f_chunk, final_norm):
    hf = h_ref[...]
    hn = _rms(hf, nw_ref[...]).astype(BF16)
    for c in range(D_FF // f_chunk):
        sl = slice(c * f_chunk, (c + 1) * f_chunk)
        g = _dot(hn, wg_ref[:, sl])
        u = _dot(hn, wu_ref[:, sl])
        a = (g * jax.nn.sigmoid(g) * u).astype(BF16)
        part = _dot(a, wd_ref[sl, :])
        if c == 0:
            acc_ref[...] = part
        else:
            acc_ref[...] += part
    out = hf + acc_ref[...]
    if final_norm:
        out = _rms(out, fw_ref[...])
    o_ref[...] = out


def _ffn(h, nw, wg, wu, wd, fw, layer, final_norm, tm=512):
    t = h.shape[0]
    const = lambda shape: pl.BlockSpec(shape, lambda i: (0, 0))
    wgt = lambda r, c: pl.BlockSpec((None, r, c), lambda i: (layer, 0, 0))
    return pl.pallas_call(
        functools.partial(_ffn_kernel, f_chunk=256, final_norm=final_norm),
        grid=(t // tm,),
        in_specs=[
            pl.BlockSpec((tm, D_MODEL), lambda i: (i, 0)),
            const((1, D_MODEL)),
            wgt(D_MODEL, D_FF), wgt(D_MODEL, D_FF), wgt(D_FF, D_MODEL),
            const((1, D_MODEL)),
        ],
        out_specs=pl.BlockSpec((tm, D_MODEL), lambda i: (i, 0)),
        out_shape=jax.ShapeDtypeStruct((t, D_MODEL), F32),
        scratch_shapes=[pltpu.VMEM((tm, D_MODEL), F32)],
        compiler_params=_cparams("parallel"),
        name="ffn",
    )(h, nw, wg, wu, wd, fw)


N_FOX_QKV = 3 * D_FOX
N_REST = 3 * D_SB + 4 * D_ML


N_USED_GATES = FOX_HEADS + 2 * ML_HEADS


def _regroup_kernel(w_ref, o_ref):
    o_ref[:, :N_FOX_QKV] = w_ref[:, :N_FOX_QKV].astype(BF16)
    rest = w_ref[:, N_FOX_QKV + FOX_HEADS:N_FOX_QKV + FOX_HEADS + N_REST]
    o_ref[:, N_FOX_QKV:D_MAIN] = rest.astype(BF16)
    lane = lax.broadcasted_iota(jnp.int32, (w_ref.shape[0], N_USED_GATES), 1)
    gates = jnp.where(lane < FOX_HEADS, w_ref[:, N_FOX_QKV:N_FOX_QKV + N_USED_GATES], w_ref[:, D_MAIN:D_MAIN + N_USED_GATES])
    o_ref[:, D_MAIN:] = jnp.zeros((w_ref.shape[0], N_GATE), BF16)
    o_ref[:, D_MAIN:D_MAIN + N_USED_GATES] = gates.astype(BF16)


def _regroup_w_in(w, tr=256):
    depth, rows, n_in = w.shape
    assert N_FOX_QKV + N_REST == D_MAIN and D_MAIN % LANES == 0 and n_in == D_MAIN + N_USED_GATES
    return pl.pallas_call(
        _regroup_kernel,
        grid=(depth, rows // tr),
        in_specs=[pl.BlockSpec((None, tr, n_in), lambda l, i: (l, i, 0))],
        out_specs=pl.BlockSpec((None, tr, D_MAIN + N_GATE), lambda l, i: (l, i, 0)),
        out_shape=jax.ShapeDtypeStruct((depth, rows, D_MAIN + N_GATE), BF16),
        compiler_params=_cparams("parallel", "parallel"),
        name="regroup",
    )(w)


def kernel(x, mem, norm_mix_w, w_in, fox_f_b, ml_conv_w, ml_i_b, ml_f_b, ml_norm_w, w_out, norm_x_w, mem_norm_w,
           wx_q, wx_kv, wx_o, norm_ffn_w, w_gate, w_up, w_down, final_norm_w):
    bsz, s, _ = x.shape
    mlen = mem.shape[1]
    depth = w_in.shape[0]
    h = x.reshape(bsz * s, D_MODEL)
    mem2d = mem.reshape(bsz * mlen, D_MODEL)
    row = lambda v: v.reshape(1, -1)
    w_all = _regroup_w_in(w_in)
    bias = jnp.concatenate([fox_f_b, ml_i_b, ml_f_b, jnp.zeros((depth, N_GATE - 3 * FOX_HEADS), F32)], axis=1)
    wo, wq, wkv, wxo = (w.astype(BF16) for w in (w_out, wx_q, wx_kv, wx_o))
    wg, wu, wd = (w.astype(BF16) for w in (w_gate, w_up, w_down))
    for l in range(depth):
        u, g_raw = _inproj(h, row(norm_mix_w[l]), w_all, l)
        g, gt = _gates(g_raw, row(bias[l]), bsz, s)
        y_fox = _fox(u, g, gt, bsz, s)
        y_sb = _sb(u, bsz, s)
        y_ml = _mlstm(u, g, gt, ml_conv_w[l], row(ml_norm_w[l]), bsz, s)
        kv = _memkv(mem2d, row(mem_norm_w[l]), wkv, l, tm=min(512, bsz * mlen))
        h = _cross(y_fox, y_sb, y_ml, wo, h, row(norm_x_w[l]), wq, kv, wxo, l, bsz, s, mlen)
        h = _ffn(h, row(norm_ffn_w[l]), wg, wu, wd, row(final_norm_w), l, final_norm=(l == depth - 1))
    return h.reshape(bsz, s, D_MODEL)
```

```python
import functools

import jax
import jax.numpy as jnp
from jax import lax
from jax.experimental import pallas as pl
from jax.experimental.pallas import tpu as pltpu

D_MODEL = 1024
HEAD_DIM = 64
FOX_HEADS = 6
SB_HEADS = 4
ML_HEADS = 6
D_FOX = FOX_HEADS * HEAD_DIM
D_SB = SB_HEADS * HEAD_DIM
D_ML = ML_HEADS * HEAD_DIM
ML_CHUNK = 128
CONV_WIDTH = 4
X_HEADS = 4
X_HEAD_DIM = 128
D_X = X_HEADS * X_HEAD_DIM
D_FF = 2816
RMS_EPS = 1e-6

LANES = 128
PAIR = 2 * HEAD_DIM
D_MAIN = 3 * D_FOX + 3 * D_SB + 4 * D_ML
N_GATE = LANES
GT_ROWS = 32
NEG = -1e30
SB_SUB = 128
SB_EXIT = 110.0
LOG2E = 1.4426950408889634
VMEM_LIMIT = 56 * 1024 * 1024

BF16 = jnp.bfloat16
F32 = jnp.float32

FOX_Q, FOX_K, FOX_V = 0, 3, 6
SB_Q, SB_K, SB_V = 9, 11, 13
G_FOX, G_MLA, G_MLB, G_MLP = 0, 6, 12, 18
EXP_B = ML_HEADS * LANES
EXP_A = EXP_B + D_ML
N_EXP = EXP_A + D_ML


def _cparams(*sem):
    return pltpu.CompilerParams(dimension_semantics=sem, vmem_limit_bytes=VMEM_LIMIT)


def _rms(xf, w):
    ms = jnp.mean(xf * xf, axis=-1, keepdims=True)
    return xf * lax.rsqrt(ms + RMS_EPS) * w


def _dot(a, b):
    return jnp.dot(a, b, preferred_element_type=F32)


def _dot_nt(a, b):
    return lax.dot_general(a, b, (((1,), (1,)), ((), ())), preferred_element_type=F32)


def _dot_tn(a, b):
    return lax.dot_general(a, b, (((0,), (0,)), ((), ())), preferred_element_type=F32)


def _log_sigmoid(x):
    return jnp.minimum(x, 0.0) - jnp.log(1.0 + jnp.exp(-jnp.abs(x)))


def _split3(x):
    x1 = x.astype(BF16)
    r1 = x - x1.astype(F32)
    x2 = r1.astype(BF16)
    return x1, x2, (r1 - x2.astype(F32)).astype(BF16)


def _dot_exact(a, b, lhs_exact):
    if lhs_exact:
        x1, x2, x3 = _split3(b)
        return _dot(a, x1) + _dot(a, x2) + _dot(a, x3)
    x1, x2, x3 = _split3(a)
    return _dot(x1, b) + _dot(x2, b) + _dot(x3, b)


def _inproj_kernel(x_ref, nw_ref, w_ref, u_ref, g_ref, *, n_chunk):
    xn = _rms(x_ref[...], nw_ref[...]).astype(BF16)
    n_all = D_MAIN + N_GATE
    for c in range(n_all // n_chunk):
        lo = c * n_chunk
        res = _dot(xn, w_ref[:, lo:lo + n_chunk])
        if lo + n_chunk <= D_MAIN:
            u_ref[:, lo:lo + n_chunk] = res.astype(BF16)
        else:
            u_ref[:, lo:D_MAIN] = res[:, :D_MAIN - lo].astype(BF16)
            g_ref[...] = res[:, D_MAIN - lo:]


def _inproj(h, nw, w_all, layer, tm=512):
    t = h.shape[0]
    return pl.pallas_call(
        functools.partial(_inproj_kernel, n_chunk=512),
        grid=(t // tm,),
        in_specs=[
            pl.BlockSpec((tm, D_MODEL), lambda i: (i, 0)),
            pl.BlockSpec((1, D_MODEL), lambda i: (0, 0)),
            pl.BlockSpec((None, D_MODEL, D_MAIN + N_GATE), lambda i: (layer, 0, 0)),
        ],
        out_specs=[
            pl.BlockSpec((tm, D_MAIN), lambda i: (i, 0)),
            pl.BlockSpec((tm, N_GATE), lambda i: (i, 0)),
        ],
        out_shape=[jax.ShapeDtypeStruct((t, D_MAIN), BF16), jax.ShapeDtypeStruct((t, N_GATE), F32)],
        compiler_params=_cparams("parallel"),
        name="inproj",
    )(h, nw, w_all)


def _gates_kernel(g_ref, b_ref, go_ref, gt_ref, *, s):
    L = ML_CHUNK
    col = lax.broadcasted_iota(jnp.int32, (L, LANES), 1)
    row = lax.broadcasted_iota(jnp.int32, (L, LANES), 0)
    is_a = (col >= G_MLA) & (col < G_MLB)
    r = lax.broadcasted_iota(jnp.int32, (L, L), 0)
    c = lax.broadcasted_iota(jnp.int32, (L, L), 1)
    tri = jnp.where(r >= c, 1.0, 0.0).astype(BF16)
    bias = b_ref[...]

    def chunk(ci, carry):
        start = pl.multiple_of(ci * L, L)
        pre = g_ref[pl.ds(start, L), :] + bias
        cum = _dot_exact(tri, jnp.where(is_a, 0.0, _log_sigmoid(pre)), lhs_exact=True)
        tot = cum + carry
        a = pre - pltpu.roll(cum, LANES - (G_MLB - G_MLA), 1)
        pm = jnp.where(is_a, a, -jnp.inf)
        k = 1
        while k < L:
            pm = jnp.maximum(pm, jnp.where(row >= k, pltpu.roll(pm, k, 0), -jnp.inf))
            k *= 2
        pm = pltpu.roll(pm, G_MLP - G_MLA, 1)
        out = jnp.where(col < G_MLA, tot,
                        jnp.where(is_a, a, jnp.where(col < G_MLP, cum, jnp.where(col < G_MLP + ML_HEADS, pm, 0.0))))
        go_ref[pl.ds(start, L), :] = out
        gt_ref[:, pl.ds(start, L)] = out.T[:GT_ROWS, :]
        return tot[L - 1:L, :]

    lax.fori_loop(0, s // L, chunk, jnp.zeros((1, LANES), F32), unroll=4)


def _gates(g, bias, bsz, s):
    return pl.pallas_call(
        functools.partial(_gates_kernel, s=s),
        grid=(bsz,),
        in_specs=[
            pl.BlockSpec((s, N_GATE), lambda b: (b, 0)),
            pl.BlockSpec((1, N_GATE), lambda b: (0, 0)),
        ],
        out_specs=[
            pl.BlockSpec((s, N_GATE), lambda b: (b, 0)),
            pl.BlockSpec((None, GT_ROWS, s), lambda b: (b, 0, 0)),
        ],
        out_shape=[jax.ShapeDtypeStruct((bsz * s, N_GATE), F32), jax.ShapeDtypeStruct((bsz, GT_ROWS, s), F32)],
        compiler_params=_cparams("parallel"),
        name="gates",
    )(g, bias)


def _head_masks(q2):
    lane = lax.broadcasted_iota(jnp.int32, q2.shape, 1)
    zero = jnp.zeros_like(q2)
    return jnp.where(lane < HEAD_DIM, q2, zero), jnp.where(lane >= HEAD_DIM, q2, zero)


def _stack_heads(q2):
    return jnp.concatenate(_head_masks(q2), axis=0)


def _unstack_heads(x, rows):
    lane = lax.broadcasted_iota(jnp.int32, (rows, LANES), 1)
    return jnp.where(lane < HEAD_DIM, x[:rows], x[rows:])


def _fox_kernel(q_ref, k_ref, v_ref, g_ref, gt_ref, o_ref, ct_ref, m_ref, acc_ref, *, tq, nq):
    p = pl.program_id(1)
    qi = pl.program_id(2)
    half = tq // 2

    def sweep(blk):
        qs = _stack_heads(q_ref[...] * jnp.asarray(HEAD_DIM ** -0.5, BF16))
        gq = g_ref[...]
        lane = lax.broadcasted_iota(jnp.int32, (tq, LANES), 1)
        for hh in range(2):
            ct = jnp.sum(jnp.where(lane == G_FOX + 2 * p + hh, gq, 0.0), axis=1, keepdims=True)
            ct_ref[hh * tq:(hh + 1) * tq, :] = jnp.broadcast_to(ct * LOG2E, (tq, LANES))
        m_ref[...] = jnp.full_like(m_ref, NEG)
        acc_ref[...] = jnp.zeros_like(acc_ref)

        def group(start, nk, row0, masked):
            nr = tq - row0
            qsub = qs if row0 == 0 else jnp.concatenate([qs[row0:tq], qs[tq + row0:]], axis=0)
            s = _dot_nt(qsub, k_ref[start:start + nk, :])
            v2 = v_ref[start:start + nk, :]
            one = jnp.ones_like(v2)
            lo_lanes = lax.broadcasted_iota(jnp.int32, (nk, LANES), 1) < HEAD_DIM
            v_ones = (jnp.where(lo_lanes, v2, one), jnp.where(lo_lanes, one, v2))
            if masked:
                rq = lax.broadcasted_iota(jnp.int32, (nr, nk), 0)
                rk = lax.broadcasted_iota(jnp.int32, (nr, nk), 1)
                causal = jnp.where(rk <= rq, 0.0, NEG)
            for hh in range(2):
                rows = slice(hh * tq + row0, (hh + 1) * tq)
                cs = gt_ref[pl.ds(G_FOX + 2 * p + hh, 1), start:start + nk] * LOG2E
                r = s[hh * nr:(hh + 1) * nr] * LOG2E - cs
                if masked:
                    r = r + causal
                ct = ct_ref[rows, :]
                m_prev = m_ref[rows, :]
                m_next = jnp.maximum(m_prev, jnp.max(r, axis=1, keepdims=True) + ct)
                pr = jnp.exp2(r - jnp.tile(m_next - ct, (1, nk // LANES)))
                alpha = jnp.exp2(m_prev - m_next)
                acc_ref[rows, :] = alpha * acc_ref[rows, :] + _dot(pr.astype(BF16), v_ones[hh])
                m_ref[rows, :] = m_next

        for j in range(blk):
            group(j * tq, tq, 0, False)
        group(blk * tq, half, 0, True)
        group(blk * tq + half, half, half, True)
        acc = acc_ref[...]
        o_ref[...] = _unstack_heads(acc / pltpu.roll(acc, HEAD_DIM, 1), tq).astype(BF16)

    for blk in range(nq):
        pl.when(qi == blk)(functools.partial(sweep, blk))


def _fox(u, g, gt, bsz, s, tq=512):
    nq = s // tq
    stat = pltpu.VMEM((2 * tq, LANES), F32)
    return pl.pallas_call(
        functools.partial(_fox_kernel, tq=tq, nq=nq),
        grid=(bsz, FOX_HEADS // 2, nq),
        in_specs=[
            pl.BlockSpec((tq, PAIR), lambda b, p, i: (b * nq + i, FOX_Q + p)),
            pl.BlockSpec((s, PAIR), lambda b, p, i: (b, FOX_K + p)),
            pl.BlockSpec((s, PAIR), lambda b, p, i: (b, FOX_V + p)),
            pl.BlockSpec((tq, N_GATE), lambda b, p, i: (b * nq + i, 0)),
            pl.BlockSpec((None, 8, s), lambda b, p, i: (b, 0, 0)),
        ],
        out_specs=pl.BlockSpec((tq, PAIR), lambda b, p, i: (b * nq + i, p)),
        out_shape=jax.ShapeDtypeStruct((bsz * s, D_FOX), BF16),
        scratch_shapes=[stat, stat, stat],
        compiler_params=_cparams("parallel", "parallel", "arbitrary"),
        name="fox",
    )(u, u, u, g, gt)


def _sb_kernel(q_ref, k_ref, v_ref, o_ref, qs_ref, run_ref, acc_ref, *, tq, tg):
    qi = pl.program_id(2)
    qs_ref[...] = _stack_heads(q_ref[...] * jnp.asarray(HEAD_DIM ** -0.5, BF16))
    r = lax.broadcasted_iota(jnp.int32, (2 * SB_SUB, 2 * SB_SUB), 0)
    c = lax.broadcasted_iota(jnp.int32, (2 * SB_SUB, 2 * SB_SUB), 1)
    suffix = jnp.where((c >= SB_SUB) | (jnp.where(r >= SB_SUB, r - SB_SUB, r) >= c), 1.0, 0.0).astype(BF16)
    run_ref[...] = jnp.zeros_like(run_ref)
    acc_ref[...] = jnp.zeros_like(acc_ref)
    n_diag = tq // tg

    def group(g, diag):
        row0 = tq - (g + 1) * tg if diag else 0
        nr = tq - row0
        heads = lambda ref: ref[...] if row0 == 0 else jnp.concatenate([ref[row0:tq, :], ref[tq + row0:, :]], axis=0)
        start = pl.multiple_of((qi + 1) * tq - (g + 1) * tg, tg)
        z = _dot_nt(heads(qs_ref), k_ref[pl.ds(start, tg), :])
        sp = jnp.maximum(z, 0.0) + jnp.log(1.0 + jnp.exp(-jnp.abs(z)))
        if diag:
            rq = lax.broadcasted_iota(jnp.int32, (2 * nr, tg), 0)
            rk = lax.broadcasted_iota(jnp.int32, (2 * nr, tg), 1)
            valid = rk < jnp.where(rq >= nr, rq - nr, rq)
            sp = jnp.where(valid, sp, 0.0)
        hi = sp.astype(BF16)
        lo = (sp - hi.astype(F32)).astype(BF16)
        offs = heads(run_ref)
        parts = [None] * (tg // SB_SUB)
        for sb in reversed(range(tg // SB_SUB)):
            sl = slice(sb * SB_SUB, (sb + 1) * SB_SUB)
            rs = _dot(jnp.concatenate([hi[:, sl], lo[:, sl]], axis=1), suffix)
            a = jnp.exp(z[:, sl] - rs[:, :SB_SUB] - offs)
            if diag:
                a = jnp.where(valid[:, sl], a, 0.0)
            parts[sb] = a.astype(BF16)
            offs = offs + rs[:, SB_SUB:]
        pv = _dot(jnp.concatenate(parts, axis=1), v_ref[pl.ds(start, tg), :])
        for hh in range(2):
            rows = slice(hh * tq + row0, (hh + 1) * tq)
            run_ref[rows, :] = offs[hh * nr:(hh + 1) * nr]
            acc_ref[rows, :] += pv[hh * nr:(hh + 1) * nr]
        return jnp.min(offs)

    for g in range(n_diag):
        low = group(g, True)

    def cond(carry):
        g, low = carry
        return (g < (qi + 1) * n_diag) & (low < SB_EXIT)

    def body(carry):
        g, _ = carry
        return g + 1, group(g, False)

    lax.while_loop(cond, body, (jnp.int32(n_diag), low))
    o_ref[...] = _unstack_heads(acc_ref[...], tq).astype(BF16)


def _sb(u, bsz, s, tq=512, tg=256):
    nq = s // tq
    stat = pltpu.VMEM((2 * tq, LANES), F32)
    return pl.pallas_call(
        functools.partial(_sb_kernel, tq=tq, tg=tg),
        grid=(bsz, SB_HEADS // 2, nq),
        in_specs=[
            pl.BlockSpec((tq, PAIR), lambda b, p, i: (b * nq + i, SB_Q + p)),
            pl.BlockSpec((s, PAIR), lambda b, p, i: (b, SB_K + p)),
            pl.BlockSpec((s, PAIR), lambda b, p, i: (b, SB_V + p)),
        ],
        out_specs=pl.BlockSpec((tq, PAIR), lambda b, p, i: (b * nq + i, p)),
        out_shape=jax.ShapeDtypeStruct((bsz * s, D_SB), BF16),
        scratch_shapes=[pltpu.VMEM((2 * tq, PAIR), BF16), stat, stat],
        compiler_params=_cparams("parallel", "parallel", "arbitrary"),
        name="sb",
    )(u, u, u)


def _mlstm_kernel(xq_ref, xk_ref, v_ref, og_ref, g_ref, gt_ref, cw_ref, nw_ref, o_ref, qs_ref, ks_ref, e_ref, *, s):
    L = ML_CHUNK
    nc = s // L
    TAIL = 16
    row = lax.broadcasted_iota(jnp.int32, (TAIL, D_ML), 0)

    def conv_chunk(c, _):
        start = pl.multiple_of(c * L, L)
        prev_start = pl.multiple_of(jnp.maximum(c - 1, 0) * L, L)
        for x_ref, dst_ref, w0, scale in ((xq_ref, qs_ref, 0, 1.0), (xk_ref, ks_ref, D_ML, HEAD_DIM ** -0.5)):
            cur = x_ref[pl.ds(start, L), :].astype(F32)
            tail = x_ref[pl.ds(prev_start + L - TAIL, TAIL), :].astype(F32)
            tail = jnp.where(c > 0, tail, 0.0)
            w = cw_ref[:, w0:w0 + D_ML]
            y = cur * w[CONV_WIDTH - 1:CONV_WIDTH, :]
            for k in range(1, CONV_WIDTH):
                sh = pltpu.roll(cur, k, 0)
                head = jnp.where(row < k, pltpu.roll(tail, k, 0), sh[:TAIL])
                sh = jnp.concatenate([head, sh[TAIL:]], axis=0)
                y = y + sh * w[CONV_WIDTH - 1 - k:CONV_WIDTH - k, :]
            y = y * jax.nn.sigmoid(y) * scale
            dst_ref[pl.ds(start, L), :] = y.astype(BF16)
        return 0

    conv_chunk(0, 0)

    ci = lax.broadcasted_iota(jnp.int32, (LANES, N_EXP), 0)
    cj = lax.broadcasted_iota(jnp.int32, (LANES, N_EXP), 1)
    src_col = jnp.where(cj < EXP_B, G_MLP + lax.shift_right_logical(cj, 7),
                        jnp.where(cj < EXP_A, G_MLB + lax.shift_right_logical(cj - EXP_B, 6),
                                  G_MLA + lax.shift_right_logical(cj - EXP_A, 6)))
    e_ref[...] = jnp.where(ci == src_col, 1.0, 0.0).astype(BF16)

    lane = lax.broadcasted_iota(jnp.int32, (L, LANES), 1)
    lo_lanes = lane < HEAD_DIM
    r_ll = lax.broadcasted_iota(jnp.int32, (L, L), 0)
    c_ll = lax.broadcasted_iota(jnp.int32, (L, L), 1)
    tri = r_ll >= c_ll
    blockdiag = (r_ll < HEAD_DIM) == (c_ll < HEAD_DIM)
    bd_ones = jnp.where(blockdiag, 1.0, 0.0).astype(BF16)
    ones = jnp.ones((L, LANES), BF16)

    def chunk(c, carry):
        conv_chunk(jnp.minimum(c + 1, nc - 1), 0)
        start = pl.multiple_of(c * L, L)
        rep = _dot_exact(g_ref[pl.ds(start, L), :], e_ref[...], lhs_exact=False)
        gtc = gt_ref[:, pl.ds(start, L)]
        new_carry = []
        for p in range(ML_HEADS // 2):
            c2, nm, m2 = carry[p]
            sl = slice(p * PAIR, (p + 1) * PAIR)
            q2 = qs_ref[pl.ds(start, L), sl]
            k2 = ks_ref[pl.ds(start, L), sl]
            v_one = jnp.concatenate([v_ref[pl.ds(start, L), sl], ones], axis=1)
            b2 = rep[:, EXP_B + p * PAIR:EXP_B + (p + 1) * PAIR]
            a2 = rep[:, EXP_A + p * PAIR:EXP_A + (p + 1) * PAIR]
            pm = [rep[:, (2 * p + hh) * LANES:(2 * p + hh + 1) * LANES] for hh in range(2)]
            pm2 = jnp.where(lo_lanes, pm[0], pm[1])
            qk = _dot_nt(_stack_heads(q2), k2)
            sc = []
            for hh in range(2):
                a_row = gtc[G_MLA + 2 * p + hh:G_MLA + 2 * p + hh + 1, :]
                sc.append(qk[hh * L:(hh + 1) * L] * jnp.exp(jnp.where(tri, a_row - pm[hh], -jnp.inf)))
            r = _dot(jnp.concatenate(sc, axis=0).astype(BF16), v_one)
            num_loc = jnp.where(lo_lanes, r[:L, :LANES], r[L:, :LANES])
            den_loc = jnp.where(lo_lanes, r[:L, LANES:], r[L:, LANES:])
            m_loc = b2 + pm2
            inter = _dot(q2, jnp.concatenate([c2, nm], axis=1).astype(BF16))
            it2 = b2 + m2
            m_t = jnp.maximum(it2, m_loc)
            f_loc = jnp.exp(m_loc - m_t)
            w_int = jnp.exp(it2 - m_t)
            num = num_loc * f_loc + w_int * inter[:, :LANES]
            den = den_loc * f_loc + w_int * inter[:, LANES:]
            hp = num / jnp.maximum(jnp.abs(den), jnp.exp(-m_t))
            sq = hp * hp
            sq_hi = sq.astype(BF16)
            sq_lo = (sq - sq_hi.astype(F32)).astype(BF16)
            msq = (_dot(sq_hi, bd_ones) + _dot(sq_lo, bd_ones)) * (1.0 / HEAD_DIM)
            hp = hp * lax.rsqrt(msq + RMS_EPS) * nw_ref[:, sl]
            gate = jax.nn.sigmoid(og_ref[pl.ds(start, L), sl].astype(F32))
            o_ref[pl.ds(start, L), sl] = (gate * hp).astype(BF16)
            g2 = b2[L - 1:L, :]
            pl2 = pm2[L - 1:L, :]
            kw = (k2.astype(F32) * jnp.exp(a2 - pl2)).astype(BF16)
            upd = _dot_tn(kw, v_one)
            m_new = jnp.maximum(g2 + m2, g2 + pl2)
            decay = jnp.exp(g2 + m2 - m_new)
            scale = jnp.exp(g2 + pl2 - m_new)
            c2 = decay * c2 + scale * jnp.where(blockdiag, upd[:, :LANES], 0.0)
            nm = decay * nm + scale * jnp.where(blockdiag, upd[:, LANES:], 0.0)
            new_carry.append((c2, nm, m_new))
        return tuple(new_carry)

    init = tuple((jnp.zeros((LANES, LANES), F32), jnp.zeros((LANES, LANES), F32), jnp.zeros((1, LANES), F32))
                 for _ in range(ML_HEADS // 2))
    lax.fori_loop(0, nc, chunk, init, unroll=4)


def _mlstm(u, g, gt, conv_w, norm_w, bsz, s):
    blk = lambda j: pl.BlockSpec((s, D_ML), lambda b: (b, j))
    first = (3 * D_FOX + 3 * D_SB) // D_ML
    return pl.pallas_call(
        functools.partial(_mlstm_kernel, s=s),
        grid=(bsz,),
        in_specs=[
            blk(first), blk(first + 1), blk(first + 2), blk(first + 3),
            pl.BlockSpec((s, N_GATE), lambda b: (b, 0)),
            pl.BlockSpec((None, GT_ROWS, s), lambda b: (b, 0, 0)),
            pl.BlockSpec((CONV_WIDTH, 2 * D_ML), lambda b: (0, 0)),
            pl.BlockSpec((1, D_ML), lambda b: (0, 0)),
        ],
        out_specs=pl.BlockSpec((s, D_ML), lambda b: (b, 0)),
        out_shape=jax.ShapeDtypeStruct((bsz * s, D_ML), BF16),
        scratch_shapes=[pltpu.VMEM((s, D_ML), BF16), pltpu.VMEM((s, D_ML), BF16), pltpu.VMEM((LANES, N_EXP), BF16)],
        compiler_params=_cparams("parallel"),
        name="mlstm",
    )(u, u, u, u, g, gt, conv_w, norm_w)


def _memkv_kernel(x_ref, nw_ref, w_ref, o_ref):
    xn = _rms(x_ref[...], nw_ref[...]).astype(BF16)
    o_ref[...] = _dot(xn, w_ref[...]).astype(BF16)


def _memkv(mem2d, nw, w, layer, tm=512):
    t = mem2d.shape[0]
    return pl.pallas_call(
        _memkv_kernel,
        grid=(t // tm,),
        in_specs=[
            pl.BlockSpec((tm, D_MODEL), lambda i: (i, 0)),
            pl.BlockSpec((1, D_MODEL), lambda i: (0, 0)),
            pl.BlockSpec((None, D_MODEL, 2 * D_X), lambda i: (layer, 0, 0)),
        ],
        out_specs=pl.BlockSpec((tm, 2 * D_X), lambda i: (i, 0)),
        out_shape=jax.ShapeDtypeStruct((t, 2 * D_X), BF16),
        compiler_params=_cparams("parallel"),
        name="memkv",
    )(mem2d, nw, w)


def _cross_kernel(yf_ref, ys_ref, ym_ref, wo_ref, h_ref, nw_ref, wq_ref, kv_ref, wxo_ref, o_ref):
    hf = (h_ref[...] + _dot(yf_ref[...], wo_ref[:D_FOX, :]) + _dot(ys_ref[...], wo_ref[D_FOX:D_FOX + D_SB, :])
          + _dot(ym_ref[...], wo_ref[D_FOX + D_SB:, :]))
    hn = _rms(hf, nw_ref[...]).astype(BF16)
    q = _dot(hn, wq_ref[...]).astype(BF16)
    outs = []
    for hd in range(X_HEADS):
        sl = slice(hd * X_HEAD_DIM, (hd + 1) * X_HEAD_DIM)
        k = kv_ref[:, sl]
        v = kv_ref[:, D_X + hd * X_HEAD_DIM:D_X + (hd + 1) * X_HEAD_DIM]
        s = _dot_nt(q[:, sl], k) * (X_HEAD_DIM ** -0.5)
        m = jnp.max(s, axis=1, keepdims=True)
        e = jnp.exp(s - m)
        pr = e / jnp.sum(e, axis=1, keepdims=True)
        outs.append(_dot(pr.astype(BF16), v).astype(BF16))
    o = jnp.concatenate(outs, axis=1)
    o_ref[...] = hf + _dot(o, wxo_ref[...])


def _cross(yf, ys, ym, wo, h, nw, wq, kv, wxo, layer, bsz, s, mlen, tm=512):
    nt = s // tm
    row = lambda n: pl.BlockSpec((tm, n), lambda b, i: (b * nt + i, 0))
    wgt = lambda r, c: pl.BlockSpec((None, r, c), lambda b, i: (layer, 0, 0))
    return pl.pallas_call(
        _cross_kernel,
        grid=(bsz, nt),
        in_specs=[
            row(D_FOX), row(D_SB), row(D_ML), wgt(D_MODEL, D_MODEL), row(D_MODEL),
            pl.BlockSpec((1, D_MODEL), lambda b, i: (0, 0)),
            wgt(D_MODEL, D_X),
            pl.BlockSpec((mlen, 2 * D_X), lambda b, i: (b, 0)),
            wgt(D_X, D_MODEL),
        ],
        out_specs=row(D_MODEL),
        out_shape=jax.ShapeDtypeStruct((bsz * s, D_MODEL), F32),
        compiler_params=_cparams("parallel", "parallel"),
        name="cross",
    )(yf, ys, ym, wo, h, nw, wq, kv, wxo)


def _ffn_kernel(h_ref, nw_ref, wg_ref, wu_ref, wd_ref, fw_ref, o_ref, acc_ref, *, f_chunk, final_norm):
    hf = h_ref[...]
    hn = _rms(hf, nw_ref[...]).astype(BF16)
    for c in range(D_FF // f_chunk):
        sl = slice(c * f_chunk, (c + 1) * f_chunk)
        g = _dot(hn, wg_ref[:, sl])
        u = _dot(hn, wu_ref[:, sl])
        a = (g * jax.nn.sigmoid(g) * u).astype(BF16)
        part = _dot(a, wd_ref[sl, :])
        if c == 0:
            acc_ref[...] = part
        else:
            acc_ref[...] += part
    out = hf + acc_ref[...]
    if final_norm:
        out = _rms(out, fw_ref[...])
    o_ref[...] = out


def _ffn(h, nw, wg, wu, wd, fw, layer, final_norm, tm=512):
    t = h.shape[0]
    const = lambda shape: pl.BlockSpec(shape, lambda i: (0, 0))
    wgt = lambda r, c: pl.BlockSpec((None, r, c), lambda i: (layer, 0, 0))
    return pl.pallas_call(
        functools.partial(_ffn_kernel, f_chunk=256, final_norm=final_norm),
        grid=(t // tm,),
        in_specs=[
            pl.BlockSpec((tm, D_MODEL), lambda i: (i, 0)),
            const((1, D_MODEL)),
            wgt(D_MODEL, D_FF), wgt(D_MODEL, D_FF), wgt(D_FF, D_MODEL),
            const((1, D_MODEL)),
        ],
        out_specs=pl.BlockSpec((tm, D_MODEL), lambda i: (i, 0)),
        out_shape=jax.ShapeDtypeStruct((t, D_MODEL), F32),
        scratch_shapes=[pltpu.VMEM((tm, D_MODEL), F32)],
        compiler_params=_cparams("parallel"),
        name="ffn",
    )(h, nw, wg, wu, wd, fw)


N_FOX_QKV = 3 * D_FOX
N_REST = 3 * D_SB + 4 * D_ML


N_USED_GATES = FOX_HEADS + 2 * ML_HEADS


def _regroup_kernel(w_ref, o_ref):
    o_ref[:, :N_FOX_QKV] = w_ref[:, :N_FOX_QKV].astype(BF16)
    rest = w_ref[:, N_FOX_QKV + FOX_HEADS:N_FOX_QKV + FOX_HEADS + N_REST]
    o_ref[:, N_FOX_QKV:D_MAIN] = rest.astype(BF16)
    lane = lax.broadcasted_iota(jnp.int32, (w_ref.shape[0], N_USED_GATES), 1)
    gates = jnp.where(lane < FOX_HEADS, w_ref[:, N_FOX_QKV:N_FOX_QKV + N_USED_GATES], w_ref[:, D_MAIN:D_MAIN + N_USED_GATES])
    o_ref[:, D_MAIN:] = jnp.zeros((w_ref.shape[0], N_GATE), BF16)
    o_ref[:, D_MAIN:D_MAIN + N_USED_GATES] = gates.astype(BF16)


def _regroup_w_in(w, tr=256):
    depth, rows, n_in = w.shape
    assert N_FOX_QKV + N_REST == D_MAIN and D_MAIN % LANES == 0 and n_in == D_MAIN + N_USED_GATES
    return pl.pallas_call(
        _regroup_kernel,
        grid=(depth, rows // tr),
        in_specs=[pl.BlockSpec((None, tr, n_in), lambda l, i: (l, i, 0))],
        out_specs=pl.BlockSpec((None, tr, D_MAIN + N_GATE), lambda l, i: (l, i, 0)),
        out_shape=jax.ShapeDtypeStruct((depth, rows, D_MAIN + N_GATE), BF16),
        compiler_params=_cparams("parallel", "parallel"),
        name="regroup",
    )(w)


def kernel(x, mem, norm_mix_w, w_in, fox_f_b, ml_conv_w, ml_i_b, ml_f_b, ml_norm_w, w_out, norm_x_w, mem_norm_w,
           wx_q, wx_kv, wx_o, norm_ffn_w, w_gate, w_up, w_down, final_norm_w):
    bsz, s, _ = x.shape
    mlen = mem.shape[1]
    depth = w_in.shape[0]
    h = x.reshape(bsz * s, D_MODEL)
    mem2d = mem.reshape(bsz * mlen, D_MODEL)
    row = lambda v: v.reshape(1, -1)
    w_all = _regroup_w_in(w_in)
    bias = jnp.concatenate([fox_f_b, ml_i_b, ml_f_b, jnp.zeros((depth, N_GATE - 3 * FOX_HEADS), F32)], axis=1)
    wo, wq, wkv, wxo = (w.astype(BF16) for w in (w_out, wx_q, wx_kv, wx_o))
    wg, wu, wd = (w.astype(BF16) for w in (w_gate, w_up, w_down))
    for l in range(depth):
        u, g_raw = _inproj(h, row(norm_mix_w[l]), w_all, l)
        g, gt = _gates(g_raw, row(bias[l]), bsz, s)
        y_fox = _fox(u, g, gt, bsz, s)
        y_sb = _sb(u, bsz, s)
        y_ml = _mlstm(u, g, gt, ml_conv_w[l], row(ml_norm_w[l]), bsz, s)
        kv = _memkv(mem2d, row(mem_norm_w[l]), wkv, l, tm=min(512, bsz * mlen))
        h = _cross(y_fox, y_sb, y_ml, wo, h, row(norm_x_w[l]), wq, kv, wxo, l, bsz, s, mlen)
        h = _ffn(h, row(norm_ffn_w[l]), wg, wu, wd, row(final_norm_w), l, final_norm=(l == depth - 1))
    return h.reshape(bsz, s, D_MODEL)
```

```python
import functools

import jax
import jax.numpy as jnp
from jax import lax
from jax.experimental import pallas as pl
from jax.experimental.pallas import tpu as pltpu

D_MODEL = 1024
HEAD_DIM = 64
FOX_HEADS = 6
SB_HEADS = 4
ML_HEADS = 6
D_FOX = FOX_HEADS * HEAD_DIM
D_SB = SB_HEADS * HEAD_DIM
D_ML = ML_HEADS * HEAD_DIM
ML_CHUNK = 128
CONV_WIDTH = 4
X_HEADS = 4
X_HEAD_DIM = 128
D_X = X_HEADS * X_HEAD_DIM
D_FF = 2816
RMS_EPS = 1e-6

LANES = 128
PAIR = 2 * HEAD_DIM
D_MAIN = 3 * D_FOX + 3 * D_SB + 4 * D_ML
N_GATE = LANES
GT_ROWS = 32
NEG = -1e30
SB_SUB = 128
SB_EXIT = 110.0
LOG2E = 1.4426950408889634
VMEM_LIMIT = 56 * 1024 * 1024

BF16 = jnp.bfloat16
F32 = jnp.float32

FOX_Q, FOX_K, FOX_V = 0, 3, 6
SB_Q, SB_K, SB_V = 9, 11, 13
G_FOX, G_MLA, G_MLB, G_MLP = 0, 6, 12, 18
EXP_B = ML_HEADS * LANES
EXP_A = EXP_B + D_ML
N_EXP = EXP_A + D_ML


def _cparams(*sem):
    return pltpu.CompilerParams(dimension_semantics=sem, vmem_limit_bytes=VMEM_LIMIT)


def _rms(xf, w):
    ms = jnp.mean(xf * xf, axis=-1, keepdims=True)
    return xf * lax.rsqrt(ms + RMS_EPS) * w


def _dot(a, b):
    return jnp.dot(a, b, preferred_element_type=F32)


def _dot_nt(a, b):
    return lax.dot_general(a, b, (((1,), (1,)), ((), ())), preferred_element_type=F32)


def _dot_tn(a, b):
    return lax.dot_general(a, b, (((0,), (0,)), ((), ())), preferred_element_type=F32)


def _log_sigmoid(x):
    return jnp.minimum(x, 0.0) - jnp.log(1.0 + jnp.exp(-jnp.abs(x)))


def _split3(x):
    x1 = x.astype(BF16)
    r1 = x - x1.astype(F32)
    x2 = r1.astype(BF16)
    return x1, x2, (r1 - x2.astype(F32)).astype(BF16)


def _dot_exact(a, b, lhs_exact):
    if lhs_exact:
        x1, x2, x3 = _split3(b)
        return _dot(a, x1) + _dot(a, x2) + _dot(a, x3)
    x1, x2, x3 = _split3(a)
    return _dot(x1, b) + _dot(x2, b) + _dot(x3, b)


def _inproj_kernel(x_ref, nw_ref, w_ref, u_ref, g_ref, *, n_chunk):
    xn = _rms(x_ref[...], nw_ref[...]).astype(BF16)
    n_all = D_MAIN + N_GATE
    for c in range(n_all // n_chunk):
        lo = c * n_chunk
        res = _dot(xn, w_ref[:, lo:lo + n_chunk])
        if lo + n_chunk <= D_MAIN:
            u_ref[:, lo:lo + n_chunk] = res.astype(BF16)
        else:
            u_ref[:, lo:D_MAIN] = res[:, :D_MAIN - lo].astype(BF16)
            g_ref[...] = res[:, D_MAIN - lo:]


def _inproj(h, nw, w_all, layer, tm=512):
    t = h.shape[0]
    return pl.pallas_call(
        functools.partial(_inproj_kernel, n_chunk=512),
        grid=(t // tm,),
        in_specs=[
            pl.BlockSpec((tm, D_MODEL), lambda i: (i, 0)),
            pl.BlockSpec((1, D_MODEL), lambda i: (0, 0)),
            pl.BlockSpec((None, D_MODEL, D_MAIN + N_GATE), lambda i: (layer, 0, 0)),
        ],
        out_specs=[
            pl.BlockSpec((tm, D_MAIN), lambda i: (i, 0)),
            pl.BlockSpec((tm, N_GATE), lambda i: (i, 0)),
        ],
        out_shape=[jax.ShapeDtypeStruct((t, D_MAIN), BF16), jax.ShapeDtypeStruct((t, N_GATE), F32)],
        compiler_params=_cparams("parallel"),
        name="inproj",
    )(h, nw, w_all)


def _gates_kernel(g_ref, b_ref, go_ref, gt_ref, *, s):
    L = ML_CHUNK
    col = lax.broadcasted_iota(jnp.int32, (L, LANES), 1)
    row = lax.broadcasted_iota(jnp.int32, (L, LANES), 0)
    is_a = (col >= G_MLA) & (col < G_MLB)
    r = lax.broadcasted_iota(jnp.int32, (L, L), 0)
    c = lax.broadcasted_iota(jnp.int32, (L, L), 1)
    tri = jnp.where(r >= c, 1.0, 0.0).astype(BF16)
    bias = b_ref[...]

    def chunk(ci, carry):
        start = pl.multiple_of(ci * L, L)
        pre = g_ref[pl.ds(start, L), :] + bias
        cum = _dot_exact(tri, jnp.where(is_a, 0.0, _log_sigmoid(pre)), lhs_exact=True)
        tot = cum + carry
        a = pre - pltpu.roll(cum, LANES - (G_MLB - G_MLA), 1)
        pm = jnp.where(is_a, a, -jnp.inf)
        k = 1
        while k < L:
            pm = jnp.maximum(pm, jnp.where(row >= k, pltpu.roll(pm, k, 0), -jnp.inf))
            k *= 2
        pm = pltpu.roll(pm, G_MLP - G_MLA, 1)
        out = jnp.where(col < G_MLA, tot,
                        jnp.where(is_a, a, jnp.where(col < G_MLP, cum, jnp.where(col < G_MLP + ML_HEADS, pm, 0.0))))
        go_ref[pl.ds(start, L), :] = out
        gt_ref[:, pl.ds(start, L)] = out.T[:GT_ROWS, :]
        return tot[L - 1:L, :]

    lax.fori_loop(0, s // L, chunk, jnp.zeros((1, LANES), F32), unroll=4)


def _gates(g, bias, bsz, s):
    return pl.pallas_call(
        functools.partial(_gates_kernel, s=s),
        grid=(bsz,),
        in_specs=[
            pl.BlockSpec((s, N_GATE), lambda b: (b, 0)),
            pl.BlockSpec((1, N_GATE), lambda b: (0, 0)),
        ],
        out_specs=[
            pl.BlockSpec((s, N_GATE), lambda b: (b, 0)),
            pl.BlockSpec((None, GT_ROWS, s), lambda b: (b, 0, 0)),
        ],
        out_shape=[jax.ShapeDtypeStruct((bsz * s, N_GATE), F32), jax.ShapeDtypeStruct((bsz, GT_ROWS, s), F32)],
        compiler_params=_cparams("parallel"),
        name="gates",
    )(g, bias)


def _head_masks(q2):
    lane = lax.broadcasted_iota(jnp.int32, q2.shape, 1)
    zero = jnp.zeros_like(q2)
    return jnp.where(lane < HEAD_DIM, q2, zero), jnp.where(lane >= HEAD_DIM, q2, zero)


def _stack_heads(q2):
    return jnp.concatenate(_head_masks(q2), axis=0)


def _unstack_heads(x, rows):
    lane = lax.broadcasted_iota(jnp.int32, (rows, LANES), 1)
    return jnp.where(lane < HEAD_DIM, x[:rows], x[rows:])


def _fox_kernel(q_ref, k_ref, v_ref, g_ref, gt_ref, o_ref, ct_ref, m_ref, acc_ref, *, tq, nq):
    p = pl.program_id(1)
    qi = pl.program_id(2)
    half = tq // 2

    def sweep(blk):
        qs = _stack_heads(q_ref[...] * jnp.asarray(HEAD_DIM ** -0.5, BF16))
        gq = g_ref[...]
        lane = lax.broadcasted_iota(jnp.int32, (tq, LANES), 1)
        for hh in range(2):
            ct = jnp.sum(jnp.where(lane == G_FOX + 2 * p + hh, gq, 0.0), axis=1, keepdims=True)
            ct_ref[hh * tq:(hh + 1) * tq, :] = jnp.broadcast_to(ct * LOG2E, (tq, LANES))
        m_ref[...] = jnp.full_like(m_ref, NEG)
        acc_ref[...] = jnp.zeros_like(acc_ref)

        def group(start, nk, row0, masked):
            nr = tq - row0
            qsub = qs if row0 == 0 else jnp.concatenate([qs[row0:tq], qs[tq + row0:]], axis=0)
            s = _dot_nt(qsub, k_ref[start:start + nk, :])
            v2 = v_ref[start:start + nk, :]
            one = jnp.ones_like(v2)
            lo_lanes = lax.broadcasted_iota(jnp.int32, (nk, LANES), 1) < HEAD_DIM
            v_ones = (jnp.where(lo_lanes, v2, one), jnp.where(lo_lanes, one, v2))
            if masked:
                rq = lax.broadcasted_iota(jnp.int32, (nr, nk), 0)
                rk = lax.broadcasted_iota(jnp.int32, (nr, nk), 1)
                causal = jnp.where(rk <= rq, 0.0, NEG)
            for hh in range(2):
                rows = slice(hh * tq + row0, (hh + 1) * tq)
                cs = gt_ref[pl.ds(G_FOX + 2 * p + hh, 1), start:start + nk] * LOG2E
                r = s[hh * nr:(hh + 1) * nr] * LOG2E - cs
                if masked:
                    r = r + causal
                ct = ct_ref[rows, :]
                m_prev = m_ref[rows, :]
                m_next = jnp.maximum(m_prev, jnp.max(r, axis=1, keepdims=True) + ct)
                pr = jnp.exp2(r - jnp.tile(m_next - ct, (1, nk // LANES)))
                alpha = jnp.exp2(m_prev - m_next)
                acc_ref[rows, :] = alpha * acc_ref[rows, :] + _dot(pr.astype(BF16), v_ones[hh])
                m_ref[rows, :] = m_next

        for j in range(blk):
            group(j * tq, tq, 0, False)
        group(blk * tq, half, 0, True)
        group(blk * tq + half, half, half, True)
        acc = acc_ref[...]
        o_ref[...] = _unstack_heads(acc / pltpu.roll(acc, HEAD_DIM, 1), tq).astype(BF16)

    for blk in range(nq):
        pl.when(qi == blk)(functools.partial(sweep, blk))


def _fox(u, g, gt, bsz, s, tq=512):
    nq = s // tq
    stat = pltpu.VMEM((2 * tq, LANES), F32)
    return pl.pallas_call(
        functools.partial(_fox_kernel, tq=tq, nq=nq),
        grid=(bsz, FOX_HEADS // 2, nq),
        in_specs=[
            pl.BlockSpec((tq, PAIR), lambda b, p, i: (b * nq + i, FOX_Q + p)),
            pl.BlockSpec((s, PAIR), lambda b, p, i: (b, FOX_K + p)),
            pl.BlockSpec((s, PAIR), lambda b, p, i: (b, FOX_V + p)),
            pl.BlockSpec((tq, N_GATE), lambda b, p, i: (b * nq + i, 0)),
            pl.BlockSpec((None, 8, s), lambda b, p, i: (b, 0, 0)),
        ],
        out_specs=pl.BlockSpec((tq, PAIR), lambda b, p, i: (b * nq + i, p)),
        out_shape=jax.ShapeDtypeStruct((bsz * s, D_FOX), BF16),
        scratch_shapes=[stat, stat, stat],
        compiler_params=_cparams("parallel", "parallel", "arbitrary"),
        name="fox",
    )(u, u, u, g, gt)


def _sb_kernel(q_ref, k_ref, v_ref, o_ref, qs_ref, run_ref, acc_ref, *, tq, tg, nq):
    qi = pl.program_id(2)
    n_diag = tq // tg

    def sweep(blk):
        qs_ref[...] = _stack_heads(q_ref[...] * jnp.asarray(HEAD_DIM ** -0.5, BF16))
        r = lax.broadcasted_iota(jnp.int32, (2 * SB_SUB, 2 * SB_SUB), 0)
        c = lax.broadcasted_iota(jnp.int32, (2 * SB_SUB, 2 * SB_SUB), 1)
        suffix = jnp.where((c >= SB_SUB) | (jnp.where(r >= SB_SUB, r - SB_SUB, r) >= c), 1.0, 0.0).astype(BF16)
        run_ref[...] = jnp.zeros_like(run_ref)
        acc_ref[...] = jnp.zeros_like(acc_ref)

        def group(g, diag):
            row0 = tq - (g + 1) * tg if diag else 0
            nr = tq - row0
            heads = lambda ref: ref[...] if row0 == 0 else jnp.concatenate([ref[row0:tq, :], ref[tq + row0:, :]], axis=0)
            start = (blk + 1) * tq - (g + 1) * tg
            if not isinstance(start, int):
                start = pl.multiple_of(start, tg)
            z = _dot_nt(heads(qs_ref), k_ref[pl.ds(start, tg), :])
            sp = jnp.maximum(z, 0.0) + jnp.log(1.0 + jnp.exp(-jnp.abs(z)))
            if diag:
                rq = lax.broadcasted_iota(jnp.int32, (2 * nr, tg), 0)
                rk = lax.broadcasted_iota(jnp.int32, (2 * nr, tg), 1)
                valid = rk < jnp.where(rq >= nr, rq - nr, rq)
                sp = jnp.where(valid, sp, 0.0)
            hi = sp.astype(BF16)
            lo = (sp - hi.astype(F32)).astype(BF16)
            offs = heads(run_ref)
            parts = [None] * (tg // SB_SUB)
            for sb in reversed(range(tg // SB_SUB)):
                sl = slice(sb * SB_SUB, (sb + 1) * SB_SUB)
                rs = _dot(jnp.concatenate([hi[:, sl], lo[:, sl]], axis=1), suffix)
                a = jnp.exp(z[:, sl] - rs[:, :SB_SUB] - offs)
                if diag:
                    a = jnp.where(valid[:, sl], a, 0.0)
                parts[sb] = a.astype(BF16)
                offs = offs + rs[:, SB_SUB:]
            pv = _dot(jnp.concatenate(parts, axis=1), v_ref[pl.ds(start, tg), :])
            for hh in range(2):
                rows = slice(hh * tq + row0, (hh + 1) * tq)
                run_ref[rows, :] = offs[hh * nr:(hh + 1) * nr]
                acc_ref[rows, :] += pv[hh * nr:(hh + 1) * nr]
            return jnp.min(offs)

        for g in range(n_diag):
            low = group(g, True)
        n_groups = (blk + 1) * n_diag
        first = n_diag
        if blk > 0:
            low = group(n_diag, False)
            first += 1
        if first < n_groups:
            def cond(carry):
                g, low = carry
                return (g < n_groups) & (low < SB_EXIT)

            def body(carry):
                g, _ = carry
                return g + 1, group(g, False)

            lax.while_loop(cond, body, (jnp.int32(first), low))
        o_ref[...] = _unstack_heads(acc_ref[...], tq).astype(BF16)

    for blk in range(nq):
        pl.when(qi == blk)(functools.partial(sweep, blk))


def _sb(u, bsz, s, tq=512, tg=256):
    nq = s // tq
    stat = pltpu.VMEM((2 * tq, LANES), F32)
    return pl.pallas_call(
        functools.partial(_sb_kernel, tq=tq, tg=tg, nq=nq),
        grid=(bsz, SB_HEADS // 2, nq),
        in_specs=[
            pl.BlockSpec((tq, PAIR), lambda b, p, i: (b * nq + i, SB_Q + p)),
            pl.BlockSpec((s, PAIR), lambda b, p, i: (b, SB_K + p)),
            pl.BlockSpec((s, PAIR), lambda b, p, i: (b, SB_V + p)),
        ],
        out_specs=pl.BlockSpec((tq, PAIR), lambda b, p, i: (b * nq + i, p)),
        out_shape=jax.ShapeDtypeStruct((bsz * s, D_SB), BF16),
        scratch_shapes=[pltpu.VMEM((2 * tq, PAIR), BF16), stat, stat],
        compiler_params=_cparams("parallel", "parallel", "arbitrary"),
        name="sb",
    )(u, u, u)


def _mlstm_kernel(xq_ref, xk_ref, v_ref, og_ref, g_ref, gt_ref, cw_ref, nw_ref, o_ref, qs_ref, ks_ref, e_ref, *, s):
    L = ML_CHUNK
    nc = s // L
    TAIL = 16
    row = lax.broadcasted_iota(jnp.int32, (TAIL, D_ML), 0)

    def conv_chunk(c, _):
        start = pl.multiple_of(c * L, L)
        prev_start = pl.multiple_of(jnp.maximum(c - 1, 0) * L, L)
        for x_ref, dst_ref, w0, scale in ((xq_ref, qs_ref, 0, 1.0), (xk_ref, ks_ref, D_ML, HEAD_DIM ** -0.5)):
            cur = x_ref[pl.ds(start, L), :].astype(F32)
            tail = x_ref[pl.ds(prev_start + L - TAIL, TAIL), :].astype(F32)
            tail = jnp.where(c > 0, tail, 0.0)
            w = cw_ref[:, w0:w0 + D_ML]
            y = cur * w[CONV_WIDTH - 1:CONV_WIDTH, :]
            for k in range(1, CONV_WIDTH):
                sh = pltpu.roll(cur, k, 0)
                head = jnp.where(row < k, pltpu.roll(tail, k, 0), sh[:TAIL])
                sh = jnp.concatenate([head, sh[TAIL:]], axis=0)
                y = y + sh * w[CONV_WIDTH - 1 - k:CONV_WIDTH - k, :]
            y = y * jax.nn.sigmoid(y) * scale
            dst_ref[pl.ds(start, L), :] = y.astype(BF16)
        return 0

    conv_chunk(0, 0)

    ci = lax.broadcasted_iota(jnp.int32, (LANES, N_EXP), 0)
    cj = lax.broadcasted_iota(jnp.int32, (LANES, N_EXP), 1)
    src_col = jnp.where(cj < EXP_B, G_MLP + lax.shift_right_logical(cj, 7),
                        jnp.where(cj < EXP_A, G_MLB + lax.shift_right_logical(cj - EXP_B, 6),
                                  G_MLA + lax.shift_right_logical(cj - EXP_A, 6)))
    e_ref[...] = jnp.where(ci == src_col, 1.0, 0.0).astype(BF16)

    lane = lax.broadcasted_iota(jnp.int32, (L, LANES), 1)
    lo_lanes = lane < HEAD_DIM
    r_ll = lax.broadcasted_iota(jnp.int32, (L, L), 0)
    c_ll = lax.broadcasted_iota(jnp.int32, (L, L), 1)
    tri = r_ll >= c_ll
    blockdiag = (r_ll < HEAD_DIM) == (c_ll < HEAD_DIM)
    bd_ones = jnp.where(blockdiag, 1.0, 0.0).astype(BF16)
    ones = jnp.ones((L, LANES), BF16)

    def chunk(c, carry):
        conv_chunk(jnp.minimum(c + 1, nc - 1), 0)
        start = pl.multiple_of(c * L, L)
        rep = _dot_exact(g_ref[pl.ds(start, L), :], e_ref[...], lhs_exact=False)
        gtc = gt_ref[:, pl.ds(start, L)]
        new_carry = []
        for p in range(ML_HEADS // 2):
            c2, nm, m2 = carry[p]
            sl = slice(p * PAIR, (p + 1) * PAIR)
            q2 = qs_ref[pl.ds(start, L), sl]
            k2 = ks_ref[pl.ds(start, L), sl]
            v_one = jnp.concatenate([v_ref[pl.ds(start, L), sl], ones], axis=1)
            b2 = rep[:, EXP_B + p * PAIR:EXP_B + (p + 1) * PAIR]
            a2 = rep[:, EXP_A + p * PAIR:EXP_A + (p + 1) * PAIR]
            pm = [rep[:, (2 * p + hh) * LANES:(2 * p + hh + 1) * LANES] for hh in range(2)]
            pm2 = jnp.where(lo_lanes, pm[0], pm[1])
            qk = _dot_nt(_stack_heads(q2), k2)
            sc = []
            for hh in range(2):
                a_row = gtc[G_MLA + 2 * p + hh:G_MLA + 2 * p + hh + 1, :]
                sc.append(qk[hh * L:(hh + 1) * L] * jnp.exp(jnp.where(tri, a_row - pm[hh], -jnp.inf)))
            r = _dot(jnp.concatenate(sc, axis=0).astype(BF16), v_one)
            num_loc = jnp.where(lo_lanes, r[:L, :LANES], r[L:, :LANES])
            den_loc = jnp.where(lo_lanes, r[:L, LANES:], r[L:, LANES:])
            m_loc = b2 + pm2
            inter = _dot(q2, jnp.concatenate([c2, nm], axis=1).astype(BF16))
            it2 = b2 + m2
            m_t = jnp.maximum(it2, m_loc)
            f_loc = jnp.exp(m_loc - m_t)
            w_int = jnp.exp(it2 - m_t)
            num = num_loc * f_loc + w_int * inter[:, :LANES]
            den = den_loc * f_loc + w_int * inter[:, LANES:]
            hp = num / jnp.maximum(jnp.abs(den), jnp.exp(-m_t))
            sq = hp * hp
            sq_hi = sq.astype(BF16)
            sq_lo = (sq - sq_hi.astype(F32)).astype(BF16)
            msq = (_dot(sq_hi, bd_ones) + _dot(sq_lo, bd_ones)) * (1.0 / HEAD_DIM)
            hp = hp * lax.rsqrt(msq + RMS_EPS) * nw_ref[:, sl]
            gate = jax.nn.sigmoid(og_ref[pl.ds(start, L), sl].astype(F32))
            o_ref[pl.ds(start, L), sl] = (gate * hp).astype(BF16)
            g2 = b2[L - 1:L, :]
            pl2 = pm2[L - 1:L, :]
            kw = (k2.astype(F32) * jnp.exp(a2 - pl2)).astype(BF16)
            upd = _dot_tn(kw, v_one)
            m_new = jnp.maximum(g2 + m2, g2 + pl2)
            decay = jnp.exp(g2 + m2 - m_new)
            scale = jnp.exp(g2 + pl2 - m_new)
            c2 = decay * c2 + scale * jnp.where(blockdiag, upd[:, :LANES], 0.0)
            nm = decay * nm + scale * jnp.where(blockdiag, upd[:, LANES:], 0.0)
            new_carry.append((c2, nm, m_new))
        return tuple(new_carry)

    init = tuple((jnp.zeros((LANES, LANES), F32), jnp.zeros((LANES, LANES), F32), jnp.zeros((1, LANES), F32))
                 for _ in range(ML_HEADS // 2))
    lax.fori_loop(0, nc, chunk, init, unroll=4)


def _mlstm(u, g, gt, conv_w, norm_w, bsz, s):
    blk = lambda j: pl.BlockSpec((s, D_ML), lambda b: (b, j))
    first = (3 * D_FOX + 3 * D_SB) // D_ML
    return pl.pallas_call(
        functools.partial(_mlstm_kernel, s=s),
        grid=(bsz,),
        in_specs=[
            blk(first), blk(first + 1), blk(first + 2), blk(first + 3),
            pl.BlockSpec((s, N_GATE), lambda b: (b, 0)),
            pl.BlockSpec((None, GT_ROWS, s), lambda b: (b, 0, 0)),
            pl.BlockSpec((CONV_WIDTH, 2 * D_ML), lambda b: (0, 0)),
            pl.BlockSpec((1, D_ML), lambda b: (0, 0)),
        ],
        out_specs=pl.BlockSpec((s, D_ML), lambda b: (b, 0)),
        out_shape=jax.ShapeDtypeStruct((bsz * s, D_ML), BF16),
        scratch_shapes=[pltpu.VMEM((s, D_ML), BF16), pltpu.VMEM((s, D_ML), BF16), pltpu.VMEM((LANES, N_EXP), BF16)],
        compiler_params=_cparams("parallel"),
        name="mlstm",
    )(u, u, u, u, g, gt, conv_w, norm_w)


def _memkv_kernel(x_ref, nw_ref, w_ref, o_ref):
    xn = _rms(x_ref[...], nw_ref[...]).astype(BF16)
    o_ref[...] = _dot(xn, w_ref[...]).astype(BF16)


def _memkv(mem2d, nw, w, layer, tm=512):
    t = mem2d.shape[0]
    return pl.pallas_call(
        _memkv_kernel,
        grid=(t // tm,),
        in_specs=[
            pl.BlockSpec((tm, D_MODEL), lambda i: (i, 0)),
            pl.BlockSpec((1, D_MODEL), lambda i: (0, 0)),
            pl.BlockSpec((None, D_MODEL, 2 * D_X), lambda i: (layer, 0, 0)),
        ],
        out_specs=pl.BlockSpec((tm, 2 * D_X), lambda i: (i, 0)),
        out_shape=jax.ShapeDtypeStruct((t, 2 * D_X), BF16),
        compiler_params=_cparams("parallel"),
        name="memkv",
    )(mem2d, nw, w)


def _cross_kernel(yf_ref, ys_ref, ym_ref, wo_ref, h_ref, nw_ref, wq_ref, kv_ref, wxo_ref, o_ref):
    hf = (h_ref[...] + _dot(yf_ref[...], wo_ref[:D_FOX, :]) + _dot(ys_ref[...], wo_ref[D_FOX:D_FOX + D_SB, :])
          + _dot(ym_ref[...], wo_ref[D_FOX + D_SB:, :]))
    hn = _rms(hf, nw_ref[...]).astype(BF16)
    q = _dot(hn, wq_ref[...]).astype(BF16)
    outs = []
    for hd in range(X_HEADS):
        sl = slice(hd * X_HEAD_DIM, (hd + 1) * X_HEAD_DIM)
        k = kv_ref[:, sl]
        v = kv_ref[:, D_X + hd * X_HEAD_DIM:D_X + (hd + 1) * X_HEAD_DIM]
        s = _dot_nt(q[:, sl], k) * (X_HEAD_DIM ** -0.5)
        m = jnp.max(s, axis=1, keepdims=True)
        e = jnp.exp(s - m)
        pr = e / jnp.sum(e, axis=1, keepdims=True)
        outs.append(_dot(pr.astype(BF16), v).astype(BF16))
    o = jnp.concatenate(outs, axis=1)
    o_ref[...] = hf + _dot(o, wxo_ref[...])


def _cross(yf, ys, ym, wo, h, nw, wq, kv, wxo, layer, bsz, s, mlen, tm=512):
    nt = s // tm
    row = lambda n: pl.BlockSpec((tm, n), lambda b, i: (b * nt + i, 0))
    wgt = lambda r, c: pl.BlockSpec((None, r, c), lambda b, i: (layer, 0, 0))
    return pl.pallas_call(
        _cross_kernel,
        grid=(bsz, nt),
        in_specs=[
            row(D_FOX), row(D_SB), row(D_ML), wgt(D_MODEL, D_MODEL), row(D_MODEL),
            pl.BlockSpec((1, D_MODEL), lambda b, i: (0, 0)),
            wgt(D_MODEL, D_X),
            pl.BlockSpec((mlen, 2 * D_X), lambda b, i: (b, 0)),
            wgt(D_X, D_MODEL),
        ],
        out_specs=row(D_MODEL),
        out_shape=jax.ShapeDtypeStruct((bsz * s, D_MODEL), F32),
        compiler_params=_cparams("parallel", "parallel"),
        name="cross",
    )(yf, ys, ym, wo, h, nw, wq, kv, wxo)


def _ffn_kernel(h_ref, nw_ref, wg_ref, wu_ref, wd_ref, fw_ref, o_ref, acc_ref, *, f_chunk, final_norm):
    hf = h_ref[...]
    hn = _rms(hf, nw_ref[...]).astype(BF16)
    for c in range(D_FF // f_chunk):
        sl = slice(c * f_chunk, (c + 1) * f_chunk)
        g = _dot(hn, wg_ref[:, sl])
        u = _dot(hn, wu_ref[:, sl])
        a = (g * jax.nn.sigmoid(g) * u).astype(BF16)
        part = _dot(a, wd_ref[sl, :])
        if c == 0:
            acc_ref[...] = part
        else:
            acc_ref[...] += part
    out = hf + acc_ref[...]
    if final_norm:
        out = _rms(out, fw_ref[...])
    o_ref[...] = out


def _ffn(h, nw, wg, wu, wd, fw, layer, final_norm, tm=512):
    t = h.shape[0]
    const = lambda shape: pl.BlockSpec(shape, lambda i: (0, 0))
    wgt = lambda r, c: pl.BlockSpec((None, r, c), lambda i: (layer, 0, 0))
    return pl.pallas_call(
        functools.partial(_ffn_kernel, f_chunk=256, final_norm=final_norm),
        grid=(t // tm,),
        in_specs=[
            pl.BlockSpec((tm, D_MODEL), lambda i: (i, 0)),
            const((1, D_MODEL)),
            wgt(D_MODEL, D_FF), wgt(D_MODEL, D_FF), wgt(D_FF, D_MODEL),
            const((1, D_MODEL)),
        ],
        out_specs=pl.BlockSpec((tm, D_MODEL), lambda i: (i, 0)),
        out_shape=jax.ShapeDtypeStruct((t, D_MODEL), F32),
        scratch_shapes=[pltpu.VMEM((tm, D_MODEL), F32)],
        compiler_params=_cparams("parallel"),
        name="ffn",
    )(h, nw, wg, wu, wd, fw)


N_FOX_QKV = 3 * D_FOX
N_REST = 3 * D_SB + 4 * D_ML


N_USED_GATES = FOX_HEADS + 2 * ML_HEADS


def _regroup_kernel(w_ref, o_ref):
    o_ref[:, :N_FOX_QKV] = w_ref[:, :N_FOX_QKV].astype(BF16)
    rest = w_ref[:, N_FOX_QKV + FOX_HEADS:N_FOX_QKV + FOX_HEADS + N_REST]
    o_ref[:, N_FOX_QKV:D_MAIN] = rest.astype(BF16)
    lane = lax.broadcasted_iota(jnp.int32, (w_ref.shape[0], N_USED_GATES), 1)
    gates = jnp.where(lane < FOX_HEADS, w_ref[:, N_FOX_QKV:N_FOX_QKV + N_USED_GATES], w_ref[:, D_MAIN:D_MAIN + N_USED_GATES])
    o_ref[:, D_MAIN:] = jnp.zeros((w_ref.shape[0], N_GATE), BF16)
    o_ref[:, D_MAIN:D_MAIN + N_USED_GATES] = gates.astype(BF16)


def _regroup_w_in(w, tr=256):
    depth, rows, n_in = w.shape
    assert N_FOX_QKV + N_REST == D_MAIN and D_MAIN % LANES == 0 and n_in == D_MAIN + N_USED_GATES
    return pl.pallas_call(
        _regroup_kernel,
        grid=(depth, rows // tr),
        in_specs=[pl.BlockSpec((None, tr, n_in), lambda l, i: (l, i, 0))],
        out_specs=pl.BlockSpec((None, tr, D_MAIN + N_GATE), lambda l, i: (l, i, 0)),
        out_shape=jax.ShapeDtypeStruct((depth, rows, D_MAIN + N_GATE), BF16),
        compiler_params=_cparams("parallel", "parallel"),
        name="regroup",
    )(w)


def kernel(x, mem, norm_mix_w, w_in, fox_f_b, ml_conv_w, ml_i_b, ml_f_b, ml_norm_w, w_out, norm_x_w, mem_norm_w,
           wx_q, wx_kv, wx_o, norm_ffn_w, w_gate, w_up, w_down, final_norm_w):
    bsz, s, _ = x.shape
    mlen = mem.shape[1]
    depth = w_in.shape[0]
    h = x.reshape(bsz * s, D_MODEL)
    mem2d = mem.reshape(bsz * mlen, D_MODEL)
    row = lambda v: v.reshape(1, -1)
    w_all = _regroup_w_in(w_in)
    bias = jnp.concatenate([fox_f_b, ml_i_b, ml_f_b, jnp.zeros((depth, N_GATE - 3 * FOX_HEADS), F32)], axis=1)
    wo, wq, wkv, wxo = (w.astype(BF16) for w in (w_out, wx_q, wx_kv, wx_o))
    wg, wu, wd = (w.astype(BF16) for w in (w_gate, w_up, w_down))
    for l in range(depth):
        u, g_raw = _inproj(h, row(norm_mix_w[l]), w_all, l)
        g, gt = _gates(g_raw, row(bias[l]), bsz, s)
        y_fox = _fox(u, g, gt, bsz, s)
        y_sb = _sb(u, bsz, s)
        y_ml = _mlstm(u, g, gt, ml_conv_w[l], row(ml_norm_w[l]), bsz, s)
        kv = _memkv(mem2d, row(mem_norm_w[l]), wkv, l, tm=min(512, bsz * mlen))
        h = _cross(y_fox, y_sb, y_ml, wo, h, row(norm_x_w[l]), wq, kv, wxo, l, bsz, s, mlen)
        h = _ffn(h, row(norm_ffn_w[l]), wg, wu, wd, row(final_norm_w), l, final_norm=(l == depth - 1))
    return h.reshape(bsz, s, D_MODEL)
```

```python
import functools

import jax
import jax.numpy as jnp
from jax import lax
from jax.experimental import pallas as pl
from jax.experimental.pallas import tpu as pltpu

D_MODEL = 1024
HEAD_DIM = 64
FOX_HEADS = 6
SB_HEADS = 4
ML_HEADS = 6
D_FOX = FOX_HEADS * HEAD_DIM
D_SB = SB_HEADS * HEAD_DIM
D_ML = ML_HEADS * HEAD_DIM
ML_CHUNK = 128
CONV_WIDTH = 4
X_HEADS = 4
X_HEAD_DIM = 128
D_X = X_HEADS * X_HEAD_DIM
D_FF = 2816
RMS_EPS = 1e-6

LANES = 128
PAIR = 2 * HEAD_DIM
D_MAIN = 3 * D_FOX + 3 * D_SB + 4 * D_ML
N_GATE = LANES
GT_ROWS = 32
NEG = -1e30
SB_SUB = 128
SB_EXIT = 110.0
LOG2E = 1.4426950408889634
VMEM_LIMIT = 56 * 1024 * 1024

BF16 = jnp.bfloat16
F32 = jnp.float32

FOX_Q, FOX_K, FOX_V = 0, 3, 6
SB_Q, SB_K, SB_V = 9, 11, 13
G_FOX, G_MLA, G_MLB, G_MLP = 0, 6, 12, 18
EXP_B = ML_HEADS * LANES
EXP_A = EXP_B + D_ML
N_EXP = EXP_A + D_ML


def _cparams(*sem):
    return pltpu.CompilerParams(dimension_semantics=sem, vmem_limit_bytes=VMEM_LIMIT)


def _rms(xf, w):
    ms = jnp.mean(xf * xf, axis=-1, keepdims=True)
    return xf * lax.rsqrt(ms + RMS_EPS) * w


def _dot(a, b):
    return jnp.dot(a, b, preferred_element_type=F32)


def _dot_nt(a, b):
    return lax.dot_general(a, b, (((1,), (1,)), ((), ())), preferred_element_type=F32)


def _dot_tn(a, b):
    return lax.dot_general(a, b, (((0,), (0,)), ((), ())), preferred_element_type=F32)


def _log_sigmoid(x):
    return jnp.minimum(x, 0.0) - jnp.log(1.0 + jnp.exp(-jnp.abs(x)))


def _split3(x):
    x1 = x.astype(BF16)
    r1 = x - x1.astype(F32)
    x2 = r1.astype(BF16)
    return x1, x2, (r1 - x2.astype(F32)).astype(BF16)


def _dot_exact(a, b, lhs_exact):
    if lhs_exact:
        x1, x2, x3 = _split3(b)
        return _dot(a, x1) + _dot(a, x2) + _dot(a, x3)
    x1, x2, x3 = _split3(a)
    return _dot(x1, b) + _dot(x2, b) + _dot(x3, b)


def _gate_chunk(pre, carry, tri):
    L = ML_CHUNK
    col = lax.broadcasted_iota(jnp.int32, (L, LANES), 1)
    row = lax.broadcasted_iota(jnp.int32, (L, LANES), 0)
    is_a = (col >= G_MLA) & (col < G_MLB)
    cum = _dot_exact(tri, jnp.where(is_a, 0.0, _log_sigmoid(pre)), lhs_exact=True)
    tot = cum + carry
    a = pre - pltpu.roll(cum, LANES - (G_MLB - G_MLA), 1)
    pm = jnp.where(is_a, a, -jnp.inf)
    k = 1
    while k < L:
        pm = jnp.maximum(pm, jnp.where(row >= k, pltpu.roll(pm, k, 0), -jnp.inf))
        k *= 2
    pm = pltpu.roll(pm, G_MLP - G_MLA, 1)
    out = jnp.where(col < G_MLA, tot,
                    jnp.where(is_a, a, jnp.where(col < G_MLP, cum, jnp.where(col < G_MLP + ML_HEADS, pm, 0.0))))
    return out, tot[L - 1:L, :]


def _inproj_kernel(x_ref, nw_ref, w_ref, b_ref, u_ref, go_ref, gt_ref, carry_ref, *, n_chunk, steps_per_seq):
    @pl.when(pl.program_id(0) % steps_per_seq == 0)
    def _():
        carry_ref[...] = jnp.zeros_like(carry_ref)

    xn = _rms(x_ref[...], nw_ref[...]).astype(BF16)
    n_all = D_MAIN + N_GATE
    for c in reversed(range(n_all // n_chunk)):
        lo = c * n_chunk
        res = _dot(xn, w_ref[:, lo:lo + n_chunk])
        if lo + n_chunk <= D_MAIN:
            u_ref[:, lo:lo + n_chunk] = res.astype(BF16)
            continue
        u_ref[:, lo:D_MAIN] = res[:, :D_MAIN - lo].astype(BF16)
        pre_all = res[:, D_MAIN - lo:] + b_ref[...]
        L = ML_CHUNK
        r = lax.broadcasted_iota(jnp.int32, (L, L), 0)
        c = lax.broadcasted_iota(jnp.int32, (L, L), 1)
        tri = jnp.where(r >= c, 1.0, 0.0).astype(BF16)
        carry = carry_ref[...]
        for ci in range(x_ref.shape[0] // L):
            out, carry = _gate_chunk(pre_all[ci * L:(ci + 1) * L], carry, tri)
            go_ref[ci * L:(ci + 1) * L, :] = out
            gt_ref[:, ci * L:(ci + 1) * L] = out.T[:GT_ROWS, :]
        carry_ref[...] = carry


def _inproj(h, nw, w_all, bias, layer, bsz, s, tm=512):
    t = h.shape[0]
    steps = s // tm
    return pl.pallas_call(
        functools.partial(_inproj_kernel, n_chunk=512, steps_per_seq=steps),
        grid=(t // tm,),
        in_specs=[
            pl.BlockSpec((tm, D_MODEL), lambda i: (i, 0)),
            pl.BlockSpec((1, D_MODEL), lambda i: (0, 0)),
            pl.BlockSpec((None, D_MODEL, D_MAIN + N_GATE), lambda i: (layer, 0, 0)),
            pl.BlockSpec((1, N_GATE), lambda i: (0, 0)),
        ],
        out_specs=[
            pl.BlockSpec((tm, D_MAIN), lambda i: (i, 0)),
            pl.BlockSpec((tm, N_GATE), lambda i: (i, 0)),
            pl.BlockSpec((None, GT_ROWS, tm), lambda i: (i // steps, 0, i % steps)),
        ],
        out_shape=[jax.ShapeDtypeStruct((t, D_MAIN), BF16), jax.ShapeDtypeStruct((t, N_GATE), F32),
                   jax.ShapeDtypeStruct((bsz, GT_ROWS, s), F32)],
        scratch_shapes=[pltpu.VMEM((1, N_GATE), F32)],
        compiler_params=_cparams("arbitrary"),
        name="inproj",
    )(h, nw, w_all, bias)


def _head_masks(q2):
    lane = lax.broadcasted_iota(jnp.int32, q2.shape, 1)
    zero = jnp.zeros_like(q2)
    return jnp.where(lane < HEAD_DIM, q2, zero), jnp.where(lane >= HEAD_DIM, q2, zero)


def _stack_heads(q2):
    return jnp.concatenate(_head_masks(q2), axis=0)


def _unstack_heads(x, rows):
    lane = lax.broadcasted_iota(jnp.int32, (rows, LANES), 1)
    return jnp.where(lane < HEAD_DIM, x[:rows], x[rows:])


def _fox_kernel(q_ref, k_ref, v_ref, g_ref, gt_ref, o_ref, ct_ref, m_ref, acc_ref, *, tq, nq):
    p = pl.program_id(1)
    qi = pl.program_id(2)
    half = tq // 2

    def sweep(blk):
        qs = _stack_heads(q_ref[...] * jnp.asarray(HEAD_DIM ** -0.5, BF16))
        gq = g_ref[...]
        lane = lax.broadcasted_iota(jnp.int32, (tq, LANES), 1)
        for hh in range(2):
            ct = jnp.sum(jnp.where(lane == G_FOX + 2 * p + hh, gq, 0.0), axis=1, keepdims=True)
            ct_ref[hh * tq:(hh + 1) * tq, :] = jnp.broadcast_to(ct * LOG2E, (tq, LANES))
        m_ref[...] = jnp.full_like(m_ref, NEG)
        acc_ref[...] = jnp.zeros_like(acc_ref)

        def group(start, nk, row0, masked):
            nr = tq - row0
            qsub = qs if row0 == 0 else jnp.concatenate([qs[row0:tq], qs[tq + row0:]], axis=0)
            s = _dot_nt(qsub, k_ref[start:start + nk, :])
            v2 = v_ref[start:start + nk, :]
            one = jnp.ones_like(v2)
            lo_lanes = lax.broadcasted_iota(jnp.int32, (nk, LANES), 1) < HEAD_DIM
            v_ones = (jnp.where(lo_lanes, v2, one), jnp.where(lo_lanes, one, v2))
            if masked:
                rq = lax.broadcasted_iota(jnp.int32, (nr, nk), 0)
                rk = lax.broadcasted_iota(jnp.int32, (nr, nk), 1)
                causal = jnp.where(rk <= rq, 0.0, NEG)
            for hh in range(2):
                rows = slice(hh * tq + row0, (hh + 1) * tq)
                cs = gt_ref[pl.ds(G_FOX + 2 * p + hh, 1), start:start + nk] * LOG2E
                r = s[hh * nr:(hh + 1) * nr] * LOG2E - cs
                if masked:
                    r = r + causal
                ct = ct_ref[rows, :]
                m_prev = m_ref[rows, :]
                m_next = jnp.maximum(m_prev, jnp.max(r, axis=1, keepdims=True) + ct)
                pr = jnp.exp2(r - jnp.tile(m_next - ct, (1, nk // LANES)))
                alpha = jnp.exp2(m_prev - m_next)
                acc_ref[rows, :] = alpha * acc_ref[rows, :] + _dot(pr.astype(BF16), v_ones[hh])
                m_ref[rows, :] = m_next

        for j in range(blk):
            group(j * tq, tq, 0, False)
        group(blk * tq, half, 0, True)
        group(blk * tq + half, half, half, True)
        acc = acc_ref[...]
        o_ref[...] = _unstack_heads(acc / pltpu.roll(acc, HEAD_DIM, 1), tq).astype(BF16)

    for blk in range(nq):
        pl.when(qi == blk)(functools.partial(sweep, blk))


def _fox(u, g, gt, bsz, s, tq=512):
    nq = s // tq
    stat = pltpu.VMEM((2 * tq, LANES), F32)
    return pl.pallas_call(
        functools.partial(_fox_kernel, tq=tq, nq=nq),
        grid=(bsz, FOX_HEADS // 2, nq),
        in_specs=[
            pl.BlockSpec((tq, PAIR), lambda b, p, i: (b * nq + i, FOX_Q + p)),
            pl.BlockSpec((s, PAIR), lambda b, p, i: (b, FOX_K + p)),
            pl.BlockSpec((s, PAIR), lambda b, p, i: (b, FOX_V + p)),
            pl.BlockSpec((tq, N_GATE), lambda b, p, i: (b * nq + i, 0)),
            pl.BlockSpec((None, 8, s), lambda b, p, i: (b, 0, 0)),
        ],
        out_specs=pl.BlockSpec((tq, PAIR), lambda b, p, i: (b * nq + i, p)),
        out_shape=jax.ShapeDtypeStruct((bsz * s, D_FOX), BF16),
        scratch_shapes=[stat, stat, stat],
        compiler_params=_cparams("parallel", "parallel", "arbitrary"),
        name="fox",
    )(u, u, u, g, gt)


def _sb_kernel(q_ref, k_ref, v_ref, o_ref, qs_ref, run_ref, acc_ref, *, tq, tg, nq):
    qi = pl.program_id(2)
    n_diag = tq // tg

    def sweep(blk):
        qs_ref[...] = _stack_heads(q_ref[...] * jnp.asarray(HEAD_DIM ** -0.5, BF16))
        r = lax.broadcasted_iota(jnp.int32, (2 * SB_SUB, 2 * SB_SUB), 0)
        c = lax.broadcasted_iota(jnp.int32, (2 * SB_SUB, 2 * SB_SUB), 1)
        suffix = jnp.where((c >= SB_SUB) | (jnp.where(r >= SB_SUB, r - SB_SUB, r) >= c), 1.0, 0.0).astype(BF16)
        run_ref[...] = jnp.zeros_like(run_ref)
        acc_ref[...] = jnp.zeros_like(acc_ref)

        def group(g, diag):
            row0 = tq - (g + 1) * tg if diag else 0
            nr = tq - row0
            heads = lambda ref: ref[...] if row0 == 0 else jnp.concatenate([ref[row0:tq, :], ref[tq + row0:, :]], axis=0)
            start = (blk + 1) * tq - (g + 1) * tg
            if not isinstance(start, int):
                start = pl.multiple_of(start, tg)
            z = _dot_nt(heads(qs_ref), k_ref[pl.ds(start, tg), :])
            sp = jnp.maximum(z, 0.0) + jnp.log(1.0 + jnp.exp(-jnp.abs(z)))
            if diag:
                rq = lax.broadcasted_iota(jnp.int32, (2 * nr, tg), 0)
                rk = lax.broadcasted_iota(jnp.int32, (2 * nr, tg), 1)
                valid = rk < jnp.where(rq >= nr, rq - nr, rq)
                sp = jnp.where(valid, sp, 0.0)
            hi = sp.astype(BF16)
            lo = (sp - hi.astype(F32)).astype(BF16)
            offs = heads(run_ref)
            parts = [None] * (tg // SB_SUB)
            for sb in reversed(range(tg // SB_SUB)):
                sl = slice(sb * SB_SUB, (sb + 1) * SB_SUB)
                rs = _dot(jnp.concatenate([hi[:, sl], lo[:, sl]], axis=1), suffix)
                a = jnp.exp(z[:, sl] - rs[:, :SB_SUB] - offs)
                if diag:
                    a = jnp.where(valid[:, sl], a, 0.0)
                parts[sb] = a.astype(BF16)
                offs = offs + rs[:, SB_SUB:]
            pv = _dot(jnp.concatenate(parts, axis=1), v_ref[pl.ds(start, tg), :])
            for hh in range(2):
                rows = slice(hh * tq + row0, (hh + 1) * tq)
                run_ref[rows, :] = offs[hh * nr:(hh + 1) * nr]
                acc_ref[rows, :] += pv[hh * nr:(hh + 1) * nr]
            return jnp.min(offs)

        for g in range(n_diag):
            low = group(g, True)
        n_groups = (blk + 1) * n_diag
        first = n_diag
        if blk > 0:
            low = group(n_diag, False)
            first += 1
        if first < n_groups:
            def cond(carry):
                g, low = carry
                return (g < n_groups) & (low < SB_EXIT)

            def body(carry):
                g, _ = carry
                return g + 1, group(g, False)

            lax.while_loop(cond, body, (jnp.int32(first), low))
        o_ref[...] = _unstack_heads(acc_ref[...], tq).astype(BF16)

    for blk in range(nq):
        pl.when(qi == blk)(functools.partial(sweep, blk))


def _sb(u, bsz, s, tq=512, tg=256):
    nq = s // tq
    stat = pltpu.VMEM((2 * tq, LANES), F32)
    return pl.pallas_call(
        functools.partial(_sb_kernel, tq=tq, tg=tg, nq=nq),
        grid=(bsz, SB_HEADS // 2, nq),
        in_specs=[
            pl.BlockSpec((tq, PAIR), lambda b, p, i: (b * nq + i, SB_Q + p)),
            pl.BlockSpec((s, PAIR), lambda b, p, i: (b, SB_K + p)),
            pl.BlockSpec((s, PAIR), lambda b, p, i: (b, SB_V + p)),
        ],
        out_specs=pl.BlockSpec((tq, PAIR), lambda b, p, i: (b * nq + i, p)),
        out_shape=jax.ShapeDtypeStruct((bsz * s, D_SB), BF16),
        scratch_shapes=[pltpu.VMEM((2 * tq, PAIR), BF16), stat, stat],
        compiler_params=_cparams("parallel", "parallel", "arbitrary"),
        name="sb",
    )(u, u, u)


def _mlstm_kernel(xq_ref, xk_ref, v_ref, og_ref, g_ref, gt_ref, cw_ref, nw_ref, o_ref, qs_ref, ks_ref, e_ref, *, s):
    L = ML_CHUNK
    nc = s // L
    TAIL = 16
    row = lax.broadcasted_iota(jnp.int32, (TAIL, D_ML), 0)

    def conv_chunk(c, _):
        start = pl.multiple_of(c * L, L)
        prev_start = pl.multiple_of(jnp.maximum(c - 1, 0) * L, L)
        for x_ref, dst_ref, w0, scale in ((xq_ref, qs_ref, 0, 1.0), (xk_ref, ks_ref, D_ML, HEAD_DIM ** -0.5)):
            cur = x_ref[pl.ds(start, L), :].astype(F32)
            tail = x_ref[pl.ds(prev_start + L - TAIL, TAIL), :].astype(F32)
            tail = jnp.where(c > 0, tail, 0.0)
            w = cw_ref[:, w0:w0 + D_ML]
            y = cur * w[CONV_WIDTH - 1:CONV_WIDTH, :]
            for k in range(1, CONV_WIDTH):
                sh = pltpu.roll(cur, k, 0)
                head = jnp.where(row < k, pltpu.roll(tail, k, 0), sh[:TAIL])
                sh = jnp.concatenate([head, sh[TAIL:]], axis=0)
                y = y + sh * w[CONV_WIDTH - 1 - k:CONV_WIDTH - k, :]
            y = y * jax.nn.sigmoid(y) * scale
            dst_ref[pl.ds(start, L), :] = y.astype(BF16)
        return 0

    conv_chunk(0, 0)

    ci = lax.broadcasted_iota(jnp.int32, (LANES, N_EXP), 0)
    cj = lax.broadcasted_iota(jnp.int32, (LANES, N_EXP), 1)
    src_col = jnp.where(cj < EXP_B, G_MLP + lax.shift_right_logical(cj, 7),
                        jnp.where(cj < EXP_A, G_MLB + lax.shift_right_logical(cj - EXP_B, 6),
                                  G_MLA + lax.shift_right_logical(cj - EXP_A, 6)))
    e_ref[...] = jnp.where(ci == src_col, 1.0, 0.0).astype(BF16)

    lane = lax.broadcasted_iota(jnp.int32, (L, LANES), 1)
    lo_lanes = lane < HEAD_DIM
    r_ll = lax.broadcasted_iota(jnp.int32, (L, L), 0)
    c_ll = lax.broadcasted_iota(jnp.int32, (L, L), 1)
    tri = r_ll >= c_ll
    blockdiag = (r_ll < HEAD_DIM) == (c_ll < HEAD_DIM)
    ones = jnp.ones((L, LANES), BF16)

    def chunk(c, carry):
        conv_chunk(jnp.minimum(c + 1, nc - 1), 0)
        start = pl.multiple_of(c * L, L)
        rep = _dot_exact(g_ref[pl.ds(start, L), :], e_ref[...], lhs_exact=False)
        gtc = gt_ref[:, pl.ds(start, L)]
        new_carry = []
        for p in range(ML_HEADS // 2):
            c2, nm, m2 = carry[p]
            sl = slice(p * PAIR, (p + 1) * PAIR)
            q2 = qs_ref[pl.ds(start, L), sl]
            k2 = ks_ref[pl.ds(start, L), sl]
            v_one = jnp.concatenate([v_ref[pl.ds(start, L), sl], ones], axis=1)
            b2 = rep[:, EXP_B + p * PAIR:EXP_B + (p + 1) * PAIR]
            a2 = rep[:, EXP_A + p * PAIR:EXP_A + (p + 1) * PAIR]
            pm = [rep[:, (2 * p + hh) * LANES:(2 * p + hh + 1) * LANES] for hh in range(2)]
            pm2 = jnp.where(lo_lanes, pm[0], pm[1])
            qk = _dot_nt(_stack_heads(q2), k2)
            sc = []
            for hh in range(2):
                a_row = gtc[G_MLA + 2 * p + hh:G_MLA + 2 * p + hh + 1, :]
                sc.append(qk[hh * L:(hh + 1) * L] * jnp.exp(jnp.where(tri, a_row - pm[hh], -jnp.inf)))
            r = _dot(jnp.concatenate(sc, axis=0).astype(BF16), v_one)
            num_loc = jnp.where(lo_lanes, r[:L, :LANES], r[L:, :LANES])
            den_loc = jnp.where(lo_lanes, r[:L, LANES:], r[L:, LANES:])
            m_loc = b2 + pm2
            inter = _dot(q2, jnp.concatenate([c2, nm], axis=1).astype(BF16))
            it2 = b2 + m2
            m_t = jnp.maximum(it2, m_loc)
            f_loc = jnp.exp(m_loc - m_t)
            w_int = jnp.exp(it2 - m_t)
            num = num_loc * f_loc + w_int * inter[:, :LANES]
            den = den_loc * f_loc + w_int * inter[:, LANES:]
            hp = num / jnp.maximum(jnp.abs(den), jnp.exp(-m_t))
            sq = hp * hp
            ms0 = jnp.sum(jnp.where(lo_lanes, sq, 0.0), axis=1, keepdims=True)
            ms1 = jnp.sum(jnp.where(lo_lanes, 0.0, sq), axis=1, keepdims=True)
            msq = jnp.where(lo_lanes, ms0, ms1) * (1.0 / HEAD_DIM)
            hp = hp * lax.rsqrt(msq + RMS_EPS) * nw_ref[:, sl]
            gate = jax.nn.sigmoid(og_ref[pl.ds(start, L), sl].astype(F32))
            o_ref[pl.ds(start, L), sl] = (gate * hp).astype(BF16)
            g2 = b2[L - 1:L, :]
            pl2 = pm2[L - 1:L, :]
            kw = (k2.astype(F32) * jnp.exp(a2 - pl2)).astype(BF16)
            upd = _dot_tn(kw, v_one)
            m_new = jnp.maximum(g2 + m2, g2 + pl2)
            decay = jnp.exp(g2 + m2 - m_new)
            scale = jnp.exp(g2 + pl2 - m_new)
            c2 = decay * c2 + scale * jnp.where(blockdiag, upd[:, :LANES], 0.0)
            nm = decay * nm + scale * jnp.where(blockdiag, upd[:, LANES:], 0.0)
            new_carry.append((c2, nm, m_new))
        return tuple(new_carry)

    init = tuple((jnp.zeros((LANES, LANES), F32), jnp.zeros((LANES, LANES), F32), jnp.zeros((1, LANES), F32))
                 for _ in range(ML_HEADS // 2))
    lax.fori_loop(0, nc, chunk, init, unroll=4)


def _mlstm(u, g, gt, conv_w, norm_w, bsz, s):
    blk = lambda j: pl.BlockSpec((s, D_ML), lambda b: (b, j))
    first = (3 * D_FOX + 3 * D_SB) // D_ML
    return pl.pallas_call(
        functools.partial(_mlstm_kernel, s=s),
        grid=(bsz,),
        in_specs=[
            blk(first), blk(first + 1), blk(first + 2), blk(first + 3),
            pl.BlockSpec((s, N_GATE), lambda b: (b, 0)),
            pl.BlockSpec((None, GT_ROWS, s), lambda b: (b, 0, 0)),
            pl.BlockSpec((CONV_WIDTH, 2 * D_ML), lambda b: (0, 0)),
            pl.BlockSpec((1, D_ML), lambda b: (0, 0)),
        ],
        out_specs=pl.BlockSpec((s, D_ML), lambda b: (b, 0)),
        out_shape=jax.ShapeDtypeStruct((bsz * s, D_ML), BF16),
        scratch_shapes=[pltpu.VMEM((s, D_ML), BF16), pltpu.VMEM((s, D_ML), BF16), pltpu.VMEM((LANES, N_EXP), BF16)],
        compiler_params=_cparams("parallel"),
        name="mlstm",
    )(u, u, u, u, g, gt, conv_w, norm_w)


def _memkv_kernel(x_ref, nw_ref, w_ref, o_ref):
    xn = _rms(x_ref[...], nw_ref[...]).astype(BF16)
    o_ref[...] = _dot(xn, w_ref[...]).astype(BF16)


def _memkv(mem2d, nw, w, tm=512):
    t = mem2d.shape[0]
    depth = w.shape[0]
    return pl.pallas_call(
        _memkv_kernel,
        grid=(depth, t // tm),
        in_specs=[
            pl.BlockSpec((tm, D_MODEL), lambda l, i: (i, 0)),
            pl.BlockSpec((None, 1, D_MODEL), lambda l, i: (l, 0, 0)),
            pl.BlockSpec((None, D_MODEL, 2 * D_X), lambda l, i: (l, 0, 0)),
        ],
        out_specs=pl.BlockSpec((None, tm, 2 * D_X), lambda l, i: (l, i, 0)),
        out_shape=jax.ShapeDtypeStruct((depth, t, 2 * D_X), BF16),
        compiler_params=_cparams("parallel", "parallel"),
        name="memkv",
    )(mem2d, nw, w)


def _cross_kernel(yf_ref, ys_ref, ym_ref, wo_ref, h_ref, nw_ref, wq_ref, kv_ref, wxo_ref, o_ref):
    hf = (h_ref[...] + _dot(yf_ref[...], wo_ref[:D_FOX, :]) + _dot(ys_ref[...], wo_ref[D_FOX:D_FOX + D_SB, :])
          + _dot(ym_ref[...], wo_ref[D_FOX + D_SB:, :]))
    hn = _rms(hf, nw_ref[...]).astype(BF16)
    q = _dot(hn, wq_ref[...]).astype(BF16)
    outs = []
    for hd in range(X_HEADS):
        sl = slice(hd * X_HEAD_DIM, (hd + 1) * X_HEAD_DIM)
        k = kv_ref[:, sl]
        v = kv_ref[:, D_X + hd * X_HEAD_DIM:D_X + (hd + 1) * X_HEAD_DIM]
        s = _dot_nt(q[:, sl], k) * (X_HEAD_DIM ** -0.5)
        m = jnp.max(s, axis=1, keepdims=True)
        e = jnp.exp(s - m)
        pr = e / jnp.sum(e, axis=1, keepdims=True)
        outs.append(_dot(pr.astype(BF16), v).astype(BF16))
    o = jnp.concatenate(outs, axis=1)
    o_ref[...] = hf + _dot(o, wxo_ref[...])


def _cross(yf, ys, ym, wo, h, nw, wq, kv, wxo, layer, bsz, s, mlen, tm=512):
    nt = s // tm
    row = lambda n: pl.BlockSpec((tm, n), lambda b, i: (b * nt + i, 0))
    wgt = lambda r, c: pl.BlockSpec((None, r, c), lambda b, i: (layer, 0, 0))
    return pl.pallas_call(
        _cross_kernel,
        grid=(bsz, nt),
        in_specs=[
            row(D_FOX), row(D_SB), row(D_ML), wgt(D_MODEL, D_MODEL), row(D_MODEL),
            pl.BlockSpec((1, D_MODEL), lambda b, i: (0, 0)),
            wgt(D_MODEL, D_X),
            pl.BlockSpec((None, mlen, 2 * D_X), lambda b, i: (layer, b, 0)),
            wgt(D_X, D_MODEL),
        ],
        out_specs=row(D_MODEL),
        out_shape=jax.ShapeDtypeStruct((bsz * s, D_MODEL), F32),
        compiler_params=_cparams("parallel", "parallel"),
        name="cross",
    )(yf, ys, ym, wo, h, nw, wq, kv, wxo)


def _ffn_kernel(h_ref, nw_ref, wg_ref, wu_ref, wd_ref, fw_ref, o_ref, acc_ref, *, f_chunk, final_norm):
    hf = h_ref[...]
    hn = _rms(hf, nw_ref[...]).astype(BF16)
    for c in range(D_FF // f_chunk):
        sl = slice(c * f_chunk, (c + 1) * f_chunk)
        g = _dot(hn, wg_ref[:, sl])
        u = _dot(hn, wu_ref[:, sl])
        a = (g * jax.nn.sigmoid(g) * u).astype(BF16)
        part = _dot(a, wd_ref[sl, :])
        if c == 0:
            acc_ref[...] = part
        else:
            acc_ref[...] += part
    out = hf + acc_ref[...]
    if final_norm:
        out = _rms(out, fw_ref[...])
    o_ref[...] = out


def _ffn(h, nw, wg, wu, wd, fw, layer, final_norm, tm=512):
    t = h.shape[0]
    const = lambda shape: pl.BlockSpec(shape, lambda i: (0, 0))
    wgt = lambda r, c: pl.BlockSpec((None, r, c), lambda i: (layer, 0, 0))
    return pl.pallas_call(
        functools.partial(_ffn_kernel, f_chunk=256, final_norm=final_norm),
        grid=(t // tm,),
        in_specs=[
            pl.BlockSpec((tm, D_MODEL), lambda i: (i, 0)),
            const((1, D_MODEL)),
            wgt(D_MODEL, D_FF), wgt(D_MODEL, D_FF), wgt(D_FF, D_MODEL),
            const((1, D_MODEL)),
        ],
        out_specs=pl.BlockSpec((tm, D_MODEL), lambda i: (i, 0)),
        out_shape=jax.ShapeDtypeStruct((t, D_MODEL), F32),
        scratch_shapes=[pltpu.VMEM((tm, D_MODEL), F32)],
        compiler_params=_cparams("parallel"),
        name="ffn",
    )(h, nw, wg, wu, wd, fw)


N_FOX_QKV = 3 * D_FOX
N_REST = 3 * D_SB + 4 * D_ML


N_USED_GATES = FOX_HEADS + 2 * ML_HEADS


def _regroup_kernel(w_ref, o_ref):
    o_ref[:, :N_FOX_QKV] = w_ref[:, :N_FOX_QKV].astype(BF16)
    rest = w_ref[:, N_FOX_QKV + FOX_HEADS:N_FOX_QKV + FOX_HEADS + N_REST]
    o_ref[:, N_FOX_QKV:D_MAIN] = rest.astype(BF16)
    lane = lax.broadcasted_iota(jnp.int32, (w_ref.shape[0], N_USED_GATES), 1)
    gates = jnp.where(lane < FOX_HEADS, w_ref[:, N_FOX_QKV:N_FOX_QKV + N_USED_GATES], w_ref[:, D_MAIN:D_MAIN + N_USED_GATES])
    o_ref[:, D_MAIN:] = jnp.zeros((w_ref.shape[0], N_GATE), BF16)
    o_ref[:, D_MAIN:D_MAIN + N_USED_GATES] = gates.astype(BF16)


def _regroup_w_in(w, tr=256):
    depth, rows, n_in = w.shape
    assert N_FOX_QKV + N_REST == D_MAIN and D_MAIN % LANES == 0 and n_in == D_MAIN + N_USED_GATES
    return pl.pallas_call(
        _regroup_kernel,
        grid=(depth, rows // tr),
        in_specs=[pl.BlockSpec((None, tr, n_in), lambda l, i: (l, i, 0))],
        out_specs=pl.BlockSpec((None, tr, D_MAIN + N_GATE), lambda l, i: (l, i, 0)),
        out_shape=jax.ShapeDtypeStruct((depth, rows, D_MAIN + N_GATE), BF16),
        compiler_params=_cparams("parallel", "parallel"),
        name="regroup",
    )(w)


def kernel(x, mem, norm_mix_w, w_in, fox_f_b, ml_conv_w, ml_i_b, ml_f_b, ml_norm_w, w_out, norm_x_w, mem_norm_w,
           wx_q, wx_kv, wx_o, norm_ffn_w, w_gate, w_up, w_down, final_norm_w):
    bsz, s, _ = x.shape
    mlen = mem.shape[1]
    depth = w_in.shape[0]
    h = x.reshape(bsz * s, D_MODEL)
    mem2d = mem.reshape(bsz * mlen, D_MODEL)
    row = lambda v: v.reshape(1, -1)
    w_all = _regroup_w_in(w_in)
    bias = jnp.concatenate([fox_f_b, ml_i_b, ml_f_b, jnp.zeros((depth, N_GATE - 3 * FOX_HEADS), F32)], axis=1)
    wo, wq, wkv, wxo = (w.astype(BF16) for w in (w_out, wx_q, wx_kv, wx_o))
    wg, wu, wd = (w.astype(BF16) for w in (w_gate, w_up, w_down))
    kv = _memkv(mem2d, mem_norm_w.reshape(depth, 1, D_MODEL), wkv, tm=min(512, bsz * mlen))
    for l in range(depth):
        u, g, gt = _inproj(h, row(norm_mix_w[l]), w_all, row(bias[l]), l, bsz, s)
        y_fox = _fox(u, g, gt, bsz, s)
        y_sb = _sb(u, bsz, s)
        y_ml = _mlstm(u, g, gt, ml_conv_w[l], row(ml_norm_w[l]), bsz, s)
        h = _cross(y_fox, y_sb, y_ml, wo, h, row(norm_x_w[l]), wq, kv, wxo, l, bsz, s, mlen)
        h = _ffn(h, row(norm_ffn_w[l]), wg, wu, wd, row(final_norm_w), l, final_norm=(l == depth - 1))
    return h.reshape(bsz, s, D_MODEL)
```

```python
import functools

import jax
import jax.numpy as jnp
from jax import lax
from jax.experimental import pallas as pl
from jax.experimental.pallas import tpu as pltpu

D_MODEL = 1024
HEAD_DIM = 64
FOX_HEADS = 6
SB_HEADS = 4
ML_HEADS = 6
D_FOX = FOX_HEADS * HEAD_DIM
D_SB = SB_HEADS * HEAD_DIM
D_ML = ML_HEADS * HEAD_DIM
ML_CHUNK = 128
CONV_WIDTH = 4
X_HEADS = 4
X_HEAD_DIM = 128
D_X = X_HEADS * X_HEAD_DIM
D_FF = 2816
RMS_EPS = 1e-6

LANES = 128
PAIR = 2 * HEAD_DIM
D_MAIN = 3 * D_FOX + 3 * D_SB + 4 * D_ML
N_GATE = LANES
GT_ROWS = 32
NEG = -1e30
SB_SUB = 128
SB_EXIT = 110.0
LOG2E = 1.4426950408889634
VMEM_LIMIT = 56 * 1024 * 1024

BF16 = jnp.bfloat16
F32 = jnp.float32

FOX_Q, FOX_K, FOX_V = 0, 3, 6
SB_Q, SB_K, SB_V = 9, 11, 13
G_FOX, G_MLA, G_MLB, G_MLP = 0, 6, 12, 18
EXP_B = ML_HEADS * LANES
EXP_A = EXP_B + D_ML
N_EXP = EXP_A + D_ML


def _cparams(*sem):
    return pltpu.CompilerParams(dimension_semantics=sem, vmem_limit_bytes=VMEM_LIMIT)


def _rms(xf, w):
    ms = jnp.mean(xf * xf, axis=-1, keepdims=True)
    return xf * lax.rsqrt(ms + RMS_EPS) * w


def _dot(a, b):
    return jnp.dot(a, b, preferred_element_type=F32)


def _dot_nt(a, b):
    return lax.dot_general(a, b, (((1,), (1,)), ((), ())), preferred_element_type=F32)


def _dot_tn(a, b):
    return lax.dot_general(a, b, (((0,), (0,)), ((), ())), preferred_element_type=F32)


def _log_sigmoid(x):
    return jnp.minimum(x, 0.0) - jnp.log(1.0 + jnp.exp(-jnp.abs(x)))


def _split3(x):
    x1 = x.astype(BF16)
    r1 = x - x1.astype(F32)
    x2 = r1.astype(BF16)
    return x1, x2, (r1 - x2.astype(F32)).astype(BF16)


def _dot_exact(a, b, lhs_exact):
    if lhs_exact:
        x1, x2, x3 = _split3(b)
        return _dot(a, x1) + _dot(a, x2) + _dot(a, x3)
    x1, x2, x3 = _split3(a)
    return _dot(x1, b) + _dot(x2, b) + _dot(x3, b)


def _gate_chunk(pre, carry, tri):
    L = ML_CHUNK
    col = lax.broadcasted_iota(jnp.int32, (L, LANES), 1)
    row = lax.broadcasted_iota(jnp.int32, (L, LANES), 0)
    is_a = (col >= G_MLA) & (col < G_MLB)
    cum = _dot_exact(tri, jnp.where(is_a, 0.0, _log_sigmoid(pre)), lhs_exact=True)
    tot = cum + carry
    a = pre - pltpu.roll(cum, LANES - (G_MLB - G_MLA), 1)
    pm = jnp.where(is_a, a, -jnp.inf)
    k = 1
    while k < L:
        pm = jnp.maximum(pm, jnp.where(row >= k, pltpu.roll(pm, k, 0), -jnp.inf))
        k *= 2
    pm = pltpu.roll(pm, G_MLP - G_MLA, 1)
    out = jnp.where(col < G_MLA, tot,
                    jnp.where(is_a, a, jnp.where(col < G_MLP, cum, jnp.where(col < G_MLP + ML_HEADS, pm, 0.0))))
    return out, tot[L - 1:L, :]


def _inproj_kernel(x_ref, nw_ref, w_ref, b_ref, u_ref, go_ref, gt_ref, carry_ref, *, n_chunk, steps_per_seq):
    @pl.when(pl.program_id(0) % steps_per_seq == 0)
    def _():
        carry_ref[...] = jnp.zeros_like(carry_ref)

    xn = _rms(x_ref[...], nw_ref[...]).astype(BF16)
    n_all = D_MAIN + N_GATE
    for c in reversed(range(n_all // n_chunk)):
        lo = c * n_chunk
        res = _dot(xn, w_ref[:, lo:lo + n_chunk])
        if lo + n_chunk <= D_MAIN:
            u_ref[:, lo:lo + n_chunk] = res.astype(BF16)
            continue
        u_ref[:, lo:D_MAIN] = res[:, :D_MAIN - lo].astype(BF16)
        pre_all = res[:, D_MAIN - lo:] + b_ref[...]
        L = ML_CHUNK
        r = lax.broadcasted_iota(jnp.int32, (L, L), 0)
        c = lax.broadcasted_iota(jnp.int32, (L, L), 1)
        tri = jnp.where(r >= c, 1.0, 0.0).astype(BF16)
        carry = carry_ref[...]
        for ci in range(x_ref.shape[0] // L):
            out, carry = _gate_chunk(pre_all[ci * L:(ci + 1) * L], carry, tri)
            go_ref[ci * L:(ci + 1) * L, :] = out
            gt_ref[:, ci * L:(ci + 1) * L] = out.T[:GT_ROWS, :]
        carry_ref[...] = carry


def _inproj(h, nw, w_all, bias, layer, bsz, s, tm=1024):
    t = h.shape[0]
    steps = s // tm
    return pl.pallas_call(
        functools.partial(_inproj_kernel, n_chunk=512, steps_per_seq=steps),
        grid=(t // tm,),
        in_specs=[
            pl.BlockSpec((tm, D_MODEL), lambda i: (i, 0)),
            pl.BlockSpec((1, D_MODEL), lambda i: (0, 0)),
            pl.BlockSpec((None, D_MODEL, D_MAIN + N_GATE), lambda i: (layer, 0, 0)),
            pl.BlockSpec((1, N_GATE), lambda i: (0, 0)),
        ],
        out_specs=[
            pl.BlockSpec((tm, D_MAIN), lambda i: (i, 0)),
            pl.BlockSpec((tm, N_GATE), lambda i: (i, 0)),
            pl.BlockSpec((None, GT_ROWS, tm), lambda i: (i // steps, 0, i % steps)),
        ],
        out_shape=[jax.ShapeDtypeStruct((t, D_MAIN), BF16), jax.ShapeDtypeStruct((t, N_GATE), F32),
                   jax.ShapeDtypeStruct((bsz, GT_ROWS, s), F32)],
        scratch_shapes=[pltpu.VMEM((1, N_GATE), F32)],
        compiler_params=_cparams("arbitrary"),
        name="inproj",
    )(h, nw, w_all, bias)


def _head_masks(q2):
    lane = lax.broadcasted_iota(jnp.int32, q2.shape, 1)
    zero = jnp.zeros_like(q2)
    return jnp.where(lane < HEAD_DIM, q2, zero), jnp.where(lane >= HEAD_DIM, q2, zero)


def _stack_heads(q2):
    return jnp.concatenate(_head_masks(q2), axis=0)


def _unstack_heads(x, rows):
    lane = lax.broadcasted_iota(jnp.int32, (rows, LANES), 1)
    return jnp.where(lane < HEAD_DIM, x[:rows], x[rows:])


def _fox_kernel(q_ref, k_ref, v_ref, g_ref, gt_ref, o_ref, ct_ref, m_ref, acc_ref, *, tq, nq):
    p = pl.program_id(1)
    qi = pl.program_id(2)
    half = tq // 2

    def sweep(blk):
        qs = _stack_heads(q_ref[...] * jnp.asarray(HEAD_DIM ** -0.5, BF16))
        gq = g_ref[...]
        lane = lax.broadcasted_iota(jnp.int32, (tq, LANES), 1)
        for hh in range(2):
            ct = jnp.sum(jnp.where(lane == G_FOX + 2 * p + hh, gq, 0.0), axis=1, keepdims=True)
            ct_ref[hh * tq:(hh + 1) * tq, :] = jnp.broadcast_to(ct * LOG2E, (tq, LANES))
        m_ref[...] = jnp.full_like(m_ref, NEG)
        acc_ref[...] = jnp.zeros_like(acc_ref)

        def group(start, nk, row0, masked):
            nr = tq - row0
            qsub = qs if row0 == 0 else jnp.concatenate([qs[row0:tq], qs[tq + row0:]], axis=0)
            s = _dot_nt(qsub, k_ref[start:start + nk, :])
            v2 = v_ref[start:start + nk, :]
            one = jnp.ones_like(v2)
            lo_lanes = lax.broadcasted_iota(jnp.int32, (nk, LANES), 1) < HEAD_DIM
            v_ones = (jnp.where(lo_lanes, v2, one), jnp.where(lo_lanes, one, v2))
            if masked:
                rq = lax.broadcasted_iota(jnp.int32, (nr, nk), 0)
                rk = lax.broadcasted_iota(jnp.int32, (nr, nk), 1)
                causal = jnp.where(rk <= rq, 0.0, NEG)
            for hh in range(2):
                rows = slice(hh * tq + row0, (hh + 1) * tq)
                cs = gt_ref[pl.ds(G_FOX + 2 * p + hh, 1), start:start + nk] * LOG2E
                r = s[hh * nr:(hh + 1) * nr] * LOG2E - cs
                if masked:
                    r = r + causal
                ct = ct_ref[rows, :]
                m_prev = m_ref[rows, :]
                m_next = jnp.maximum(m_prev, jnp.max(r, axis=1, keepdims=True) + ct)
                pr = jnp.exp2(r - jnp.tile(m_next - ct, (1, nk // LANES)))
                alpha = jnp.exp2(m_prev - m_next)
                acc_ref[rows, :] = alpha * acc_ref[rows, :] + _dot(pr.astype(BF16), v_ones[hh])
                m_ref[rows, :] = m_next

        for j in range(blk):
            group(j * tq, tq, 0, False)
        group(blk * tq, half, 0, True)
        group(blk * tq + half, half, half, True)
        acc = acc_ref[...]
        o_ref[...] = _unstack_heads(acc / pltpu.roll(acc, HEAD_DIM, 1), tq).astype(BF16)

    for blk in range(nq):
        pl.when(qi == blk)(functools.partial(sweep, blk))


def _fox(u, g, gt, bsz, s, tq=512):
    nq = s // tq
    stat = pltpu.VMEM((2 * tq, LANES), F32)
    return pl.pallas_call(
        functools.partial(_fox_kernel, tq=tq, nq=nq),
        grid=(bsz, FOX_HEADS // 2, nq),
        in_specs=[
            pl.BlockSpec((tq, PAIR), lambda b, p, i: (b * nq + i, FOX_Q + p)),
            pl.BlockSpec((s, PAIR), lambda b, p, i: (b, FOX_K + p)),
            pl.BlockSpec((s, PAIR), lambda b, p, i: (b, FOX_V + p)),
            pl.BlockSpec((tq, N_GATE), lambda b, p, i: (b * nq + i, 0)),
            pl.BlockSpec((None, 8, s), lambda b, p, i: (b, 0, 0)),
        ],
        out_specs=pl.BlockSpec((tq, PAIR), lambda b, p, i: (b * nq + i, p)),
        out_shape=jax.ShapeDtypeStruct((bsz * s, D_FOX), BF16),
        scratch_shapes=[stat, stat, stat],
        compiler_params=_cparams("parallel", "parallel", "arbitrary"),
        name="fox",
    )(u, u, u, g, gt)


def _sb_kernel(q_ref, k_ref, v_ref, o_ref, qs_ref, run_ref, acc_ref, *, tq, tg, nq):
    qi = pl.program_id(2)
    n_diag = tq // tg

    def sweep(blk):
        qs_ref[...] = _stack_heads(q_ref[...] * jnp.asarray(HEAD_DIM ** -0.5, BF16))
        r = lax.broadcasted_iota(jnp.int32, (2 * SB_SUB, 2 * SB_SUB), 0)
        c = lax.broadcasted_iota(jnp.int32, (2 * SB_SUB, 2 * SB_SUB), 1)
        suffix = jnp.where((c >= SB_SUB) | (jnp.where(r >= SB_SUB, r - SB_SUB, r) >= c), 1.0, 0.0).astype(BF16)
        run_ref[...] = jnp.zeros_like(run_ref)
        acc_ref[...] = jnp.zeros_like(acc_ref)

        def group(g, diag):
            row0 = tq - (g + 1) * tg if diag else 0
            nr = tq - row0
            heads = lambda ref: ref[...] if row0 == 0 else jnp.concatenate([ref[row0:tq, :], ref[tq + row0:, :]], axis=0)
            start = (blk + 1) * tq - (g + 1) * tg
            if not isinstance(start, int):
                start = pl.multiple_of(start, tg)
            z = _dot_nt(heads(qs_ref), k_ref[pl.ds(start, tg), :])
            sp = jnp.maximum(z, 0.0) + jnp.log(1.0 + jnp.exp(-jnp.abs(z)))
            if diag:
                rq = lax.broadcasted_iota(jnp.int32, (2 * nr, tg), 0)
                rk = lax.broadcasted_iota(jnp.int32, (2 * nr, tg), 1)
                valid = rk < jnp.where(rq >= nr, rq - nr, rq)
                sp = jnp.where(valid, sp, 0.0)
            hi = sp.astype(BF16)
            lo = (sp - hi.astype(F32)).astype(BF16)
            offs = heads(run_ref)
            parts = [None] * (tg // SB_SUB)
            for sb in reversed(range(tg // SB_SUB)):
                sl = slice(sb * SB_SUB, (sb + 1) * SB_SUB)
                rs = _dot(jnp.concatenate([hi[:, sl], lo[:, sl]], axis=1), suffix)
                a = jnp.exp(z[:, sl] - rs[:, :SB_SUB] - offs)
                if diag:
                    a = jnp.where(valid[:, sl], a, 0.0)
                parts[sb] = a.astype(BF16)
                offs = offs + rs[:, SB_SUB:]
            pv = _dot(jnp.concatenate(parts, axis=1), v_ref[pl.ds(start, tg), :])
            for hh in range(2):
                rows = slice(hh * tq + row0, (hh + 1) * tq)
                run_ref[rows, :] = offs[hh * nr:(hh + 1) * nr]
                acc_ref[rows, :] += pv[hh * nr:(hh + 1) * nr]
            return jnp.min(offs)

        for g in range(n_diag):
            low = group(g, True)
        n_groups = (blk + 1) * n_diag
        first = n_diag
        if blk > 0:
            low = group(n_diag, False)
            first += 1
        if first < n_groups:
            def cond(carry):
                g, low = carry
                return (g < n_groups) & (low < SB_EXIT)

            def body(carry):
                g, _ = carry
                return g + 1, group(g, False)

            lax.while_loop(cond, body, (jnp.int32(first), low))
        o_ref[...] = _unstack_heads(acc_ref[...], tq).astype(BF16)

    for blk in range(nq):
        pl.when(qi == blk)(functools.partial(sweep, blk))


def _sb(u, bsz, s, tq=512, tg=256):
    nq = s // tq
    stat = pltpu.VMEM((2 * tq, LANES), F32)
    return pl.pallas_call(
        functools.partial(_sb_kernel, tq=tq, tg=tg, nq=nq),
        grid=(bsz, SB_HEADS // 2, nq),
        in_specs=[
            pl.BlockSpec((tq, PAIR), lambda b, p, i: (b * nq + i, SB_Q + p)),
            pl.BlockSpec((s, PAIR), lambda b, p, i: (b, SB_K + p)),
            pl.BlockSpec((s, PAIR), lambda b, p, i: (b, SB_V + p)),
        ],
        out_specs=pl.BlockSpec((tq, PAIR), lambda b, p, i: (b * nq + i, p)),
        out_shape=jax.ShapeDtypeStruct((bsz * s, D_SB), BF16),
        scratch_shapes=[pltpu.VMEM((2 * tq, PAIR), BF16), stat, stat],
        compiler_params=_cparams("parallel", "parallel", "arbitrary"),
        name="sb",
    )(u, u, u)


def _mlstm_kernel(xq_ref, xk_ref, v_ref, og_ref, g_ref, gt_ref, cw_ref, nw_ref, o_ref, qs_ref, ks_ref, e_ref, *, s):
    L = ML_CHUNK
    nc = s // L
    TAIL = 16
    row = lax.broadcasted_iota(jnp.int32, (TAIL, D_ML), 0)

    def conv_chunk(c, _):
        start = pl.multiple_of(c * L, L)
        prev_start = pl.multiple_of(jnp.maximum(c - 1, 0) * L, L)
        for x_ref, dst_ref, w0, scale in ((xq_ref, qs_ref, 0, 1.0), (xk_ref, ks_ref, D_ML, HEAD_DIM ** -0.5)):
            cur = x_ref[pl.ds(start, L), :].astype(F32)
            tail = x_ref[pl.ds(prev_start + L - TAIL, TAIL), :].astype(F32)
            tail = jnp.where(c > 0, tail, 0.0)
            w = cw_ref[:, w0:w0 + D_ML]
            y = cur * w[CONV_WIDTH - 1:CONV_WIDTH, :]
            for k in range(1, CONV_WIDTH):
                sh = pltpu.roll(cur, k, 0)
                head = jnp.where(row < k, pltpu.roll(tail, k, 0), sh[:TAIL])
                sh = jnp.concatenate([head, sh[TAIL:]], axis=0)
                y = y + sh * w[CONV_WIDTH - 1 - k:CONV_WIDTH - k, :]
            y = y * jax.nn.sigmoid(y) * scale
            dst_ref[pl.ds(start, L), :] = y.astype(BF16)
        return 0

    conv_chunk(0, 0)

    ci = lax.broadcasted_iota(jnp.int32, (LANES, N_EXP), 0)
    cj = lax.broadcasted_iota(jnp.int32, (LANES, N_EXP), 1)
    src_col = jnp.where(cj < EXP_B, G_MLP + lax.shift_right_logical(cj, 7),
                        jnp.where(cj < EXP_A, G_MLB + lax.shift_right_logical(cj - EXP_B, 6),
                                  G_MLA + lax.shift_right_logical(cj - EXP_A, 6)))
    e_ref[...] = jnp.where(ci == src_col, 1.0, 0.0).astype(BF16)

    lane = lax.broadcasted_iota(jnp.int32, (L, LANES), 1)
    lo_lanes = lane < HEAD_DIM
    r_ll = lax.broadcasted_iota(jnp.int32, (L, L), 0)
    c_ll = lax.broadcasted_iota(jnp.int32, (L, L), 1)
    tri = r_ll >= c_ll
    blockdiag = (r_ll < HEAD_DIM) == (c_ll < HEAD_DIM)
    ones = jnp.ones((L, LANES), BF16)

    def chunk(c, carry):
        conv_chunk(jnp.minimum(c + 1, nc - 1), 0)
        start = pl.multiple_of(c * L, L)
        rep = _dot_exact(g_ref[pl.ds(start, L), :], e_ref[...], lhs_exact=False)
        gtc = gt_ref[:, pl.ds(start, L)]
        new_carry = []
        for p in range(ML_HEADS // 2):
            c2, nm, m2 = carry[p]
            sl = slice(p * PAIR, (p + 1) * PAIR)
            q2 = qs_ref[pl.ds(start, L), sl]
            k2 = ks_ref[pl.ds(start, L), sl]
            v_one = jnp.concatenate([v_ref[pl.ds(start, L), sl], ones], axis=1)
            b2 = rep[:, EXP_B + p * PAIR:EXP_B + (p + 1) * PAIR]
            a2 = rep[:, EXP_A + p * PAIR:EXP_A + (p + 1) * PAIR]
            pm = [rep[:, (2 * p + hh) * LANES:(2 * p + hh + 1) * LANES] for hh in range(2)]
            pm2 = jnp.where(lo_lanes, pm[0], pm[1])
            qk = _dot_nt(_stack_heads(q2), k2)
            sc = []
            for hh in range(2):
                a_row = gtc[G_MLA + 2 * p + hh:G_MLA + 2 * p + hh + 1, :]
                sc.append(qk[hh * L:(hh + 1) * L] * jnp.exp(jnp.where(tri, a_row - pm[hh], -jnp.inf)))
            r = _dot(jnp.concatenate(sc, axis=0).astype(BF16), v_one)
            num_loc = jnp.where(lo_lanes, r[:L, :LANES], r[L:, :LANES])
            den_loc = jnp.where(lo_lanes, r[:L, LANES:], r[L:, LANES:])
            m_loc = b2 + pm2
            inter = _dot(q2, jnp.concatenate([c2, nm], axis=1).astype(BF16))
            it2 = b2 + m2
            m_t = jnp.maximum(it2, m_loc)
            f_loc = jnp.exp(m_loc - m_t)
            w_int = jnp.exp(it2 - m_t)
            num = num_loc * f_loc + w_int * inter[:, :LANES]
            den = den_loc * f_loc + w_int * inter[:, LANES:]
            hp = num / jnp.maximum(jnp.abs(den), jnp.exp(-m_t))
            sq = hp * hp
            ms0 = jnp.sum(jnp.where(lo_lanes, sq, 0.0), axis=1, keepdims=True)
            ms1 = jnp.sum(jnp.where(lo_lanes, 0.0, sq), axis=1, keepdims=True)
            msq = jnp.where(lo_lanes, ms0, ms1) * (1.0 / HEAD_DIM)
            hp = hp * lax.rsqrt(msq + RMS_EPS) * nw_ref[:, sl]
            gate = jax.nn.sigmoid(og_ref[pl.ds(start, L), sl].astype(F32))
            o_ref[pl.ds(start, L), sl] = (gate * hp).astype(BF16)
            g2 = b2[L - 1:L, :]
            pl2 = pm2[L - 1:L, :]
            kw = (k2.astype(F32) * jnp.exp(a2 - pl2)).astype(BF16)
            upd = _dot_tn(kw, v_one)
            m_new = jnp.maximum(g2 + m2, g2 + pl2)
            decay = jnp.exp(g2 + m2 - m_new)
            scale = jnp.exp(g2 + pl2 - m_new)
            c2 = decay * c2 + scale * jnp.where(blockdiag, upd[:, :LANES], 0.0)
            nm = decay * nm + scale * jnp.where(blockdiag, upd[:, LANES:], 0.0)
            new_carry.append((c2, nm, m_new))
        return tuple(new_carry)

    init = tuple((jnp.zeros((LANES, LANES), F32), jnp.zeros((LANES, LANES), F32), jnp.zeros((1, LANES), F32))
                 for _ in range(ML_HEADS // 2))
    lax.fori_loop(0, nc, chunk, init, unroll=4)


def _mlstm(u, g, gt, conv_w, norm_w, bsz, s):
    blk = lambda j: pl.BlockSpec((s, D_ML), lambda b: (b, j))
    first = (3 * D_FOX + 3 * D_SB) // D_ML
    return pl.pallas_call(
        functools.partial(_mlstm_kernel, s=s),
        grid=(bsz,),
        in_specs=[
            blk(first), blk(first + 1), blk(first + 2), blk(first + 3),
            pl.BlockSpec((s, N_GATE), lambda b: (b, 0)),
            pl.BlockSpec((None, GT_ROWS, s), lambda b: (b, 0, 0)),
            pl.BlockSpec((CONV_WIDTH, 2 * D_ML), lambda b: (0, 0)),
            pl.BlockSpec((1, D_ML), lambda b: (0, 0)),
        ],
        out_specs=pl.BlockSpec((s, D_ML), lambda b: (b, 0)),
        out_shape=jax.ShapeDtypeStruct((bsz * s, D_ML), BF16),
        scratch_shapes=[pltpu.VMEM((s, D_ML), BF16), pltpu.VMEM((s, D_ML), BF16), pltpu.VMEM((LANES, N_EXP), BF16)],
        compiler_params=_cparams("parallel"),
        name="mlstm",
    )(u, u, u, u, g, gt, conv_w, norm_w)


def _memkv_kernel(x_ref, nw_ref, w_ref, o_ref):
    xn = _rms(x_ref[...], nw_ref[...]).astype(BF16)
    o_ref[...] = _dot(xn, w_ref[...]).astype(BF16)


def _memkv(mem2d, nw, w, tm=512):
    t = mem2d.shape[0]
    depth = w.shape[0]
    return pl.pallas_call(
        _memkv_kernel,
        grid=(depth, t // tm),
        in_specs=[
            pl.BlockSpec((tm, D_MODEL), lambda l, i: (i, 0)),
            pl.BlockSpec((None, 1, D_MODEL), lambda l, i: (l, 0, 0)),
            pl.BlockSpec((None, D_MODEL, 2 * D_X), lambda l, i: (l, 0, 0)),
        ],
        out_specs=pl.BlockSpec((None, tm, 2 * D_X), lambda l, i: (l, i, 0)),
        out_shape=jax.ShapeDtypeStruct((depth, t, 2 * D_X), BF16),
        compiler_params=_cparams("parallel", "parallel"),
        name="memkv",
    )(mem2d, nw, w)


def _cross_kernel(yf_ref, ys_ref, ym_ref, wo_ref, h_ref, nw_ref, wq_ref, kv_ref, wxo_ref, o_ref):
    hf = (h_ref[...] + _dot(yf_ref[...], wo_ref[:D_FOX, :]) + _dot(ys_ref[...], wo_ref[D_FOX:D_FOX + D_SB, :])
          + _dot(ym_ref[...], wo_ref[D_FOX + D_SB:, :]))
    hn = _rms(hf, nw_ref[...]).astype(BF16)
    q = _dot(hn, wq_ref[...]).astype(BF16)
    outs = []
    for hd in range(X_HEADS):
        sl = slice(hd * X_HEAD_DIM, (hd + 1) * X_HEAD_DIM)
        k = kv_ref[:, sl]
        v = kv_ref[:, D_X + hd * X_HEAD_DIM:D_X + (hd + 1) * X_HEAD_DIM]
        s = _dot_nt(q[:, sl], k) * (X_HEAD_DIM ** -0.5)
        m = jnp.max(s, axis=1, keepdims=True)
        e = jnp.exp(s - m)
        pr = e / jnp.sum(e, axis=1, keepdims=True)
        outs.append(_dot(pr.astype(BF16), v).astype(BF16))
    o = jnp.concatenate(outs, axis=1)
    o_ref[...] = hf + _dot(o, wxo_ref[...])


def _cross(yf, ys, ym, wo, h, nw, wq, kv, wxo, layer, bsz, s, mlen, tm=1024):
    nt = s // tm
    row = lambda n: pl.BlockSpec((tm, n), lambda b, i: (b * nt + i, 0))
    wgt = lambda r, c: pl.BlockSpec((None, r, c), lambda b, i: (layer, 0, 0))
    return pl.pallas_call(
        _cross_kernel,
        grid=(bsz, nt),
        in_specs=[
            row(D_FOX), row(D_SB), row(D_ML), wgt(D_MODEL, D_MODEL), row(D_MODEL),
            pl.BlockSpec((1, D_MODEL), lambda b, i: (0, 0)),
            wgt(D_MODEL, D_X),
            pl.BlockSpec((None, mlen, 2 * D_X), lambda b, i: (layer, b, 0)),
            wgt(D_X, D_MODEL),
        ],
        out_specs=row(D_MODEL),
        out_shape=jax.ShapeDtypeStruct((bsz * s, D_MODEL), F32),
        compiler_params=_cparams("parallel", "parallel"),
        name="cross",
    )(yf, ys, ym, wo, h, nw, wq, kv, wxo)


def _ffn_kernel(h_ref, nw_ref, wg_ref, wu_ref, wd_ref, fw_ref, o_ref, acc_ref, *, f_chunk, final_norm):
    hf = h_ref[...]
    hn = _rms(hf, nw_ref[...]).astype(BF16)
    for c in range(D_FF // f_chunk):
        sl = slice(c * f_chunk, (c + 1) * f_chunk)
        g = _dot(hn, wg_ref[:, sl])
        u = _dot(hn, wu_ref[:, sl])
        a = (g * jax.nn.sigmoid(g) * u).astype(BF16)
        part = _dot(a, wd_ref[sl, :])
        if c == 0:
            acc_ref[...] = part
        else:
            acc_ref[...] += part
    out = hf + acc_ref[...]
    if final_norm:
        out = _rms(out, fw_ref[...])
    o_ref[...] = out


def _ffn(h, nw, wg, wu, wd, fw, layer, final_norm, tm=1024):
    t = h.shape[0]
    const = lambda shape: pl.BlockSpec(shape, lambda i: (0, 0))
    wgt = lambda r, c: pl.BlockSpec((None, r, c), lambda i: (layer, 0, 0), pipeline_mode=pl.Buffered(1))
    return pl.pallas_call(
        functools.partial(_ffn_kernel, f_chunk=256, final_norm=final_norm),
        grid=(t // tm,),
        in_specs=[
            pl.BlockSpec((tm, D_MODEL), lambda i: (i, 0)),
            const((1, D_MODEL)),
            wgt(D_MODEL, D_FF), wgt(D_MODEL, D_FF), wgt(D_FF, D_MODEL),
            const((1, D_MODEL)),
        ],
        out_specs=pl.BlockSpec((tm, D_MODEL), lambda i: (i, 0)),
        out_shape=jax.ShapeDtypeStruct((t, D_MODEL), F32),
        scratch_shapes=[pltpu.VMEM((tm, D_MODEL), F32)],
        compiler_params=_cparams("parallel"),
        name="ffn",
    )(h, nw, wg, wu, wd, fw)


N_FOX_QKV = 3 * D_FOX
N_REST = 3 * D_SB + 4 * D_ML


N_USED_GATES = FOX_HEADS + 2 * ML_HEADS


def _regroup_kernel(w_ref, o_ref):
    o_ref[:, :N_FOX_QKV] = w_ref[:, :N_FOX_QKV].astype(BF16)
    rest = w_ref[:, N_FOX_QKV + FOX_HEADS:N_FOX_QKV + FOX_HEADS + N_REST]
    o_ref[:, N_FOX_QKV:D_MAIN] = rest.astype(BF16)
    lane = lax.broadcasted_iota(jnp.int32, (w_ref.shape[0], N_USED_GATES), 1)
    gates = jnp.where(lane < FOX_HEADS, w_ref[:, N_FOX_QKV:N_FOX_QKV + N_USED_GATES], w_ref[:, D_MAIN:D_MAIN + N_USED_GATES])
    o_ref[:, D_MAIN:] = jnp.zeros((w_ref.shape[0], N_GATE), BF16)
    o_ref[:, D_MAIN:D_MAIN + N_USED_GATES] = gates.astype(BF16)


def _regroup_w_in(w, tr=256):
    depth, rows, n_in = w.shape
    assert N_FOX_QKV + N_REST == D_MAIN and D_MAIN % LANES == 0 and n_in == D_MAIN + N_USED_GATES
    return pl.pallas_call(
        _regroup_kernel,
        grid=(depth, rows // tr),
        in_specs=[pl.BlockSpec((None, tr, n_in), lambda l, i: (l, i, 0))],
        out_specs=pl.BlockSpec((None, tr, D_MAIN + N_GATE), lambda l, i: (l, i, 0)),
        out_shape=jax.ShapeDtypeStruct((depth, rows, D_MAIN + N_GATE), BF16),
        compiler_params=_cparams("parallel", "parallel"),
        name="regroup",
    )(w)


def kernel(x, mem, norm_mix_w, w_in, fox_f_b, ml_conv_w, ml_i_b, ml_f_b, ml_norm_w, w_out, norm_x_w, mem_norm_w,
           wx_q, wx_kv, wx_o, norm_ffn_w, w_gate, w_up, w_down, final_norm_w):
    bsz, s, _ = x.shape
    mlen = mem.shape[1]
    depth = w_in.shape[0]
    h = x.reshape(bsz * s, D_MODEL)
    mem2d = mem.reshape(bsz * mlen, D_MODEL)
    row = lambda v: v.reshape(1, -1)
    w_all = _regroup_w_in(w_in)
    bias = jnp.concatenate([fox_f_b, ml_i_b, ml_f_b, jnp.zeros((depth, N_GATE - 3 * FOX_HEADS), F32)], axis=1)
    wo, wq, wkv, wxo = (w.astype(BF16) for w in (w_out, wx_q, wx_kv, wx_o))
    wg, wu, wd = (w.astype(BF16) for w in (w_gate, w_up, w_down))
    kv = _memkv(mem2d, mem_norm_w.reshape(depth, 1, D_MODEL), wkv, tm=min(512, bsz * mlen))
    for l in range(depth):
        u, g, gt = _inproj(h, row(norm_mix_w[l]), w_all, row(bias[l]), l, bsz, s)
        y_fox = _fox(u, g, gt, bsz, s)
        y_sb = _sb(u, bsz, s)
        y_ml = _mlstm(u, g, gt, ml_conv_w[l], row(ml_norm_w[l]), bsz, s)
        h = _cross(y_fox, y_sb, y_ml, wo, h, row(norm_x_w[l]), wq, kv, wxo, l, bsz, s, mlen)
        h = _ffn(h, row(norm_ffn_w[l]), wg, wu, wd, row(final_norm_w), l, final_norm=(l == depth - 1))
    return h.reshape(bsz, s, D_MODEL)
```

```python
import functools

import jax
import jax.numpy as jnp
from jax import lax
from jax.experimental import pallas as pl
from jax.experimental.pallas import tpu as pltpu

D_MODEL = 1024
HEAD_DIM = 64
FOX_HEADS = 6
SB_HEADS = 4
ML_HEADS = 6
D_FOX = FOX_HEADS * HEAD_DIM
D_SB = SB_HEADS * HEAD_DIM
D_ML = ML_HEADS * HEAD_DIM
ML_CHUNK = 128
CONV_WIDTH = 4
X_HEADS = 4
X_HEAD_DIM = 128
D_X = X_HEADS * X_HEAD_DIM
D_FF = 2816
RMS_EPS = 1e-6

LANES = 128
PAIR = 2 * HEAD_DIM
D_MAIN = 3 * D_FOX + 3 * D_SB + 4 * D_ML
N_GATE = LANES
GT_ROWS = 32
NEG = -1e30
SB_SUB = 128
SB_EXIT = 110.0
LOG2E = 1.4426950408889634
VMEM_LIMIT = 56 * 1024 * 1024

BF16 = jnp.bfloat16
F32 = jnp.float32

FOX_Q, FOX_K, FOX_V = 0, 3, 6
SB_Q, SB_K, SB_V = 9, 11, 13
G_FOX, G_MLA, G_MLB, G_MLP = 0, 6, 12, 18
EXP_B = ML_HEADS * LANES
EXP_A = EXP_B + D_ML
N_EXP = EXP_A + D_ML


def _cparams(*sem):
    return pltpu.CompilerParams(dimension_semantics=sem, vmem_limit_bytes=VMEM_LIMIT)


def _rms(xf, w):
    ms = jnp.mean(xf * xf, axis=-1, keepdims=True)
    return xf * lax.rsqrt(ms + RMS_EPS) * w


def _dot(a, b):
    return jnp.dot(a, b, preferred_element_type=F32)


def _dot_nt(a, b):
    return lax.dot_general(a, b, (((1,), (1,)), ((), ())), preferred_element_type=F32)


def _dot_tn(a, b):
    return lax.dot_general(a, b, (((0,), (0,)), ((), ())), preferred_element_type=F32)


def _log_sigmoid(x):
    return jnp.minimum(x, 0.0) - jnp.log(1.0 + jnp.exp(-jnp.abs(x)))


def _split3(x):
    x1 = x.astype(BF16)
    r1 = x - x1.astype(F32)
    x2 = r1.astype(BF16)
    return x1, x2, (r1 - x2.astype(F32)).astype(BF16)


def _dot_exact(a, b, lhs_exact):
    if lhs_exact:
        x1, x2, x3 = _split3(b)
        return _dot(a, x1) + _dot(a, x2) + _dot(a, x3)
    x1, x2, x3 = _split3(a)
    return _dot(x1, b) + _dot(x2, b) + _dot(x3, b)


def _gate_chunk(pre, carry, tri):
    L = ML_CHUNK
    col = lax.broadcasted_iota(jnp.int32, (L, LANES), 1)
    row = lax.broadcasted_iota(jnp.int32, (L, LANES), 0)
    is_a = (col >= G_MLA) & (col < G_MLB)
    cum = _dot_exact(tri, jnp.where(is_a, 0.0, _log_sigmoid(pre)), lhs_exact=True)
    tot = cum + carry
    a = pre - pltpu.roll(cum, LANES - (G_MLB - G_MLA), 1)
    pm = jnp.where(is_a, a, -jnp.inf)
    k = 1
    while k < L:
        pm = jnp.maximum(pm, jnp.where(row >= k, pltpu.roll(pm, k, 0), -jnp.inf))
        k *= 2
    pm = pltpu.roll(pm, G_MLP - G_MLA, 1)
    out = jnp.where(col < G_MLA, tot,
                    jnp.where(is_a, a, jnp.where(col < G_MLP, cum, jnp.where(col < G_MLP + ML_HEADS, pm, 0.0))))
    return out, tot[L - 1:L, :]


def _inproj_kernel(x_ref, nw_ref, w_ref, b_ref, u_ref, go_ref, gt_ref, carry_ref, *, n_chunk, steps_per_seq):
    @pl.when(pl.program_id(0) % steps_per_seq == 0)
    def _():
        carry_ref[...] = jnp.zeros_like(carry_ref)

    xn = _rms(x_ref[...], nw_ref[...]).astype(BF16)
    n_all = D_MAIN + N_GATE
    for c in reversed(range(n_all // n_chunk)):
        lo = c * n_chunk
        res = _dot(xn, w_ref[:, lo:lo + n_chunk])
        if lo + n_chunk <= D_MAIN:
            u_ref[:, lo:lo + n_chunk] = res.astype(BF16)
            continue
        u_ref[:, lo:D_MAIN] = res[:, :D_MAIN - lo].astype(BF16)
        pre_all = res[:, D_MAIN - lo:] + b_ref[...]
        L = ML_CHUNK
        r = lax.broadcasted_iota(jnp.int32, (L, L), 0)
        c = lax.broadcasted_iota(jnp.int32, (L, L), 1)
        tri = jnp.where(r >= c, 1.0, 0.0).astype(BF16)
        carry = carry_ref[...]
        for ci in range(x_ref.shape[0] // L):
            out, carry = _gate_chunk(pre_all[ci * L:(ci + 1) * L], carry, tri)
            go_ref[ci * L:(ci + 1) * L, :] = out
            gt_ref[:, ci * L:(ci + 1) * L] = out.T[:GT_ROWS, :]
        carry_ref[...] = carry


def _inproj(h, nw, w_all, bias, layer, bsz, s, tm=1024):
    t = h.shape[0]
    steps = s // tm
    return pl.pallas_call(
        functools.partial(_inproj_kernel, n_chunk=512, steps_per_seq=steps),
        grid=(t // tm,),
        in_specs=[
            pl.BlockSpec((tm, D_MODEL), lambda i: (i, 0)),
            pl.BlockSpec((1, D_MODEL), lambda i: (0, 0)),
            pl.BlockSpec((None, D_MODEL, D_MAIN + N_GATE), lambda i: (layer, 0, 0)),
            pl.BlockSpec((1, N_GATE), lambda i: (0, 0)),
        ],
        out_specs=[
            pl.BlockSpec((tm, D_MAIN), lambda i: (i, 0)),
            pl.BlockSpec((tm, N_GATE), lambda i: (i, 0)),
            pl.BlockSpec((None, GT_ROWS, tm), lambda i: (i // steps, 0, i % steps)),
        ],
        out_shape=[jax.ShapeDtypeStruct((t, D_MAIN), BF16), jax.ShapeDtypeStruct((t, N_GATE), F32),
                   jax.ShapeDtypeStruct((bsz, GT_ROWS, s), F32)],
        scratch_shapes=[pltpu.VMEM((1, N_GATE), F32)],
        compiler_params=_cparams("arbitrary"),
        name="inproj",
    )(h, nw, w_all, bias)


def _head_masks(q2):
    lane = lax.broadcasted_iota(jnp.int32, q2.shape, 1)
    zero = jnp.zeros_like(q2)
    return jnp.where(lane < HEAD_DIM, q2, zero), jnp.where(lane >= HEAD_DIM, q2, zero)


def _stack_heads(q2):
    return jnp.concatenate(_head_masks(q2), axis=0)


def _unstack_heads(x, rows):
    lane = lax.broadcasted_iota(jnp.int32, (rows, LANES), 1)
    return jnp.where(lane < HEAD_DIM, x[:rows], x[rows:])


def _fox_kernel(q_ref, k_ref, v_ref, g_ref, gt_ref, o_ref, ct_ref, m_ref, acc_ref, *, tq, nq):
    p = pl.program_id(1)
    qi = pl.program_id(2)
    half = tq // 2

    def sweep(blk):
        qs = _stack_heads(q_ref[...] * jnp.asarray(HEAD_DIM ** -0.5, BF16))
        gq = g_ref[...]
        lane = lax.broadcasted_iota(jnp.int32, (tq, LANES), 1)
        for hh in range(2):
            ct = jnp.sum(jnp.where(lane == G_FOX + 2 * p + hh, gq, 0.0), axis=1, keepdims=True)
            ct_ref[hh * tq:(hh + 1) * tq, :] = jnp.broadcast_to(ct * LOG2E, (tq, LANES))
        m_ref[...] = jnp.full_like(m_ref, NEG)
        acc_ref[...] = jnp.zeros_like(acc_ref)

        def group(start, nk, row0, masked):
            nr = tq - row0
            qsub = qs if row0 == 0 else jnp.concatenate([qs[row0:tq], qs[tq + row0:]], axis=0)
            s = _dot_nt(qsub, k_ref[start:start + nk, :])
            v2 = v_ref[start:start + nk, :]
            one = jnp.ones_like(v2)
            lo_lanes = lax.broadcasted_iota(jnp.int32, (nk, LANES), 1) < HEAD_DIM
            v_ones = (jnp.where(lo_lanes, v2, one), jnp.where(lo_lanes, one, v2))
            if masked:
                rq = lax.broadcasted_iota(jnp.int32, (nr, nk), 0)
                rk = lax.broadcasted_iota(jnp.int32, (nr, nk), 1)
                causal = jnp.where(rk <= rq, 0.0, NEG)
            for hh in range(2):
                rows = slice(hh * tq + row0, (hh + 1) * tq)
                cs = gt_ref[pl.ds(G_FOX + 2 * p + hh, 1), start:start + nk] * LOG2E
                r = s[hh * nr:(hh + 1) * nr] * LOG2E - cs
                if masked:
                    r = r + causal
                ct = ct_ref[rows, :]
                m_prev = m_ref[rows, :]
                m_next = jnp.maximum(m_prev, jnp.max(r, axis=1, keepdims=True) + ct)
                pr = jnp.exp2(r - jnp.tile(m_next - ct, (1, nk // LANES)))
                alpha = jnp.exp2(m_prev - m_next)
                acc_ref[rows, :] = alpha * acc_ref[rows, :] + _dot(pr.astype(BF16), v_ones[hh])
                m_ref[rows, :] = m_next

        for j in range(blk):
            group(j * tq, tq, 0, False)
        group(blk * tq, half, 0, True)
        group(blk * tq + half, half, half, True)
        acc = acc_ref[...]
        o_ref[...] = _unstack_heads(acc / pltpu.roll(acc, HEAD_DIM, 1), tq).astype(BF16)

    for blk in range(nq):
        pl.when(qi == blk)(functools.partial(sweep, blk))


def _fox(u, g, gt, bsz, s, tq=512):
    nq = s // tq
    stat = pltpu.VMEM((2 * tq, LANES), F32)
    return pl.pallas_call(
        functools.partial(_fox_kernel, tq=tq, nq=nq),
        grid=(bsz, FOX_HEADS // 2, nq),
        in_specs=[
            pl.BlockSpec((tq, PAIR), lambda b, p, i: (b * nq + i, FOX_Q + p)),
            pl.BlockSpec((s, PAIR), lambda b, p, i: (b, FOX_K + p)),
            pl.BlockSpec((s, PAIR), lambda b, p, i: (b, FOX_V + p)),
            pl.BlockSpec((tq, N_GATE), lambda b, p, i: (b * nq + i, 0)),
            pl.BlockSpec((None, 8, s), lambda b, p, i: (b, 0, 0)),
        ],
        out_specs=pl.BlockSpec((tq, PAIR), lambda b, p, i: (b * nq + i, p)),
        out_shape=jax.ShapeDtypeStruct((bsz * s, D_FOX), BF16),
        scratch_shapes=[stat, stat, stat],
        compiler_params=_cparams("parallel", "parallel", "arbitrary"),
        name="fox",
    )(u, u, u, g, gt)


def _sb_kernel(q_ref, k_ref, v_ref, o_ref, qs_ref, run_ref, acc_ref, *, tq, tg, nq):
    qi = pl.program_id(2)
    n_diag = tq // tg

    def sweep(blk):
        qs_ref[...] = _stack_heads(q_ref[...] * jnp.asarray(HEAD_DIM ** -0.5, BF16))
        r = lax.broadcasted_iota(jnp.int32, (2 * SB_SUB, 2 * SB_SUB), 0)
        c = lax.broadcasted_iota(jnp.int32, (2 * SB_SUB, 2 * SB_SUB), 1)
        suffix = jnp.where((c >= SB_SUB) | (jnp.where(r >= SB_SUB, r - SB_SUB, r) >= c), 1.0, 0.0).astype(BF16)
        run_ref[...] = jnp.zeros_like(run_ref)
        acc_ref[...] = jnp.zeros_like(acc_ref)

        def group(g, diag):
            row0 = tq - (g + 1) * tg if diag else 0
            nr = tq - row0
            heads = lambda ref: ref[...] if row0 == 0 else jnp.concatenate([ref[row0:tq, :], ref[tq + row0:, :]], axis=0)
            start = (blk + 1) * tq - (g + 1) * tg
            if not isinstance(start, int):
                start = pl.multiple_of(start, tg)
            z = _dot_nt(heads(qs_ref), k_ref[pl.ds(start, tg), :])
            sp = jnp.maximum(z, 0.0) + jnp.log(1.0 + jnp.exp(-jnp.abs(z)))
            if diag:
                rq = lax.broadcasted_iota(jnp.int32, (2 * nr, tg), 0)
                rk = lax.broadcasted_iota(jnp.int32, (2 * nr, tg), 1)
                valid = rk < jnp.where(rq >= nr, rq - nr, rq)
                sp = jnp.where(valid, sp, 0.0)
            hi = sp.astype(BF16)
            lo = (sp - hi.astype(F32)).astype(BF16)
            offs = heads(run_ref)
            parts = [None] * (tg // SB_SUB)
            for sb in reversed(range(tg // SB_SUB)):
                sl = slice(sb * SB_SUB, (sb + 1) * SB_SUB)
                rs = _dot(jnp.concatenate([hi[:, sl], lo[:, sl]], axis=1), suffix)
                a = jnp.exp(z[:, sl] - rs[:, :SB_SUB] - offs)
                if diag:
                    a = jnp.where(valid[:, sl], a, 0.0)
                parts[sb] = a.astype(BF16)
                offs = offs + rs[:, SB_SUB:]
            pv = _dot(jnp.concatenate(parts, axis=1), v_ref[pl.ds(start, tg), :])
            for hh in range(2):
                rows = slice(hh * tq + row0, (hh + 1) * tq)
                run_ref[rows, :] = offs[hh * nr:(hh + 1) * nr]
                acc_ref[rows, :] += pv[hh * nr:(hh + 1) * nr]
            return jnp.min(offs)

        for g in range(n_diag):
            low = group(g, True)
        n_groups = (blk + 1) * n_diag
        first = n_diag
        if blk > 0:
            low = group(n_diag, False)
            first += 1
        if first < n_groups:
            def cond(carry):
                g, low = carry
                return (g < n_groups) & (low < SB_EXIT)

            def body(carry):
                g, _ = carry
                return g + 1, group(g, False)

            lax.while_loop(cond, body, (jnp.int32(first), low))
        o_ref[...] = _unstack_heads(acc_ref[...], tq).astype(BF16)

    for blk in range(nq):
        pl.when(qi == blk)(functools.partial(sweep, blk))


def _sb(u, bsz, s, tq=512, tg=256):
    nq = s // tq
    stat = pltpu.VMEM((2 * tq, LANES), F32)
    return pl.pallas_call(
        functools.partial(_sb_kernel, tq=tq, tg=tg, nq=nq),
        grid=(bsz, SB_HEADS // 2, nq),
        in_specs=[
            pl.BlockSpec((tq, PAIR), lambda b, p, i: (b * nq + i, SB_Q + p)),
            pl.BlockSpec((s, PAIR), lambda b, p, i: (b, SB_K + p)),
            pl.BlockSpec((s, PAIR), lambda b, p, i: (b, SB_V + p)),
        ],
        out_specs=pl.BlockSpec((tq, PAIR), lambda b, p, i: (b * nq + i, p)),
        out_shape=jax.ShapeDtypeStruct((bsz * s, D_SB), BF16),
        scratch_shapes=[pltpu.VMEM((2 * tq, PAIR), BF16), stat, stat],
        compiler_params=_cparams("parallel", "parallel", "arbitrary"),
        name="sb",
    )(u, u, u)


def _mlstm_kernel(xq_ref, xk_ref, v_ref, og_ref, g_ref, gt_ref, cw_ref, nw_ref, o_ref, qs_ref, ks_ref, e_ref, *, s):
    L = ML_CHUNK
    nc = s // L
    TAIL = 16
    row = lax.broadcasted_iota(jnp.int32, (TAIL, D_ML), 0)

    def conv_chunk(c, _):
        start = pl.multiple_of(c * L, L)
        prev_start = pl.multiple_of(jnp.maximum(c - 1, 0) * L, L)
        for x_ref, dst_ref, w0, scale in ((xq_ref, qs_ref, 0, 1.0), (xk_ref, ks_ref, D_ML, HEAD_DIM ** -0.5)):
            cur = x_ref[pl.ds(start, L), :].astype(F32)
            tail = x_ref[pl.ds(prev_start + L - TAIL, TAIL), :].astype(F32)
            tail = jnp.where(c > 0, tail, 0.0)
            w = cw_ref[:, w0:w0 + D_ML]
            y = cur * w[CONV_WIDTH - 1:CONV_WIDTH, :]
            for k in range(1, CONV_WIDTH):
                sh = pltpu.roll(cur, k, 0)
                head = jnp.where(row < k, pltpu.roll(tail, k, 0), sh[:TAIL])
                sh = jnp.concatenate([head, sh[TAIL:]], axis=0)
                y = y + sh * w[CONV_WIDTH - 1 - k:CONV_WIDTH - k, :]
            y = y * jax.nn.sigmoid(y) * scale
            dst_ref[pl.ds(start, L), :] = y.astype(BF16)
        return 0

    conv_chunk(0, 0)

    ci = lax.broadcasted_iota(jnp.int32, (LANES, N_EXP), 0)
    cj = lax.broadcasted_iota(jnp.int32, (LANES, N_EXP), 1)
    src_col = jnp.where(cj < EXP_B, G_MLP + lax.shift_right_logical(cj, 7),
                        jnp.where(cj < EXP_A, G_MLB + lax.shift_right_logical(cj - EXP_B, 6),
                                  G_MLA + lax.shift_right_logical(cj - EXP_A, 6)))
    e_ref[...] = jnp.where(ci == src_col, 1.0, 0.0).astype(BF16)

    lane = lax.broadcasted_iota(jnp.int32, (L, LANES), 1)
    lo_lanes = lane < HEAD_DIM
    r_ll = lax.broadcasted_iota(jnp.int32, (L, L), 0)
    c_ll = lax.broadcasted_iota(jnp.int32, (L, L), 1)
    tri = r_ll >= c_ll
    blockdiag = (r_ll < HEAD_DIM) == (c_ll < HEAD_DIM)
    ones = jnp.ones((L, LANES), BF16)

    def chunk(c, carry):
        conv_chunk(jnp.minimum(c + 1, nc - 1), 0)
        start = pl.multiple_of(c * L, L)
        rep = _dot_exact(g_ref[pl.ds(start, L), :], e_ref[...], lhs_exact=False)
        gtc = gt_ref[:, pl.ds(start, L)]
        new_carry = []
        for p in range(ML_HEADS // 2):
            c2, nm, m2 = carry[p]
            sl = slice(p * PAIR, (p + 1) * PAIR)
            q2 = qs_ref[pl.ds(start, L), sl]
            k2 = ks_ref[pl.ds(start, L), sl]
            v_one = jnp.concatenate([v_ref[pl.ds(start, L), sl], ones], axis=1)
            b2 = rep[:, EXP_B + p * PAIR:EXP_B + (p + 1) * PAIR]
            a2 = rep[:, EXP_A + p * PAIR:EXP_A + (p + 1) * PAIR]
            pm = [rep[:, (2 * p + hh) * LANES:(2 * p + hh + 1) * LANES] for hh in range(2)]
            pm2 = jnp.where(lo_lanes, pm[0], pm[1])
            qk = _dot_nt(_stack_heads(q2), k2)
            sc = []
            for hh in range(2):
                a_row = gtc[G_MLA + 2 * p + hh:G_MLA + 2 * p + hh + 1, :]
                sc.append(qk[hh * L:(hh + 1) * L] * jnp.exp(jnp.where(tri, a_row - pm[hh], -jnp.inf)))
            r = _dot(jnp.concatenate(sc, axis=0).astype(BF16), v_one)
            num_loc = jnp.where(lo_lanes, r[:L, :LANES], r[L:, :LANES])
            den_loc = jnp.where(lo_lanes, r[:L, LANES:], r[L:, LANES:])
            m_loc = b2 + pm2
            inter = _dot(q2, jnp.concatenate([c2, nm], axis=1).astype(BF16))
            it2 = b2 + m2
            m_t = jnp.maximum(it2, m_loc)
            f_loc = jnp.exp(m_loc - m_t)
            w_int = jnp.exp(it2 - m_t)
            num = num_loc * f_loc + w_int * inter[:, :LANES]
            den = den_loc * f_loc + w_int * inter[:, LANES:]
            hp = num / jnp.maximum(jnp.abs(den), jnp.exp(-m_t))
            sq = hp * hp
            ms0 = jnp.sum(jnp.where(lo_lanes, sq, 0.0), axis=1, keepdims=True)
            ms1 = jnp.sum(jnp.where(lo_lanes, 0.0, sq), axis=1, keepdims=True)
            msq = jnp.where(lo_lanes, ms0, ms1) * (1.0 / HEAD_DIM)
            hp = hp * lax.rsqrt(msq + RMS_EPS) * nw_ref[:, sl]
            gate = jax.nn.sigmoid(og_ref[pl.ds(start, L), sl].astype(F32))
            o_ref[pl.ds(start, L), sl] = (gate * hp).astype(BF16)
            g2 = b2[L - 1:L, :]
            pl2 = pm2[L - 1:L, :]
            kw = (k2.astype(F32) * jnp.exp(a2 - pl2)).astype(BF16)
            upd = _dot_tn(kw, v_one)
            m_new = jnp.maximum(g2 + m2, g2 + pl2)
            decay = jnp.exp(g2 + m2 - m_new)
            scale = jnp.exp(g2 + pl2 - m_new)
            c2 = decay * c2 + scale * jnp.where(blockdiag, upd[:, :LANES], 0.0)
            nm = decay * nm + scale * jnp.where(blockdiag, upd[:, LANES:], 0.0)
            new_carry.append((c2, nm, m_new))
        return tuple(new_carry)

    init = tuple((jnp.zeros((LANES, LANES), F32), jnp.zeros((LANES, LANES), F32), jnp.zeros((1, LANES), F32))
                 for _ in range(ML_HEADS // 2))
    lax.fori_loop(0, nc, chunk, init, unroll=4)


def _mlstm(u, g, gt, conv_w, norm_w, bsz, s):
    blk = lambda j: pl.BlockSpec((s, D_ML), lambda b: (b, j))
    first = (3 * D_FOX + 3 * D_SB) // D_ML
    return pl.pallas_call(
        functools.partial(_mlstm_kernel, s=s),
        grid=(bsz,),
        in_specs=[
            blk(first), blk(first + 1), blk(first + 2), blk(first + 3),
            pl.BlockSpec((s, N_GATE), lambda b: (b, 0)),
            pl.BlockSpec((None, GT_ROWS, s), lambda b: (b, 0, 0)),
            pl.BlockSpec((CONV_WIDTH, 2 * D_ML), lambda b: (0, 0)),
            pl.BlockSpec((1, D_ML), lambda b: (0, 0)),
        ],
        out_specs=pl.BlockSpec((s, D_ML), lambda b: (b, 0)),
        out_shape=jax.ShapeDtypeStruct((bsz * s, D_ML), BF16),
        scratch_shapes=[pltpu.VMEM((s, D_ML), BF16), pltpu.VMEM((s, D_ML), BF16), pltpu.VMEM((LANES, N_EXP), BF16)],
        compiler_params=_cparams("parallel"),
        name="mlstm",
    )(u, u, u, u, g, gt, conv_w, norm_w)


def _memkv_kernel(x_ref, nw_ref, w_ref, o_ref):
    xn = _rms(x_ref[...], nw_ref[...]).astype(BF16)
    o_ref[...] = _dot(xn, w_ref[...]).astype(BF16)


def _memkv(mem2d, nw, w, tm=512):
    t = mem2d.shape[0]
    depth = w.shape[0]
    return pl.pallas_call(
        _memkv_kernel,
        grid=(depth, t // tm),
        in_specs=[
            pl.BlockSpec((tm, D_MODEL), lambda l, i: (i, 0)),
            pl.BlockSpec((None, 1, D_MODEL), lambda l, i: (l, 0, 0)),
            pl.BlockSpec((None, D_MODEL, 2 * D_X), lambda l, i: (l, 0, 0)),
        ],
        out_specs=pl.BlockSpec((None, tm, 2 * D_X), lambda l, i: (l, i, 0)),
        out_shape=jax.ShapeDtypeStruct((depth, t, 2 * D_X), BF16),
        compiler_params=_cparams("parallel", "parallel"),
        name="memkv",
    )(mem2d, nw, w)


def _cross_kernel(yf_ref, ys_ref, ym_ref, wo_ref, h_ref, nw_ref, wq_ref, kv_ref, wxo_ref, o_ref):
    hf = (h_ref[...] + _dot(yf_ref[...], wo_ref[:D_FOX, :]) + _dot(ys_ref[...], wo_ref[D_FOX:D_FOX + D_SB, :])
          + _dot(ym_ref[...], wo_ref[D_FOX + D_SB:, :]))
    hn = _rms(hf, nw_ref[...]).astype(BF16)
    q = _dot(hn, wq_ref[...]).astype(BF16)
    outs = []
    for hd in range(X_HEADS):
        sl = slice(hd * X_HEAD_DIM, (hd + 1) * X_HEAD_DIM)
        k = kv_ref[:, sl]
        v = kv_ref[:, D_X + hd * X_HEAD_DIM:D_X + (hd + 1) * X_HEAD_DIM]
        s = _dot_nt(q[:, sl], k) * (X_HEAD_DIM ** -0.5)
        m = jnp.max(s, axis=1, keepdims=True)
        e = jnp.exp(s - m)
        pr = e / jnp.sum(e, axis=1, keepdims=True)
        outs.append(_dot(pr.astype(BF16), v).astype(BF16))
    o = jnp.concatenate(outs, axis=1)
    o_ref[...] = hf + _dot(o, wxo_ref[...])


def _cross(yf, ys, ym, wo, h, nw, wq, kv, wxo, layer, bsz, s, mlen, tm=1024):
    nt = s // tm
    row = lambda n: pl.BlockSpec((tm, n), lambda b, i: (b * nt + i, 0))
    wgt = lambda r, c: pl.BlockSpec((None, r, c), lambda b, i: (layer, 0, 0))
    return pl.pallas_call(
        _cross_kernel,
        grid=(bsz, nt),
        in_specs=[
            row(D_FOX), row(D_SB), row(D_ML), wgt(D_MODEL, D_MODEL), row(D_MODEL),
            pl.BlockSpec((1, D_MODEL), lambda b, i: (0, 0)),
            wgt(D_MODEL, D_X),
            pl.BlockSpec((None, mlen, 2 * D_X), lambda b, i: (layer, b, 0)),
            wgt(D_X, D_MODEL),
        ],
        out_specs=row(D_MODEL),
        out_shape=jax.ShapeDtypeStruct((bsz * s, D_MODEL), F32),
        compiler_params=_cparams("parallel", "parallel"),
        name="cross",
    )(yf, ys, ym, wo, h, nw, wq, kv, wxo)


def _ffn_kernel(h_ref, nw_ref, wg_ref, wu_ref, wd_ref, fw_ref, o_ref, acc_ref, *, f_chunk, final_norm):
    hf = h_ref[...]
    hn = _rms(hf, nw_ref[...]).astype(BF16)
    for c in range(D_FF // f_chunk):
        sl = slice(c * f_chunk, (c + 1) * f_chunk)
        g = _dot(hn, wg_ref[:, sl].astype(BF16))
        u = _dot(hn, wu_ref[:, sl].astype(BF16))
        a = (g * jax.nn.sigmoid(g) * u).astype(BF16)
        part = _dot(a, wd_ref[sl, :].astype(BF16))
        if c == 0:
            acc_ref[...] = part
        else:
            acc_ref[...] += part
    out = hf + acc_ref[...]
    if final_norm:
        out = _rms(out, fw_ref[...])
    o_ref[...] = out


def _ffn(h, nw, wg, wu, wd, fw, layer, final_norm, tm=512):
    t = h.shape[0]
    const = lambda shape: pl.BlockSpec(shape, lambda i: (0, 0))
    wgt = lambda r, c: pl.BlockSpec((None, r, c), lambda i: (layer, 0, 0), pipeline_mode=pl.Buffered(1))
    return pl.pallas_call(
        functools.partial(_ffn_kernel, f_chunk=256, final_norm=final_norm),
        grid=(t // tm,),
        in_specs=[
            pl.BlockSpec((tm, D_MODEL), lambda i: (i, 0)),
            const((1, D_MODEL)),
            wgt(D_MODEL, D_FF), wgt(D_MODEL, D_FF), wgt(D_FF, D_MODEL),
            const((1, D_MODEL)),
        ],
        out_specs=pl.BlockSpec((tm, D_MODEL), lambda i: (i, 0)),
        out_shape=jax.ShapeDtypeStruct((t, D_MODEL), F32),
        scratch_shapes=[pltpu.VMEM((tm, D_MODEL), F32)],
        compiler_params=_cparams("parallel"),
        name="ffn",
    )(h, nw, wg, wu, wd, fw)


N_FOX_QKV = 3 * D_FOX
N_REST = 3 * D_SB + 4 * D_ML


N_USED_GATES = FOX_HEADS + 2 * ML_HEADS


def _regroup_kernel(w_ref, o_ref):
    o_ref[:, :N_FOX_QKV] = w_ref[:, :N_FOX_QKV].astype(BF16)
    rest = w_ref[:, N_FOX_QKV + FOX_HEADS:N_FOX_QKV + FOX_HEADS + N_REST]
    o_ref[:, N_FOX_QKV:D_MAIN] = rest.astype(BF16)
    lane = lax.broadcasted_iota(jnp.int32, (w_ref.shape[0], N_USED_GATES), 1)
    gates = jnp.where(lane < FOX_HEADS, w_ref[:, N_FOX_QKV:N_FOX_QKV + N_USED_GATES], w_ref[:, D_MAIN:D_MAIN + N_USED_GATES])
    o_ref[:, D_MAIN:] = jnp.zeros((w_ref.shape[0], N_GATE), BF16)
    o_ref[:, D_MAIN:D_MAIN + N_USED_GATES] = gates.astype(BF16)


def _regroup_w_in(w, tr=256):
    depth, rows, n_in = w.shape
    assert N_FOX_QKV + N_REST == D_MAIN and D_MAIN % LANES == 0 and n_in == D_MAIN + N_USED_GATES
    return pl.pallas_call(
        _regroup_kernel,
        grid=(depth, rows // tr),
        in_specs=[pl.BlockSpec((None, tr, n_in), lambda l, i: (l, i, 0))],
        out_specs=pl.BlockSpec((None, tr, D_MAIN + N_GATE), lambda l, i: (l, i, 0)),
        out_shape=jax.ShapeDtypeStruct((depth, rows, D_MAIN + N_GATE), BF16),
        compiler_params=_cparams("parallel", "parallel"),
        name="regroup",
    )(w)


def kernel(x, mem, norm_mix_w, w_in, fox_f_b, ml_conv_w, ml_i_b, ml_f_b, ml_norm_w, w_out, norm_x_w, mem_norm_w,
           wx_q, wx_kv, wx_o, norm_ffn_w, w_gate, w_up, w_down, final_norm_w):
    bsz, s, _ = x.shape
    mlen = mem.shape[1]
    depth = w_in.shape[0]
    h = x.reshape(bsz * s, D_MODEL)
    mem2d = mem.reshape(bsz * mlen, D_MODEL)
    row = lambda v: v.reshape(1, -1)
    w_all = _regroup_w_in(w_in)
    bias = jnp.concatenate([fox_f_b, ml_i_b, ml_f_b, jnp.zeros((depth, N_GATE - 3 * FOX_HEADS), F32)], axis=1)
    wo, wq, wkv, wxo = (w.astype(BF16) for w in (w_out, wx_q, wx_kv, wx_o))
    kv = _memkv(mem2d, mem_norm_w.reshape(depth, 1, D_MODEL), wkv, tm=min(512, bsz * mlen))
    for l in range(depth):
        u, g, gt = _inproj(h, row(norm_mix_w[l]), w_all, row(bias[l]), l, bsz, s)
        y_fox = _fox(u, g, gt, bsz, s)
        y_sb = _sb(u, bsz, s)
        y_ml = _mlstm(u, g, gt, ml_conv_w[l], row(ml_norm_w[l]), bsz, s)
        h = _cross(y_fox, y_sb, y_ml, wo, h, row(norm_x_w[l]), wq, kv, wxo, l, bsz, s, mlen)
        h = _ffn(h, row(norm_ffn_w[l]), w_gate, w_up, w_down, row(final_norm_w), l, final_norm=(l == depth - 1))
    return h.reshape(bsz, s, D_MODEL)
```

```python
import functools

import jax
import jax.numpy as jnp
from jax import lax
from jax.experimental import pallas as pl
from jax.experimental.pallas import tpu as pltpu

D_MODEL = 1024
HEAD_DIM = 64
FOX_HEADS = 6
SB_HEADS = 4
ML_HEADS = 6
D_FOX = FOX_HEADS * HEAD_DIM
D_SB = SB_HEADS * HEAD_DIM
D_ML = ML_HEADS * HEAD_DIM
ML_CHUNK = 128
CONV_WIDTH = 4
X_HEADS = 4
X_HEAD_DIM = 128
D_X = X_HEADS * X_HEAD_DIM
D_FF = 2816
RMS_EPS = 1e-6

LANES = 128
PAIR = 2 * HEAD_DIM
D_MAIN = 3 * D_FOX + 3 * D_SB + 4 * D_ML
N_GATE = LANES
GT_ROWS = 32
NEG = -1e30
SB_SUB = 128
SB_EXIT = 110.0
LOG2E = 1.4426950408889634
VMEM_LIMIT = 56 * 1024 * 1024

BF16 = jnp.bfloat16
F32 = jnp.float32

FOX_Q, FOX_K, FOX_V = 0, 3, 6
SB_Q, SB_K, SB_V = 9, 11, 13
G_FOX, G_MLA, G_MLB, G_MLP = 0, 6, 12, 18
EXP_B = ML_HEADS * LANES
EXP_A = EXP_B + D_ML
N_EXP = EXP_A + D_ML


def _cparams(*sem):
    return pltpu.CompilerParams(dimension_semantics=sem, vmem_limit_bytes=VMEM_LIMIT)


def _rms(xf, w):
    ms = jnp.mean(xf * xf, axis=-1, keepdims=True)
    return xf * lax.rsqrt(ms + RMS_EPS) * w


def _dot(a, b):
    return jnp.dot(a, b, preferred_element_type=F32)


def _dot_nt(a, b):
    return lax.dot_general(a, b, (((1,), (1,)), ((), ())), preferred_element_type=F32)


def _dot_tn(a, b):
    return lax.dot_general(a, b, (((0,), (0,)), ((), ())), preferred_element_type=F32)


def _log_sigmoid(x):
    return jnp.minimum(x, 0.0) - jnp.log(1.0 + jnp.exp(-jnp.abs(x)))


def _split3(x):
    x1 = x.astype(BF16)
    r1 = x - x1.astype(F32)
    x2 = r1.astype(BF16)
    return x1, x2, (r1 - x2.astype(F32)).astype(BF16)


def _dot_exact(a, b, lhs_exact):
    if lhs_exact:
        x1, x2, x3 = _split3(b)
        return _dot(a, x1) + _dot(a, x2) + _dot(a, x3)
    x1, x2, x3 = _split3(a)
    return _dot(x1, b) + _dot(x2, b) + _dot(x3, b)


def _gate_chunk(pre, carry, tri):
    L = ML_CHUNK
    col = lax.broadcasted_iota(jnp.int32, (L, LANES), 1)
    row = lax.broadcasted_iota(jnp.int32, (L, LANES), 0)
    is_a = (col >= G_MLA) & (col < G_MLB)
    cum = _dot_exact(tri, jnp.where(is_a, 0.0, _log_sigmoid(pre)), lhs_exact=True)
    tot = cum + carry
    a = pre - pltpu.roll(cum, LANES - (G_MLB - G_MLA), 1)
    pm = jnp.where(is_a, a, -jnp.inf)
    k = 1
    while k < L:
        pm = jnp.maximum(pm, jnp.where(row >= k, pltpu.roll(pm, k, 0), -jnp.inf))
        k *= 2
    pm = pltpu.roll(pm, G_MLP - G_MLA, 1)
    out = jnp.where(col < G_MLA, tot,
                    jnp.where(is_a, a, jnp.where(col < G_MLP, cum, jnp.where(col < G_MLP + ML_HEADS, pm, 0.0))))
    return out, tot[L - 1:L, :]


def _inproj_kernel(x_ref, nw_ref, w_ref, b_ref, u_ref, go_ref, gt_ref, carry_ref, *, n_chunk, steps_per_seq):
    @pl.when(pl.program_id(0) % steps_per_seq == 0)
    def _():
        carry_ref[...] = jnp.zeros_like(carry_ref)

    xn = _rms(x_ref[...], nw_ref[...]).astype(BF16)
    n_all = D_MAIN + N_GATE
    for c in reversed(range(n_all // n_chunk)):
        lo = c * n_chunk
        res = _dot(xn, w_ref[:, lo:lo + n_chunk])
        if lo + n_chunk <= D_MAIN:
            u_ref[:, lo:lo + n_chunk] = res.astype(BF16)
            continue
        u_ref[:, lo:D_MAIN] = res[:, :D_MAIN - lo].astype(BF16)
        pre_all = res[:, D_MAIN - lo:] + b_ref[...]
        L = ML_CHUNK
        r = lax.broadcasted_iota(jnp.int32, (L, L), 0)
        c = lax.broadcasted_iota(jnp.int32, (L, L), 1)
        tri = jnp.where(r >= c, 1.0, 0.0).astype(BF16)
        carry = carry_ref[...]
        for ci in range(x_ref.shape[0] // L):
            out, carry = _gate_chunk(pre_all[ci * L:(ci + 1) * L], carry, tri)
            go_ref[ci * L:(ci + 1) * L, :] = out
            gt_ref[:, ci * L:(ci + 1) * L] = out.T[:GT_ROWS, :]
        carry_ref[...] = carry


def _inproj(h, nw, w_all, bias, layer, bsz, s, tm=1024):
    t = h.shape[0]
    steps = s // tm
    return pl.pallas_call(
        functools.partial(_inproj_kernel, n_chunk=512, steps_per_seq=steps),
        grid=(t // tm,),
        in_specs=[
            pl.BlockSpec((tm, D_MODEL), lambda i: (i, 0)),
            pl.BlockSpec((1, D_MODEL), lambda i: (0, 0)),
            pl.BlockSpec((None, D_MODEL, D_MAIN + N_GATE), lambda i: (layer, 0, 0)),
            pl.BlockSpec((1, N_GATE), lambda i: (0, 0)),
        ],
        out_specs=[
            pl.BlockSpec((tm, D_MAIN), lambda i: (i, 0)),
            pl.BlockSpec((tm, N_GATE), lambda i: (i, 0)),
            pl.BlockSpec((None, GT_ROWS, tm), lambda i: (i // steps, 0, i % steps)),
        ],
        out_shape=[jax.ShapeDtypeStruct((t, D_MAIN), BF16), jax.ShapeDtypeStruct((t, N_GATE), F32),
                   jax.ShapeDtypeStruct((bsz, GT_ROWS, s), F32)],
        scratch_shapes=[pltpu.VMEM((1, N_GATE), F32)],
        compiler_params=_cparams("arbitrary"),
        name="inproj",
    )(h, nw, w_all, bias)


def _head_masks(q2):
    lane = lax.broadcasted_iota(jnp.int32, q2.shape, 1)
    zero = jnp.zeros_like(q2)
    return jnp.where(lane < HEAD_DIM, q2, zero), jnp.where(lane >= HEAD_DIM, q2, zero)


def _stack_heads(q2):
    return jnp.concatenate(_head_masks(q2), axis=0)


def _unstack_heads(x, rows):
    lane = lax.broadcasted_iota(jnp.int32, (rows, LANES), 1)
    return jnp.where(lane < HEAD_DIM, x[:rows], x[rows:])


def _fox_kernel(q_ref, k_ref, v_ref, g_ref, gt_ref, o_ref, ct_all, m_all, acc_all, *, tq, nq):
    p = pl.program_id(1)
    half = tq // 2

    def sweep(blk):
        qrows = slice(blk * tq, (blk + 1) * tq)
        ct_ref, m_ref, acc_ref = ct_all.at[blk], m_all.at[blk], acc_all.at[blk]
        qs = _stack_heads(q_ref[qrows, :] * jnp.asarray(HEAD_DIM ** -0.5, BF16))
        gq = g_ref[qrows, :]
        lane = lax.broadcasted_iota(jnp.int32, (tq, LANES), 1)
        for hh in range(2):
            ct = jnp.sum(jnp.where(lane == G_FOX + 2 * p + hh, gq, 0.0), axis=1, keepdims=True)
            ct_ref[hh * tq:(hh + 1) * tq, :] = jnp.broadcast_to(ct * LOG2E, (tq, LANES))
        m_ref[...] = jnp.full_like(m_ref, NEG)
        acc_ref[...] = jnp.zeros_like(acc_ref)

        def group(start, nk, row0, masked):
            nr = tq - row0
            qsub = qs if row0 == 0 else jnp.concatenate([qs[row0:tq], qs[tq + row0:]], axis=0)
            s = _dot_nt(qsub, k_ref[start:start + nk, :])
            v2 = v_ref[start:start + nk, :]
            one = jnp.ones_like(v2)
            lo_lanes = lax.broadcasted_iota(jnp.int32, (nk, LANES), 1) < HEAD_DIM
            v_ones = (jnp.where(lo_lanes, v2, one), jnp.where(lo_lanes, one, v2))
            if masked:
                rq = lax.broadcasted_iota(jnp.int32, (nr, nk), 0)
                rk = lax.broadcasted_iota(jnp.int32, (nr, nk), 1)
                causal = jnp.where(rk <= rq, 0.0, NEG)
            for hh in range(2):
                rows = slice(hh * tq + row0, (hh + 1) * tq)
                cs = gt_ref[pl.ds(G_FOX + 2 * p + hh, 1), start:start + nk] * LOG2E
                r = s[hh * nr:(hh + 1) * nr] * LOG2E - cs
                if masked:
                    r = r + causal
                ct = ct_ref[rows, :]
                m_prev = m_ref[rows, :]
                m_next = jnp.maximum(m_prev, jnp.max(r, axis=1, keepdims=True) + ct)
                pr = jnp.exp2(r - jnp.tile(m_next - ct, (1, nk // LANES)))
                alpha = jnp.exp2(m_prev - m_next)
                acc_ref[rows, :] = alpha * acc_ref[rows, :] + _dot(pr.astype(BF16), v_ones[hh])
                m_ref[rows, :] = m_next

        for j in range(blk):
            group(j * tq, tq, 0, False)
        group(blk * tq, half, 0, True)
        group(blk * tq + half, half, half, True)
        acc = acc_ref[...]
        o_ref[qrows, :] = _unstack_heads(acc / pltpu.roll(acc, HEAD_DIM, 1), tq).astype(BF16)

    for blk in range(nq):
        sweep(blk)


def _fox(u, g, gt, bsz, s, tq=512):
    nq = s // tq
    stat = pltpu.VMEM((nq, 2 * tq, LANES), F32)
    return pl.pallas_call(
        functools.partial(_fox_kernel, tq=tq, nq=nq),
        grid=(bsz, FOX_HEADS // 2),
        in_specs=[
            pl.BlockSpec((s, PAIR), lambda b, p: (b, FOX_Q + p)),
            pl.BlockSpec((s, PAIR), lambda b, p: (b, FOX_K + p)),
            pl.BlockSpec((s, PAIR), lambda b, p: (b, FOX_V + p)),
            pl.BlockSpec((s, N_GATE), lambda b, p: (b, 0)),
            pl.BlockSpec((None, 8, s), lambda b, p: (b, 0, 0)),
        ],
        out_specs=pl.BlockSpec((s, PAIR), lambda b, p: (b, p)),
        out_shape=jax.ShapeDtypeStruct((bsz * s, D_FOX), BF16),
        scratch_shapes=[stat, stat, stat],
        compiler_params=_cparams("parallel", "parallel"),
        name="fox",
    )(u, u, u, g, gt)


def _sb_kernel(q_ref, k_ref, v_ref, o_ref, qs_all, run_all, acc_all, *, tq, tg, nq):
    n_diag = tq // tg

    def sweep(blk):
        qrows = slice(blk * tq, (blk + 1) * tq)
        qs_ref, run_ref, acc_ref = qs_all.at[blk], run_all.at[blk], acc_all.at[blk]
        qs_ref[...] = _stack_heads(q_ref[qrows, :] * jnp.asarray(HEAD_DIM ** -0.5, BF16))
        r = lax.broadcasted_iota(jnp.int32, (2 * SB_SUB, 2 * SB_SUB), 0)
        c = lax.broadcasted_iota(jnp.int32, (2 * SB_SUB, 2 * SB_SUB), 1)
        suffix = jnp.where((c >= SB_SUB) | (jnp.where(r >= SB_SUB, r - SB_SUB, r) >= c), 1.0, 0.0).astype(BF16)
        run_ref[...] = jnp.zeros_like(run_ref)
        acc_ref[...] = jnp.zeros_like(acc_ref)

        def group(g, diag):
            row0 = tq - (g + 1) * tg if diag else 0
            nr = tq - row0
            heads = lambda ref: ref[...] if row0 == 0 else jnp.concatenate([ref[row0:tq, :], ref[tq + row0:, :]], axis=0)
            start = (blk + 1) * tq - (g + 1) * tg
            if not isinstance(start, int):
                start = pl.multiple_of(start, tg)
            z = _dot_nt(heads(qs_ref), k_ref[pl.ds(start, tg), :])
            sp = jnp.maximum(z, 0.0) + jnp.log(1.0 + jnp.exp(-jnp.abs(z)))
            if diag:
                rq = lax.broadcasted_iota(jnp.int32, (2 * nr, tg), 0)
                rk = lax.broadcasted_iota(jnp.int32, (2 * nr, tg), 1)
                valid = rk < jnp.where(rq >= nr, rq - nr, rq)
                sp = jnp.where(valid, sp, 0.0)
            hi = sp.astype(BF16)
            lo = (sp - hi.astype(F32)).astype(BF16)
            offs = heads(run_ref)
            parts = [None] * (tg // SB_SUB)
            for sb in reversed(range(tg // SB_SUB)):
                sl = slice(sb * SB_SUB, (sb + 1) * SB_SUB)
                rs = _dot(jnp.concatenate([hi[:, sl], lo[:, sl]], axis=1), suffix)
                a = jnp.exp(z[:, sl] - rs[:, :SB_SUB] - offs)
                if diag:
                    a = jnp.where(valid[:, sl], a, 0.0)
                parts[sb] = a.astype(BF16)
                offs = offs + rs[:, SB_SUB:]
            pv = _dot(jnp.concatenate(parts, axis=1), v_ref[pl.ds(start, tg), :])
            for hh in range(2):
                rows = slice(hh * tq + row0, (hh + 1) * tq)
                run_ref[rows, :] = offs[hh * nr:(hh + 1) * nr]
                acc_ref[rows, :] += pv[hh * nr:(hh + 1) * nr]
            return jnp.min(offs)

        for g in range(n_diag):
            low = group(g, True)
        n_groups = (blk + 1) * n_diag
        first = n_diag
        if blk > 0:
            low = group(n_diag, False)
            first += 1
        if first < n_groups:
            def cond(carry):
                g, low = carry
                return (g < n_groups) & (low < SB_EXIT)

            def body(carry):
                g, _ = carry
                return g + 1, group(g, False)

            lax.while_loop(cond, body, (jnp.int32(first), low))
        o_ref[qrows, :] = _unstack_heads(acc_ref[...], tq).astype(BF16)

    for blk in range(nq):
        sweep(blk)


def _sb(u, bsz, s, tq=512, tg=256):
    nq = s // tq
    stat = pltpu.VMEM((nq, 2 * tq, LANES), F32)
    return pl.pallas_call(
        functools.partial(_sb_kernel, tq=tq, tg=tg, nq=nq),
        grid=(bsz, SB_HEADS // 2),
        in_specs=[
            pl.BlockSpec((s, PAIR), lambda b, p: (b, SB_Q + p)),
            pl.BlockSpec((s, PAIR), lambda b, p: (b, SB_K + p)),
            pl.BlockSpec((s, PAIR), lambda b, p: (b, SB_V + p)),
        ],
        out_specs=pl.BlockSpec((s, PAIR), lambda b, p: (b, p)),
        out_shape=jax.ShapeDtypeStruct((bsz * s, D_SB), BF16),
        scratch_shapes=[pltpu.VMEM((nq, 2 * tq, PAIR), BF16), stat, stat],
        compiler_params=_cparams("parallel", "parallel"),
        name="sb",
    )(u, u, u)


def _mlstm_kernel(xq_ref, xk_ref, v_ref, og_ref, g_ref, gt_ref, cw_ref, nw_ref, o_ref, qs_ref, ks_ref, e_ref, *, s):
    L = ML_CHUNK
    nc = s // L
    TAIL = 16
    row = lax.broadcasted_iota(jnp.int32, (TAIL, D_ML), 0)

    def conv_chunk(c, _):
        start = pl.multiple_of(c * L, L)
        prev_start = pl.multiple_of(jnp.maximum(c - 1, 0) * L, L)
        for x_ref, dst_ref, w0, scale in ((xq_ref, qs_ref, 0, 1.0), (xk_ref, ks_ref, D_ML, HEAD_DIM ** -0.5)):
            cur = x_ref[pl.ds(start, L), :].astype(F32)
            tail = x_ref[pl.ds(prev_start + L - TAIL, TAIL), :].astype(F32)
            tail = jnp.where(c > 0, tail, 0.0)
            w = cw_ref[:, w0:w0 + D_ML]
            y = cur * w[CONV_WIDTH - 1:CONV_WIDTH, :]
            for k in range(1, CONV_WIDTH):
                sh = pltpu.roll(cur, k, 0)
                head = jnp.where(row < k, pltpu.roll(tail, k, 0), sh[:TAIL])
                sh = jnp.concatenate([head, sh[TAIL:]], axis=0)
                y = y + sh * w[CONV_WIDTH - 1 - k:CONV_WIDTH - k, :]
            y = y * jax.nn.sigmoid(y) * scale
            dst_ref[pl.ds(start, L), :] = y.astype(BF16)
        return 0

    conv_chunk(0, 0)

    ci = lax.broadcasted_iota(jnp.int32, (LANES, N_EXP), 0)
    cj = lax.broadcasted_iota(jnp.int32, (LANES, N_EXP), 1)
    src_col = jnp.where(cj < EXP_B, G_MLP + lax.shift_right_logical(cj, 7),
                        jnp.where(cj < EXP_A, G_MLB + lax.shift_right_logical(cj - EXP_B, 6),
                                  G_MLA + lax.shift_right_logical(cj - EXP_A, 6)))
    e_ref[...] = jnp.where(ci == src_col, 1.0, 0.0).astype(BF16)

    lane = lax.broadcasted_iota(jnp.int32, (L, LANES), 1)
    lo_lanes = lane < HEAD_DIM
    r_ll = lax.broadcasted_iota(jnp.int32, (L, L), 0)
    c_ll = lax.broadcasted_iota(jnp.int32, (L, L), 1)
    tri = r_ll >= c_ll
    blockdiag = (r_ll < HEAD_DIM) == (c_ll < HEAD_DIM)
    ones = jnp.ones((L, LANES), BF16)

    def chunk(c, carry):
        conv_chunk(jnp.minimum(c + 1, nc - 1), 0)
        start = pl.multiple_of(c * L, L)
        rep = _dot_exact(g_ref[pl.ds(start, L), :], e_ref[...], lhs_exact=False)
        gtc = gt_ref[:, pl.ds(start, L)]
        new_carry = []
        for p in range(ML_HEADS // 2):
            c2, nm, m2 = carry[p]
            sl = slice(p * PAIR, (p + 1) * PAIR)
            q2 = qs_ref[pl.ds(start, L), sl]
            k2 = ks_ref[pl.ds(start, L), sl]
            v_one = jnp.concatenate([v_ref[pl.ds(start, L), sl], ones], axis=1)
            b2 = rep[:, EXP_B + p * PAIR:EXP_B + (p + 1) * PAIR]
            a2 = rep[:, EXP_A + p * PAIR:EXP_A + (p + 1) * PAIR]
            pm = [rep[:, (2 * p + hh) * LANES:(2 * p + hh + 1) * LANES] for hh in range(2)]
            pm2 = jnp.where(lo_lanes, pm[0], pm[1])
            qk = _dot_nt(_stack_heads(q2), k2)
            sc = []
            for hh in range(2):
                a_row = gtc[G_MLA + 2 * p + hh:G_MLA + 2 * p + hh + 1, :]
                sc.append(qk[hh * L:(hh + 1) * L] * jnp.exp(jnp.where(tri, a_row - pm[hh], -jnp.inf)))
            r = _dot(jnp.concatenate(sc, axis=0).astype(BF16), v_one)
            num_loc = jnp.where(lo_lanes, r[:L, :LANES], r[L:, :LANES])
            den_loc = jnp.where(lo_lanes, r[:L, LANES:], r[L:, LANES:])
            m_loc = b2 + pm2
            inter = _dot(q2, jnp.concatenate([c2, nm], axis=1).astype(BF16))
            it2 = b2 + m2
            m_t = jnp.maximum(it2, m_loc)
            f_loc = jnp.exp(m_loc - m_t)
            w_int = jnp.exp(it2 - m_t)
            num = num_loc * f_loc + w_int * inter[:, :LANES]
            den = den_loc * f_loc + w_int * inter[:, LANES:]
            hp = num / jnp.maximum(jnp.abs(den), jnp.exp(-m_t))
            sq = hp * hp
            ms0 = jnp.sum(jnp.where(lo_lanes, sq, 0.0), axis=1, keepdims=True)
            ms1 = jnp.sum(jnp.where(lo_lanes, 0.0, sq), axis=1, keepdims=True)
            msq = jnp.where(lo_lanes, ms0, ms1) * (1.0 / HEAD_DIM)
            hp = hp * lax.rsqrt(msq + RMS_EPS) * nw_ref[:, sl]
            gate = jax.nn.sigmoid(og_ref[pl.ds(start, L), sl].astype(F32))
            o_ref[pl.ds(start, L), sl] = (gate * hp).astype(BF16)
            g2 = b2[L - 1:L, :]
            pl2 = pm2[L - 1:L, :]
            kw = (k2.astype(F32) * jnp.exp(a2 - pl2)).astype(BF16)
            upd = _dot_tn(kw, v_one)
            m_new = jnp.maximum(g2 + m2, g2 + pl2)
            decay = jnp.exp(g2 + m2 - m_new)
            scale = jnp.exp(g2 + pl2 - m_new)
            c2 = decay * c2 + scale * jnp.where(blockdiag, upd[:, :LANES], 0.0)
            nm = decay * nm + scale * jnp.where(blockdiag, upd[:, LANES:], 0.0)
            new_carry.append((c2, nm, m_new))
        return tuple(new_carry)

    init = tuple((jnp.zeros((LANES, LANES), F32), jnp.zeros((LANES, LANES), F32), jnp.zeros((1, LANES), F32))
                 for _ in range(ML_HEADS // 2))
    lax.fori_loop(0, nc, chunk, init, unroll=4)


def _mlstm(u, g, gt, conv_w, norm_w, bsz, s):
    blk = lambda j: pl.BlockSpec((s, D_ML), lambda b: (b, j))
    first = (3 * D_FOX + 3 * D_SB) // D_ML
    return pl.pallas_call(
        functools.partial(_mlstm_kernel, s=s),
        grid=(bsz,),
        in_specs=[
            blk(first), blk(first + 1), blk(first + 2), blk(first + 3),
            pl.BlockSpec((s, N_GATE), lambda b: (b, 0)),
            pl.BlockSpec((None, GT_ROWS, s), lambda b: (b, 0, 0)),
            pl.BlockSpec((CONV_WIDTH, 2 * D_ML), lambda b: (0, 0)),
            pl.BlockSpec((1, D_ML), lambda b: (0, 0)),
        ],
        out_specs=pl.BlockSpec((s, D_ML), lambda b: (b, 0)),
        out_shape=jax.ShapeDtypeStruct((bsz * s, D_ML), BF16),
        scratch_shapes=[pltpu.VMEM((s, D_ML), BF16), pltpu.VMEM((s, D_ML), BF16), pltpu.VMEM((LANES, N_EXP), BF16)],
        compiler_params=_cparams("parallel"),
        name="mlstm",
    )(u, u, u, u, g, gt, conv_w, norm_w)


def _memkv_kernel(x_ref, nw_ref, w_ref, o_ref):
    xn = _rms(x_ref[...], nw_ref[...]).astype(BF16)
    o_ref[...] = _dot(xn, w_ref[...]).astype(BF16)


def _memkv(mem2d, nw, w, tm=512):
    t = mem2d.shape[0]
    depth = w.shape[0]
    return pl.pallas_call(
        _memkv_kernel,
        grid=(depth, t // tm),
        in_specs=[
            pl.BlockSpec((tm, D_MODEL), lambda l, i: (i, 0)),
            pl.BlockSpec((None, 1, D_MODEL), lambda l, i: (l, 0, 0)),
            pl.BlockSpec((None, D_MODEL, 2 * D_X), lambda l, i: (l, 0, 0)),
        ],
        out_specs=pl.BlockSpec((None, tm, 2 * D_X), lambda l, i: (l, i, 0)),
        out_shape=jax.ShapeDtypeStruct((depth, t, 2 * D_X), BF16),
        compiler_params=_cparams("parallel", "parallel"),
        name="memkv",
    )(mem2d, nw, w)


def _cross_kernel(yf_ref, ys_ref, ym_ref, wo_ref, h_ref, nw_ref, wq_ref, kv_ref, wxo_ref, o_ref):
    hf = (h_ref[...] + _dot(yf_ref[...], wo_ref[:D_FOX, :]) + _dot(ys_ref[...], wo_ref[D_FOX:D_FOX + D_SB, :])
          + _dot(ym_ref[...], wo_ref[D_FOX + D_SB:, :]))
    hn = _rms(hf, nw_ref[...]).astype(BF16)
    q = _dot(hn, wq_ref[...]).astype(BF16)
    outs = []
    for hd in range(X_HEADS):
        sl = slice(hd * X_HEAD_DIM, (hd + 1) * X_HEAD_DIM)
        k = kv_ref[:, sl]
        v = kv_ref[:, D_X + hd * X_HEAD_DIM:D_X + (hd + 1) * X_HEAD_DIM]
        s = _dot_nt(q[:, sl], k) * (X_HEAD_DIM ** -0.5)
        m = jnp.max(s, axis=1, keepdims=True)
        e = jnp.exp(s - m)
        pr = e / jnp.sum(e, axis=1, keepdims=True)
        outs.append(_dot(pr.astype(BF16), v).astype(BF16))
    o = jnp.concatenate(outs, axis=1)
    o_ref[...] = hf + _dot(o, wxo_ref[...])


def _cross(yf, ys, ym, wo, h, nw, wq, kv, wxo, layer, bsz, s, mlen, tm=1024):
    nt = s // tm
    row = lambda n: pl.BlockSpec((tm, n), lambda b, i: (b * nt + i, 0))
    wgt = lambda r, c: pl.BlockSpec((None, r, c), lambda b, i: (layer, 0, 0))
    return pl.pallas_call(
        _cross_kernel,
        grid=(bsz, nt),
        in_specs=[
            row(D_FOX), row(D_SB), row(D_ML), wgt(D_MODEL, D_MODEL), row(D_MODEL),
            pl.BlockSpec((1, D_MODEL), lambda b, i: (0, 0)),
            wgt(D_MODEL, D_X),
            pl.BlockSpec((None, mlen, 2 * D_X), lambda b, i: (layer, b, 0)),
            wgt(D_X, D_MODEL),
        ],
        out_specs=row(D_MODEL),
        out_shape=jax.ShapeDtypeStruct((bsz * s, D_MODEL), F32),
        compiler_params=_cparams("parallel", "parallel"),
        name="cross",
    )(yf, ys, ym, wo, h, nw, wq, kv, wxo)


def _ffn_kernel(h_ref, nw_ref, wg_ref, wu_ref, wd_ref, fw_ref, o_ref, acc_ref, *, f_chunk, final_norm):
    hf = h_ref[...]
    hn = _rms(hf, nw_ref[...]).astype(BF16)
    for c in range(D_FF // f_chunk):
        sl = slice(c * f_chunk, (c + 1) * f_chunk)
        g = _dot(hn, wg_ref[:, sl].astype(BF16))
        u = _dot(hn, wu_ref[:, sl].astype(BF16))
        a = (g * jax.nn.sigmoid(g) * u).astype(BF16)
        part = _dot(a, wd_ref[sl, :].astype(BF16))
        if c == 0:
            acc_ref[...] = part
        else:
            acc_ref[...] += part
    out = hf + acc_ref[...]
    if final_norm:
        out = _rms(out, fw_ref[...])
    o_ref[...] = out


def _ffn(h, nw, wg, wu, wd, fw, layer, final_norm, tm=512):
    t = h.shape[0]
    const = lambda shape: pl.BlockSpec(shape, lambda i: (0, 0))
    wgt = lambda r, c: pl.BlockSpec((None, r, c), lambda i: (layer, 0, 0), pipeline_mode=pl.Buffered(1))
    return pl.pallas_call(
        functools.partial(_ffn_kernel, f_chunk=256, final_norm=final_norm),
        grid=(t // tm,),
        in_specs=[
            pl.BlockSpec((tm, D_MODEL), lambda i: (i, 0)),
            const((1, D_MODEL)),
            wgt(D_MODEL, D_FF), wgt(D_MODEL, D_FF), wgt(D_FF, D_MODEL),
            const((1, D_MODEL)),
        ],
        out_specs=pl.BlockSpec((tm, D_MODEL), lambda i: (i, 0)),
        out_shape=jax.ShapeDtypeStruct((t, D_MODEL), F32),
        scratch_shapes=[pltpu.VMEM((tm, D_MODEL), F32)],
        compiler_params=_cparams("parallel"),
        name="ffn",
    )(h, nw, wg, wu, wd, fw)


N_FOX_QKV = 3 * D_FOX
N_REST = 3 * D_SB + 4 * D_ML


N_USED_GATES = FOX_HEADS + 2 * ML_HEADS


def _regroup_kernel(w_ref, o_ref):
    o_ref[:, :N_FOX_QKV] = w_ref[:, :N_FOX_QKV].astype(BF16)
    rest = w_ref[:, N_FOX_QKV + FOX_HEADS:N_FOX_QKV + FOX_HEADS + N_REST]
    o_ref[:, N_FOX_QKV:D_MAIN] = rest.astype(BF16)
    lane = lax.broadcasted_iota(jnp.int32, (w_ref.shape[0], N_USED_GATES), 1)
    gates = jnp.where(lane < FOX_HEADS, w_ref[:, N_FOX_QKV:N_FOX_QKV + N_USED_GATES], w_ref[:, D_MAIN:D_MAIN + N_USED_GATES])
    o_ref[:, D_MAIN:] = jnp.zeros((w_ref.shape[0], N_GATE), BF16)
    o_ref[:, D_MAIN:D_MAIN + N_USED_GATES] = gates.astype(BF16)


def _regroup_w_in(w, tr=256):
    depth, rows, n_in = w.shape
    assert N_FOX_QKV + N_REST == D_MAIN and D_MAIN % LANES == 0 and n_in == D_MAIN + N_USED_GATES
    return pl.pallas_call(
        _regroup_kernel,
        grid=(depth, rows // tr),
        in_specs=[pl.BlockSpec((None, tr, n_in), lambda l, i: (l, i, 0))],
        out_specs=pl.BlockSpec((None, tr, D_MAIN + N_GATE), lambda l, i: (l, i, 0)),
        out_shape=jax.ShapeDtypeStruct((depth, rows, D_MAIN + N_GATE), BF16),
        compiler_params=_cparams("parallel", "parallel"),
        name="regroup",
    )(w)


def kernel(x, mem, norm_mix_w, w_in, fox_f_b, ml_conv_w, ml_i_b, ml_f_b, ml_norm_w, w_out, norm_x_w, mem_norm_w,
           wx_q, wx_kv, wx_o, norm_ffn_w, w_gate, w_up, w_down, final_norm_w):
    bsz, s, _ = x.shape
    mlen = mem.shape[1]
    depth = w_in.shape[0]
    h = x.reshape(bsz * s, D_MODEL)
    mem2d = mem.reshape(bsz * mlen, D_MODEL)
    row = lambda v: v.reshape(1, -1)
    w_all = _regroup_w_in(w_in)
    bias = jnp.concatenate([fox_f_b, ml_i_b, ml_f_b, jnp.zeros((depth, N_GATE - 3 * FOX_HEADS), F32)], axis=1)
    wo, wq, wkv, wxo = (w.astype(BF16) for w in (w_out, wx_q, wx_kv, wx_o))
    kv = _memkv(mem2d, mem_norm_w.reshape(depth, 1, D_MODEL), wkv, tm=min(512, bsz * mlen))
    for l in range(depth):
        u, g, gt = _inproj(h, row(norm_mix_w[l]), w_all, row(bias[l]), l, bsz, s)
        y_fox = _fox(u, g, gt, bsz, s)
        y_sb = _sb(u, bsz, s)
        y_ml = _mlstm(u, g, gt, ml_conv_w[l], row(ml_norm_w[l]), bsz, s)
        h = _cross(y_fox, y_sb, y_ml, wo, h, row(norm_x_w[l]), wq, kv, wxo, l, bsz, s, mlen)
        h = _ffn(h, row(norm_ffn_w[l]), w_gate, w_up, w_down, row(final_norm_w), l, final_norm=(l == depth - 1))
    return h.reshape(bsz, s, D_MODEL)
```

```python
import functools

import jax
import jax.numpy as jnp
from jax import lax
from jax.experimental import pallas as pl
from jax.experimental.pallas import tpu as pltpu

D_MODEL = 1024
HEAD_DIM = 64
FOX_HEADS = 6
SB_HEADS = 4
ML_HEADS = 6
D_FOX = FOX_HEADS * HEAD_DIM
D_SB = SB_HEADS * HEAD_DIM
D_ML = ML_HEADS * HEAD_DIM
ML_CHUNK = 128
CONV_WIDTH = 4
X_HEADS = 4
X_HEAD_DIM = 128
D_X = X_HEADS * X_HEAD_DIM
D_FF = 2816
RMS_EPS = 1e-6

LANES = 128
PAIR = 2 * HEAD_DIM
D_MAIN = 3 * D_FOX + 3 * D_SB + 4 * D_ML
N_GATE = LANES
GT_ROWS = 32
NEG = -1e30
SB_SUB = 128
SB_EXIT = 110.0
LOG2E = 1.4426950408889634
VMEM_LIMIT = 56 * 1024 * 1024

BF16 = jnp.bfloat16
F32 = jnp.float32

FOX_Q, FOX_K, FOX_V = 0, 3, 6
SB_Q, SB_K, SB_V = 9, 11, 13
G_FOX, G_MLA, G_MLB, G_MLP = 0, 6, 12, 18
EXP_B = ML_HEADS * LANES
EXP_A = EXP_B + D_ML
N_EXP = EXP_A + D_ML


def _cparams(*sem):
    return pltpu.CompilerParams(dimension_semantics=sem, vmem_limit_bytes=VMEM_LIMIT)


def _rms(xf, w):
    ms = jnp.mean(xf * xf, axis=-1, keepdims=True)
    return xf * lax.rsqrt(ms + RMS_EPS) * w


def _dot(a, b):
    return jnp.dot(a, b, preferred_element_type=F32)


def _dot_nt(a, b):
    return lax.dot_general(a, b, (((1,), (1,)), ((), ())), preferred_element_type=F32)


def _dot_tn(a, b):
    return lax.dot_general(a, b, (((0,), (0,)), ((), ())), preferred_element_type=F32)


def _log_sigmoid(x):
    return jnp.minimum(x, 0.0) - jnp.log(1.0 + jnp.exp(-jnp.abs(x)))


def _split3(x):
    x1 = x.astype(BF16)
    r1 = x - x1.astype(F32)
    x2 = r1.astype(BF16)
    return x1, x2, (r1 - x2.astype(F32)).astype(BF16)


def _dot_exact(a, b, lhs_exact):
    if lhs_exact:
        x1, x2, x3 = _split3(b)
        return _dot(a, x1) + _dot(a, x2) + _dot(a, x3)
    x1, x2, x3 = _split3(a)
    return _dot(x1, b) + _dot(x2, b) + _dot(x3, b)


def _gate_chunk(pre, carry, tri):
    L = ML_CHUNK
    col = lax.broadcasted_iota(jnp.int32, (L, LANES), 1)
    row = lax.broadcasted_iota(jnp.int32, (L, LANES), 0)
    is_a = (col >= G_MLA) & (col < G_MLB)
    cum = _dot_exact(tri, jnp.where(is_a, 0.0, _log_sigmoid(pre)), lhs_exact=True)
    tot = cum + carry
    a = pre - pltpu.roll(cum, LANES - (G_MLB - G_MLA), 1)
    pm = jnp.where(is_a, a, -jnp.inf)
    k = 1
    while k < L:
        pm = jnp.maximum(pm, jnp.where(row >= k, pltpu.roll(pm, k, 0), -jnp.inf))
        k *= 2
    pm = pltpu.roll(pm, G_MLP - G_MLA, 1)
    out = jnp.where(col < G_MLA, tot,
                    jnp.where(is_a, a, jnp.where(col < G_MLP, cum, jnp.where(col < G_MLP + ML_HEADS, pm, 0.0))))
    return out, tot[L - 1:L, :]


ML_QK0 = 3 * D_FOX + 3 * D_SB
CONV_TAIL = 8


def _conv_silu(x, tail, w, scale):
    row = lax.broadcasted_iota(jnp.int32, tail.shape, 0)
    y = x * w[CONV_WIDTH - 1:CONV_WIDTH, :]
    for k in range(1, CONV_WIDTH):
        sh = pltpu.roll(x, k, 0)
        head = jnp.where(row < k, pltpu.roll(tail, k, 0), sh[:CONV_TAIL])
        sh = jnp.concatenate([head, sh[CONV_TAIL:]], axis=0)
        y = y + sh * w[CONV_WIDTH - 1 - k:CONV_WIDTH - k, :]
    return y * jax.nn.sigmoid(y) * scale


def _inproj_kernel(x_ref, nw_ref, w_ref, b_ref, cw_ref, u_ref, go_ref, gt_ref, carry_ref, tail_ref, *,
                   n_chunk, steps_per_seq):
    @pl.when(pl.program_id(0) % steps_per_seq == 0)
    def _():
        carry_ref[...] = jnp.zeros_like(carry_ref)
        tail_ref[...] = jnp.zeros_like(tail_ref)

    tm = x_ref.shape[0]
    xn = _rms(x_ref[...], nw_ref[...]).astype(BF16)
    n_all = D_MAIN + N_GATE
    qk_col = lax.broadcasted_iota(jnp.int32, (1, 2 * D_ML), 1)
    qk_scale = jnp.where(qk_col < D_ML, 1.0, HEAD_DIM ** -0.5)
    for c in reversed(range(n_all // n_chunk)):
        lo = c * n_chunk
        res = _dot(xn, w_ref[:, lo:lo + n_chunk])
        a, b = max(lo, ML_QK0), min(lo + n_chunk, ML_QK0 + 2 * D_ML)
        if a < b:
            seg = res[:, a - lo:b - lo]
            cols = slice(a - ML_QK0, b - ML_QK0)
            conv = _conv_silu(seg, tail_ref[:, cols], cw_ref[:, cols], qk_scale[:, cols])
            tail_ref[:, cols] = seg[tm - CONV_TAIL:, :]
            pieces = [res[:, :a - lo]] * (a > lo) + [conv] + [res[:, b - lo:]] * (b < lo + n_chunk)
            res = pieces[0] if len(pieces) == 1 else jnp.concatenate(pieces, axis=1)
        if lo + n_chunk <= D_MAIN:
            u_ref[:, lo:lo + n_chunk] = res.astype(BF16)
            continue
        u_ref[:, lo:D_MAIN] = res[:, :D_MAIN - lo].astype(BF16)
        pre_all = res[:, D_MAIN - lo:] + b_ref[...]
        L = ML_CHUNK
        r = lax.broadcasted_iota(jnp.int32, (L, L), 0)
        c = lax.broadcasted_iota(jnp.int32, (L, L), 1)
        tri = jnp.where(r >= c, 1.0, 0.0).astype(BF16)
        carry = carry_ref[...]
        for ci in range(tm // L):
            out, carry = _gate_chunk(pre_all[ci * L:(ci + 1) * L], carry, tri)
            go_ref[ci * L:(ci + 1) * L, :] = out
            gt_ref[:, ci * L:(ci + 1) * L] = out.T[:GT_ROWS, :]
        carry_ref[...] = carry


def _inproj(h, nw, w_all, bias, conv_w, layer, bsz, s, tm=1024):
    t = h.shape[0]
    steps = s // tm
    return pl.pallas_call(
        functools.partial(_inproj_kernel, n_chunk=512, steps_per_seq=steps),
        grid=(t // tm,),
        in_specs=[
            pl.BlockSpec((tm, D_MODEL), lambda i: (i, 0)),
            pl.BlockSpec((1, D_MODEL), lambda i: (0, 0)),
            pl.BlockSpec((None, D_MODEL, D_MAIN + N_GATE), lambda i: (layer, 0, 0)),
            pl.BlockSpec((1, N_GATE), lambda i: (0, 0)),
            pl.BlockSpec((CONV_WIDTH, 2 * D_ML), lambda i: (0, 0)),
        ],
        out_specs=[
            pl.BlockSpec((tm, D_MAIN), lambda i: (i, 0)),
            pl.BlockSpec((tm, N_GATE), lambda i: (i, 0)),
            pl.BlockSpec((None, GT_ROWS, tm), lambda i: (i // steps, 0, i % steps)),
        ],
        out_shape=[jax.ShapeDtypeStruct((t, D_MAIN), BF16), jax.ShapeDtypeStruct((t, N_GATE), F32),
                   jax.ShapeDtypeStruct((bsz, GT_ROWS, s), F32)],
        scratch_shapes=[pltpu.VMEM((1, N_GATE), F32), pltpu.VMEM((CONV_TAIL, 2 * D_ML), F32)],
        compiler_params=_cparams("arbitrary"),
        name="inproj",
    )(h, nw, w_all, bias, conv_w)


def _head_masks(q2):
    lane = lax.broadcasted_iota(jnp.int32, q2.shape, 1)
    zero = jnp.zeros_like(q2)
    return jnp.where(lane < HEAD_DIM, q2, zero), jnp.where(lane >= HEAD_DIM, q2, zero)


def _stack_heads(q2):
    return jnp.concatenate(_head_masks(q2), axis=0)


def _unstack_heads(x, rows):
    lane = lax.broadcasted_iota(jnp.int32, (rows, LANES), 1)
    return jnp.where(lane < HEAD_DIM, x[:rows], x[rows:])


def _fox_kernel(q_ref, k_ref, v_ref, g_ref, gt_ref, o_ref, ct_all, m_all, acc_all, *, tq, nq):
    p = pl.program_id(1)
    half = tq // 2

    def sweep(blk):
        qrows = slice(blk * tq, (blk + 1) * tq)
        ct_ref, m_ref, acc_ref = ct_all.at[blk], m_all.at[blk], acc_all.at[blk]
        qs = _stack_heads(q_ref[qrows, :] * jnp.asarray(HEAD_DIM ** -0.5, BF16))
        gq = g_ref[qrows, :]
        lane = lax.broadcasted_iota(jnp.int32, (tq, LANES), 1)
        for hh in range(2):
            ct = jnp.sum(jnp.where(lane == G_FOX + 2 * p + hh, gq, 0.0), axis=1, keepdims=True)
            ct_ref[hh * tq:(hh + 1) * tq, :] = jnp.broadcast_to(ct * LOG2E, (tq, LANES))
        m_ref[...] = jnp.full_like(m_ref, NEG)
        acc_ref[...] = jnp.zeros_like(acc_ref)

        def group(start, nk, row0, masked):
            nr = tq - row0
            qsub = qs if row0 == 0 else jnp.concatenate([qs[row0:tq], qs[tq + row0:]], axis=0)
            s = _dot_nt(qsub, k_ref[start:start + nk, :])
            v2 = v_ref[start:start + nk, :]
            one = jnp.ones_like(v2)
            lo_lanes = lax.broadcasted_iota(jnp.int32, (nk, LANES), 1) < HEAD_DIM
            v_ones = (jnp.where(lo_lanes, v2, one), jnp.where(lo_lanes, one, v2))
            if masked:
                rq = lax.broadcasted_iota(jnp.int32, (nr, nk), 0)
                rk = lax.broadcasted_iota(jnp.int32, (nr, nk), 1)
                causal = jnp.where(rk <= rq, 0.0, NEG)
            for hh in range(2):
                rows = slice(hh * tq + row0, (hh + 1) * tq)
                cs = gt_ref[pl.ds(G_FOX + 2 * p + hh, 1), start:start + nk] * LOG2E
                r = s[hh * nr:(hh + 1) * nr] * LOG2E - cs
                if masked:
                    r = r + causal
                ct = ct_ref[rows, :]
                m_prev = m_ref[rows, :]
                m_next = jnp.maximum(m_prev, jnp.max(r, axis=1, keepdims=True) + ct)
                pr = jnp.exp2(r - jnp.tile(m_next - ct, (1, nk // LANES)))
                alpha = jnp.exp2(m_prev - m_next)
                acc_ref[rows, :] = alpha * acc_ref[rows, :] + _dot(pr.astype(BF16), v_ones[hh])
                m_ref[rows, :] = m_next

        for j in range(blk):
            group(j * tq, tq, 0, False)
        group(blk * tq, half, 0, True)
        group(blk * tq + half, half, half, True)
        acc = acc_ref[...]
        o_ref[qrows, :] = _unstack_heads(acc / pltpu.roll(acc, HEAD_DIM, 1), tq).astype(BF16)

    for blk in range(nq):
        sweep(blk)


def _fox(u, g, gt, bsz, s, tq=512):
    nq = s // tq
    stat = pltpu.VMEM((nq, 2 * tq, LANES), F32)
    return pl.pallas_call(
        functools.partial(_fox_kernel, tq=tq, nq=nq),
        grid=(bsz, FOX_HEADS // 2),
        in_specs=[
            pl.BlockSpec((s, PAIR), lambda b, p: (b, FOX_Q + p)),
            pl.BlockSpec((s, PAIR), lambda b, p: (b, FOX_K + p)),
            pl.BlockSpec((s, PAIR), lambda b, p: (b, FOX_V + p)),
            pl.BlockSpec((s, N_GATE), lambda b, p: (b, 0)),
            pl.BlockSpec((None, 8, s), lambda b, p: (b, 0, 0)),
        ],
        out_specs=pl.BlockSpec((s, PAIR), lambda b, p: (b, p)),
        out_shape=jax.ShapeDtypeStruct((bsz * s, D_FOX), BF16),
        scratch_shapes=[stat, stat, stat],
        compiler_params=_cparams("parallel", "parallel"),
        name="fox",
    )(u, u, u, g, gt)


def _sb_kernel(q_ref, k_ref, v_ref, o_ref, qs_all, run_all, acc_all, *, tq, tg, nq):
    n_diag = tq // tg

    def sweep(blk):
        qrows = slice(blk * tq, (blk + 1) * tq)
        qs_ref, run_ref, acc_ref = qs_all.at[blk], run_all.at[blk], acc_all.at[blk]
        qs_ref[...] = _stack_heads(q_ref[qrows, :] * jnp.asarray(HEAD_DIM ** -0.5, BF16))
        r = lax.broadcasted_iota(jnp.int32, (2 * SB_SUB, 2 * SB_SUB), 0)
        c = lax.broadcasted_iota(jnp.int32, (2 * SB_SUB, 2 * SB_SUB), 1)
        suffix = jnp.where((c >= SB_SUB) | (jnp.where(r >= SB_SUB, r - SB_SUB, r) >= c), 1.0, 0.0).astype(BF16)
        run_ref[...] = jnp.zeros_like(run_ref)
        acc_ref[...] = jnp.zeros_like(acc_ref)

        def group(g, diag):
            row0 = tq - (g + 1) * tg if diag else 0
            nr = tq - row0
            heads = lambda ref: ref[...] if row0 == 0 else jnp.concatenate([ref[row0:tq, :], ref[tq + row0:, :]], axis=0)
            start = (blk + 1) * tq - (g + 1) * tg
            if not isinstance(start, int):
                start = pl.multiple_of(start, tg)
            z = _dot_nt(heads(qs_ref), k_ref[pl.ds(start, tg), :])
            sp = jnp.maximum(z, 0.0) + jnp.log(1.0 + jnp.exp(-jnp.abs(z)))
            if diag:
                rq = lax.broadcasted_iota(jnp.int32, (2 * nr, tg), 0)
                rk = lax.broadcasted_iota(jnp.int32, (2 * nr, tg), 1)
                valid = rk < jnp.where(rq >= nr, rq - nr, rq)
                sp = jnp.where(valid, sp, 0.0)
            hi = sp.astype(BF16)
            lo = (sp - hi.astype(F32)).astype(BF16)
            offs = heads(run_ref)
            parts = [None] * (tg // SB_SUB)
            for sb in reversed(range(tg // SB_SUB)):
                sl = slice(sb * SB_SUB, (sb + 1) * SB_SUB)
                rs = _dot(jnp.concatenate([hi[:, sl], lo[:, sl]], axis=1), suffix)
                a = jnp.exp(z[:, sl] - rs[:, :SB_SUB] - offs)
                if diag:
                    a = jnp.where(valid[:, sl], a, 0.0)
                parts[sb] = a.astype(BF16)
                offs = offs + rs[:, SB_SUB:]
            pv = _dot(jnp.concatenate(parts, axis=1), v_ref[pl.ds(start, tg), :])
            for hh in range(2):
                rows = slice(hh * tq + row0, (hh + 1) * tq)
                run_ref[rows, :] = offs[hh * nr:(hh + 1) * nr]
                acc_ref[rows, :] += pv[hh * nr:(hh + 1) * nr]
            return jnp.min(offs)

        for g in range(n_diag):
            low = group(g, True)
        n_groups = (blk + 1) * n_diag
        first = n_diag
        if blk > 0:
            low = group(n_diag, False)
            first += 1
        if first < n_groups:
            def cond(carry):
                g, low = carry
                return (g < n_groups) & (low < SB_EXIT)

            def body(carry):
                g, _ = carry
                return g + 1, group(g, False)

            lax.while_loop(cond, body, (jnp.int32(first), low))
        o_ref[qrows, :] = _unstack_heads(acc_ref[...], tq).astype(BF16)

    for blk in range(nq):
        sweep(blk)


def _sb(u, bsz, s, tq=512, tg=256):
    nq = s // tq
    stat = pltpu.VMEM((nq, 2 * tq, LANES), F32)
    return pl.pallas_call(
        functools.partial(_sb_kernel, tq=tq, tg=tg, nq=nq),
        grid=(bsz, SB_HEADS // 2),
        in_specs=[
            pl.BlockSpec((s, PAIR), lambda b, p: (b, SB_Q + p)),
            pl.BlockSpec((s, PAIR), lambda b, p: (b, SB_K + p)),
            pl.BlockSpec((s, PAIR), lambda b, p: (b, SB_V + p)),
        ],
        out_specs=pl.BlockSpec((s, PAIR), lambda b, p: (b, p)),
        out_shape=jax.ShapeDtypeStruct((bsz * s, D_SB), BF16),
        scratch_shapes=[pltpu.VMEM((nq, 2 * tq, PAIR), BF16), stat, stat],
        compiler_params=_cparams("parallel", "parallel"),
        name="sb",
    )(u, u, u)


def _mlstm_kernel(q_ref, k_ref, v_ref, og_ref, g_ref, gt_ref, nw_ref, o_ref, e_ref, *, s):
    L = ML_CHUNK
    nc = s // L

    ci = lax.broadcasted_iota(jnp.int32, (LANES, N_EXP), 0)
    cj = lax.broadcasted_iota(jnp.int32, (LANES, N_EXP), 1)
    src_col = jnp.where(cj < EXP_B, G_MLP + lax.shift_right_logical(cj, 7),
                        jnp.where(cj < EXP_A, G_MLB + lax.shift_right_logical(cj - EXP_B, 6),
                                  G_MLA + lax.shift_right_logical(cj - EXP_A, 6)))
    e_ref[...] = jnp.where(ci == src_col, 1.0, 0.0).astype(BF16)

    lane = lax.broadcasted_iota(jnp.int32, (L, LANES), 1)
    lo_lanes = lane < HEAD_DIM
    r_ll = lax.broadcasted_iota(jnp.int32, (L, L), 0)
    c_ll = lax.broadcasted_iota(jnp.int32, (L, L), 1)
    tri = r_ll >= c_ll
    blockdiag = (r_ll < HEAD_DIM) == (c_ll < HEAD_DIM)
    ones = jnp.ones((L, LANES), BF16)

    def chunk(c, carry):
        start = pl.multiple_of(c * L, L)
        rep = _dot_exact(g_ref[pl.ds(start, L), :], e_ref[...], lhs_exact=False)
        gtc = gt_ref[:, pl.ds(start, L)]
        new_carry = []
        for p in range(ML_HEADS // 2):
            c2, nm, m2 = carry[p]
            sl = slice(p * PAIR, (p + 1) * PAIR)
            q2 = q_ref[pl.ds(start, L), sl]
            k2 = k_ref[pl.ds(start, L), sl]
            v_one = jnp.concatenate([v_ref[pl.ds(start, L), sl], ones], axis=1)
            b2 = rep[:, EXP_B + p * PAIR:EXP_B + (p + 1) * PAIR]
            a2 = rep[:, EXP_A + p * PAIR:EXP_A + (p + 1) * PAIR]
            pm = [rep[:, (2 * p + hh) * LANES:(2 * p + hh + 1) * LANES] for hh in range(2)]
            pm2 = jnp.where(lo_lanes, pm[0], pm[1])
            qk = _dot_nt(_stack_heads(q2), k2)
            sc = []
            for hh in range(2):
                a_row = gtc[G_MLA + 2 * p + hh:G_MLA + 2 * p + hh + 1, :]
                sc.append(qk[hh * L:(hh + 1) * L] * jnp.exp(jnp.where(tri, a_row - pm[hh], -jnp.inf)))
            r = _dot(jnp.concatenate(sc, axis=0).astype(BF16), v_one)
            num_loc = jnp.where(lo_lanes, r[:L, :LANES], r[L:, :LANES])
            den_loc = jnp.where(lo_lanes, r[:L, LANES:], r[L:, LANES:])
            m_loc = b2 + pm2
            inter = _dot(q2, jnp.concatenate([c2, nm], axis=1).astype(BF16))
            it2 = b2 + m2
            m_t = jnp.maximum(it2, m_loc)
            f_loc = jnp.exp(m_loc - m_t)
            w_int = jnp.exp(it2 - m_t)
            num = num_loc * f_loc + w_int * inter[:, :LANES]
            den = den_loc * f_loc + w_int * inter[:, LANES:]
            hp = num / jnp.maximum(jnp.abs(den), jnp.exp(-m_t))
            sq = hp * hp
            ms0 = jnp.sum(jnp.where(lo_lanes, sq, 0.0), axis=1, keepdims=True)
            ms1 = jnp.sum(jnp.where(lo_lanes, 0.0, sq), axis=1, keepdims=True)
            msq = jnp.where(lo_lanes, ms0, ms1) * (1.0 / HEAD_DIM)
            hp = hp * lax.rsqrt(msq + RMS_EPS) * nw_ref[:, sl]
            gate = jax.nn.sigmoid(og_ref[pl.ds(start, L), sl].astype(F32))
            o_ref[pl.ds(start, L), sl] = (gate * hp).astype(BF16)
            g2 = b2[L - 1:L, :]
            pl2 = pm2[L - 1:L, :]
            kw = (k2.astype(F32) * jnp.exp(a2 - pl2)).astype(BF16)
            upd = _dot_tn(kw, v_one)
            m_new = jnp.maximum(g2 + m2, g2 + pl2)
            decay = jnp.exp(g2 + m2 - m_new)
            scale = jnp.exp(g2 + pl2 - m_new)
            c2 = decay * c2 + scale * jnp.where(blockdiag, upd[:, :LANES], 0.0)
            nm = decay * nm + scale * jnp.where(blockdiag, upd[:, LANES:], 0.0)
            new_carry.append((c2, nm, m_new))
        return tuple(new_carry)

    init = tuple((jnp.zeros((LANES, LANES), F32), jnp.zeros((LANES, LANES), F32), jnp.zeros((1, LANES), F32))
                 for _ in range(ML_HEADS // 2))
    lax.fori_loop(0, nc, chunk, init, unroll=8)


def _mlstm(u, g, gt, norm_w, bsz, s):
    blk = lambda j: pl.BlockSpec((s, D_ML), lambda b: (b, j))
    first = (3 * D_FOX + 3 * D_SB) // D_ML
    return pl.pallas_call(
        functools.partial(_mlstm_kernel, s=s),
        grid=(bsz,),
        in_specs=[
            blk(first), blk(first + 1), blk(first + 2), blk(first + 3),
            pl.BlockSpec((s, N_GATE), lambda b: (b, 0)),
            pl.BlockSpec((None, GT_ROWS, s), lambda b: (b, 0, 0)),
            pl.BlockSpec((1, D_ML), lambda b: (0, 0)),
        ],
        out_specs=pl.BlockSpec((s, D_ML), lambda b: (b, 0)),
        out_shape=jax.ShapeDtypeStruct((bsz * s, D_ML), BF16),
        scratch_shapes=[pltpu.VMEM((LANES, N_EXP), BF16)],
        compiler_params=_cparams("parallel"),
        name="mlstm",
    )(u, u, u, u, g, gt, norm_w)


def _memkv_kernel(x_ref, nw_ref, w_ref, o_ref):
    xn = _rms(x_ref[...], nw_ref[...]).astype(BF16)
    o_ref[...] = _dot(xn, w_ref[...]).astype(BF16)


def _memkv(mem2d, nw, w, tm=512):
    t = mem2d.shape[0]
    depth = w.shape[0]
    return pl.pallas_call(
        _memkv_kernel,
        grid=(depth, t // tm),
        in_specs=[
            pl.BlockSpec((tm, D_MODEL), lambda l, i: (i, 0)),
            pl.BlockSpec((None, 1, D_MODEL), lambda l, i: (l, 0, 0)),
            pl.BlockSpec((None, D_MODEL, 2 * D_X), lambda l, i: (l, 0, 0)),
        ],
        out_specs=pl.BlockSpec((None, tm, 2 * D_X), lambda l, i: (l, i, 0)),
        out_shape=jax.ShapeDtypeStruct((depth, t, 2 * D_X), BF16),
        compiler_params=_cparams("parallel", "parallel"),
        name="memkv",
    )(mem2d, nw, w)


def _cross_kernel(yf_ref, ys_ref, ym_ref, wo_ref, h_ref, nw_ref, wq_ref, kv_ref, wxo_ref, o_ref):
    hf = (h_ref[...] + _dot(yf_ref[...], wo_ref[:D_FOX, :]) + _dot(ys_ref[...], wo_ref[D_FOX:D_FOX + D_SB, :])
          + _dot(ym_ref[...], wo_ref[D_FOX + D_SB:, :]))
    hn = _rms(hf, nw_ref[...]).astype(BF16)
    q = _dot(hn, wq_ref[...]).astype(BF16)
    outs = []
    for hd in range(X_HEADS):
        sl = slice(hd * X_HEAD_DIM, (hd + 1) * X_HEAD_DIM)
        k = kv_ref[:, sl]
        v = kv_ref[:, D_X + hd * X_HEAD_DIM:D_X + (hd + 1) * X_HEAD_DIM]
        s = _dot_nt(q[:, sl], k) * (X_HEAD_DIM ** -0.5)
        m = jnp.max(s, axis=1, keepdims=True)
        e = jnp.exp(s - m)
        pr = e / jnp.sum(e, axis=1, keepdims=True)
        outs.append(_dot(pr.astype(BF16), v).astype(BF16))
    o = jnp.concatenate(outs, axis=1)
    o_ref[...] = hf + _dot(o, wxo_ref[...])


def _cross(yf, ys, ym, wo, h, nw, wq, kv, wxo, layer, bsz, s, mlen, tm=1024):
    nt = s // tm
    row = lambda n: pl.BlockSpec((tm, n), lambda b, i: (b * nt + i, 0))
    wgt = lambda r, c: pl.BlockSpec((None, r, c), lambda b, i: (layer, 0, 0))
    return pl.pallas_call(
        _cross_kernel,
        grid=(bsz, nt),
        in_specs=[
            row(D_FOX), row(D_SB), row(D_ML), wgt(D_MODEL, D_MODEL), row(D_MODEL),
            pl.BlockSpec((1, D_MODEL), lambda b, i: (0, 0)),
            wgt(D_MODEL, D_X),
            pl.BlockSpec((None, mlen, 2 * D_X), lambda b, i: (layer, b, 0)),
            wgt(D_X, D_MODEL),
        ],
        out_specs=row(D_MODEL),
        out_shape=jax.ShapeDtypeStruct((bsz * s, D_MODEL), F32),
        compiler_params=_cparams("parallel", "parallel"),
        name="cross",
    )(yf, ys, ym, wo, h, nw, wq, kv, wxo)


def _ffn_kernel(h_ref, nw_ref, wg_ref, wu_ref, wd_ref, fw_ref, o_ref, acc_ref, *, f_chunk, final_norm):
    hf = h_ref[...]
    hn = _rms(hf, nw_ref[...]).astype(BF16)
    for c in range(D_FF // f_chunk):
        sl = slice(c * f_chunk, (c + 1) * f_chunk)
        g = _dot(hn, wg_ref[:, sl].astype(BF16))
        u = _dot(hn, wu_ref[:, sl].astype(BF16))
        a = (g * jax.nn.sigmoid(g) * u).astype(BF16)
        part = _dot(a, wd_ref[sl, :].astype(BF16))
        if c == 0:
            acc_ref[...] = part
        else:
            acc_ref[...] += part
    out = hf + acc_ref[...]
    if final_norm:
        out = _rms(out, fw_ref[...])
    o_ref[...] = out


def _ffn(h, nw, wg, wu, wd, fw, layer, final_norm, tm=512):
    t = h.shape[0]
    const = lambda shape: pl.BlockSpec(shape, lambda i: (0, 0))
    wgt = lambda r, c: pl.BlockSpec((None, r, c), lambda i: (layer, 0, 0), pipeline_mode=pl.Buffered(1))
    return pl.pallas_call(
        functools.partial(_ffn_kernel, f_chunk=256, final_norm=final_norm),
        grid=(t // tm,),
        in_specs=[
            pl.BlockSpec((tm, D_MODEL), lambda i: (i, 0)),
            const((1, D_MODEL)),
            wgt(D_MODEL, D_FF), wgt(D_MODEL, D_FF), wgt(D_FF, D_MODEL),
            const((1, D_MODEL)),
        ],
        out_specs=pl.BlockSpec((tm, D_MODEL), lambda i: (i, 0)),
        out_shape=jax.ShapeDtypeStruct((t, D_MODEL), F32),
        scratch_shapes=[pltpu.VMEM((tm, D_MODEL), F32)],
        compiler_params=_cparams("parallel"),
        name="ffn",
    )(h, nw, wg, wu, wd, fw)


N_FOX_QKV = 3 * D_FOX
N_REST = 3 * D_SB + 4 * D_ML


N_USED_GATES = FOX_HEADS + 2 * ML_HEADS


def _regroup_kernel(w_ref, o_ref):
    o_ref[:, :N_FOX_QKV] = w_ref[:, :N_FOX_QKV].astype(BF16)
    rest = w_ref[:, N_FOX_QKV + FOX_HEADS:N_FOX_QKV + FOX_HEADS + N_REST]
    o_ref[:, N_FOX_QKV:D_MAIN] = rest.astype(BF16)
    lane = lax.broadcasted_iota(jnp.int32, (w_ref.shape[0], N_USED_GATES), 1)
    gates = jnp.where(lane < FOX_HEADS, w_ref[:, N_FOX_QKV:N_FOX_QKV + N_USED_GATES], w_ref[:, D_MAIN:D_MAIN + N_USED_GATES])
    o_ref[:, D_MAIN:] = jnp.zeros((w_ref.shape[0], N_GATE), BF16)
    o_ref[:, D_MAIN:D_MAIN + N_USED_GATES] = gates.astype(BF16)


def _regroup_w_in(w, tr=256):
    depth, rows, n_in = w.shape
    assert N_FOX_QKV + N_REST == D_MAIN and D_MAIN % LANES == 0 and n_in == D_MAIN + N_USED_GATES
    return pl.pallas_call(
        _regroup_kernel,
        grid=(depth, rows // tr),
        in_specs=[pl.BlockSpec((None, tr, n_in), lambda l, i: (l, i, 0))],
        out_specs=pl.BlockSpec((None, tr, D_MAIN + N_GATE), lambda l, i: (l, i, 0)),
        out_shape=jax.ShapeDtypeStruct((depth, rows, D_MAIN + N_GATE), BF16),
        compiler_params=_cparams("parallel", "parallel"),
        name="regroup",
    )(w)


def kernel(x, mem, norm_mix_w, w_in, fox_f_b, ml_conv_w, ml_i_b, ml_f_b, ml_norm_w, w_out, norm_x_w, mem_norm_w,
           wx_q, wx_kv, wx_o, norm_ffn_w, w_gate, w_up, w_down, final_norm_w):
    bsz, s, _ = x.shape
    mlen = mem.shape[1]
    depth = w_in.shape[0]
    h = x.reshape(bsz * s, D_MODEL)
    mem2d = mem.reshape(bsz * mlen, D_MODEL)
    row = lambda v: v.reshape(1, -1)
    w_all = _regroup_w_in(w_in)
    bias = jnp.concatenate([fox_f_b, ml_i_b, ml_f_b, jnp.zeros((depth, N_GATE - 3 * FOX_HEADS), F32)], axis=1)
    wo, wq, wkv, wxo = (w.astype(BF16) for w in (w_out, wx_q, wx_kv, wx_o))
    kv = _memkv(mem2d, mem_norm_w.reshape(depth, 1, D_MODEL), wkv, tm=min(512, bsz * mlen))
    for l in range(depth):
        u, g, gt = _inproj(h, row(norm_mix_w[l]), w_all, row(bias[l]), ml_conv_w[l], l, bsz, s)
        y_fox = _fox(u, g, gt, bsz, s)
        y_sb = _sb(u, bsz, s)
        y_ml = _mlstm(u, g, gt, row(ml_norm_w[l]), bsz, s)
        h = _cross(y_fox, y_sb, y_ml, wo, h, row(norm_x_w[l]), wq, kv, wxo, l, bsz, s, mlen)
        h = _ffn(h, row(norm_ffn_w[l]), w_gate, w_up, w_down, row(final_norm_w), l, final_norm=(l == depth - 1))
    return h.reshape(bsz, s, D_MODEL)
```

```python
import functools

import jax
import jax.numpy as jnp
from jax import lax
from jax.experimental import pallas as pl
from jax.experimental.pallas import tpu as pltpu

D_MODEL = 1024
HEAD_DIM = 64
FOX_HEADS = 6
SB_HEADS = 4
ML_HEADS = 6
D_FOX = FOX_HEADS * HEAD_DIM
D_SB = SB_HEADS * HEAD_DIM
D_ML = ML_HEADS * HEAD_DIM
ML_CHUNK = 128
CONV_WIDTH = 4
X_HEADS = 4
X_HEAD_DIM = 128
D_X = X_HEADS * X_HEAD_DIM
D_FF = 2816
RMS_EPS = 1e-6

LANES = 128
PAIR = 2 * HEAD_DIM
D_MAIN = 3 * D_FOX + 3 * D_SB + 4 * D_ML
N_GATE = LANES
GT_ROWS = 32
NEG = -1e30
SB_SUB = 128
SB_EXIT = 110.0
LOG2E = 1.4426950408889634
VMEM_LIMIT = 56 * 1024 * 1024

BF16 = jnp.bfloat16
F32 = jnp.float32

FOX_Q, FOX_K, FOX_V = 0, 3, 6
SB_Q, SB_K, SB_V = 9, 11, 13
G_FOX, G_MLA, G_MLB, G_MLP = 0, 6, 12, 18
EXP_B = ML_HEADS * LANES
EXP_A = EXP_B + D_ML
N_EXP = EXP_A + D_ML


def _cparams(*sem):
    return pltpu.CompilerParams(dimension_semantics=sem, vmem_limit_bytes=VMEM_LIMIT)


def _rms(xf, w):
    ms = jnp.mean(xf * xf, axis=-1, keepdims=True)
    return xf * lax.rsqrt(ms + RMS_EPS) * w


def _dot(a, b):
    return jnp.dot(a, b, preferred_element_type=F32)


def _dot_nt(a, b):
    return lax.dot_general(a, b, (((1,), (1,)), ((), ())), preferred_element_type=F32)


def _dot_tn(a, b):
    return lax.dot_general(a, b, (((0,), (0,)), ((), ())), preferred_element_type=F32)


def _log_sigmoid(x):
    return jnp.minimum(x, 0.0) - jnp.log(1.0 + jnp.exp(-jnp.abs(x)))


def _split3(x):
    x1 = x.astype(BF16)
    r1 = x - x1.astype(F32)
    x2 = r1.astype(BF16)
    return x1, x2, (r1 - x2.astype(F32)).astype(BF16)


def _dot_exact(a, b, lhs_exact):
    if lhs_exact:
        x1, x2, x3 = _split3(b)
        return _dot(a, x1) + _dot(a, x2) + _dot(a, x3)
    x1, x2, x3 = _split3(a)
    return _dot(x1, b) + _dot(x2, b) + _dot(x3, b)


def _gate_chunk(pre, carry, tri):
    L = ML_CHUNK
    col = lax.broadcasted_iota(jnp.int32, (L, LANES), 1)
    row = lax.broadcasted_iota(jnp.int32, (L, LANES), 0)
    is_a = (col >= G_MLA) & (col < G_MLB)
    cum = _dot_exact(tri, jnp.where(is_a, 0.0, _log_sigmoid(pre)), lhs_exact=True)
    tot = cum + carry
    a = pre - pltpu.roll(cum, LANES - (G_MLB - G_MLA), 1)
    pm = jnp.where(is_a, a, -jnp.inf)
    k = 1
    while k < L:
        pm = jnp.maximum(pm, jnp.where(row >= k, pltpu.roll(pm, k, 0), -jnp.inf))
        k *= 2
    pm = pltpu.roll(pm, G_MLP - G_MLA, 1)
    out = jnp.where(col < G_MLA, tot,
                    jnp.where(is_a, a, jnp.where(col < G_MLP, cum, jnp.where(col < G_MLP + ML_HEADS, pm, 0.0))))
    return out, tot[L - 1:L, :]


ML_QK0 = 3 * D_FOX + 3 * D_SB
CONV_TAIL = 8


def _conv_silu(x, tail, w, scale):
    row = lax.broadcasted_iota(jnp.int32, tail.shape, 0)
    y = x * w[CONV_WIDTH - 1:CONV_WIDTH, :]
    for k in range(1, CONV_WIDTH):
        sh = pltpu.roll(x, k, 0)
        head = jnp.where(row < k, pltpu.roll(tail, k, 0), sh[:CONV_TAIL])
        sh = jnp.concatenate([head, sh[CONV_TAIL:]], axis=0)
        y = y + sh * w[CONV_WIDTH - 1 - k:CONV_WIDTH - k, :]
    return y * jax.nn.sigmoid(y) * scale


def _inproj_kernel(x_ref, nw_ref, w_ref, b_ref, cw_ref, u_ref, go_ref, gt_ref, carry_ref, tail_ref, *,
                   n_chunk, steps_per_seq):
    @pl.when(pl.program_id(0) % steps_per_seq == 0)
    def _():
        carry_ref[...] = jnp.zeros_like(carry_ref)
        tail_ref[...] = jnp.zeros_like(tail_ref)

    tm = x_ref.shape[0]
    xn = _rms(x_ref[...], nw_ref[...]).astype(BF16)
    n_all = D_MAIN + N_GATE
    qk_col = lax.broadcasted_iota(jnp.int32, (1, 2 * D_ML), 1)
    qk_scale = jnp.where(qk_col < D_ML, 1.0, HEAD_DIM ** -0.5)
    for c in reversed(range(n_all // n_chunk)):
        lo = c * n_chunk
        res = _dot(xn, w_ref[:, lo:lo + n_chunk])
        a, b = max(lo, ML_QK0), min(lo + n_chunk, ML_QK0 + 2 * D_ML)
        if a < b:
            seg = res[:, a - lo:b - lo]
            cols = slice(a - ML_QK0, b - ML_QK0)
            conv = _conv_silu(seg, tail_ref[:, cols], cw_ref[:, cols], qk_scale[:, cols])
            tail_ref[:, cols] = seg[tm - CONV_TAIL:, :]
            pieces = [res[:, :a - lo]] * (a > lo) + [conv] + [res[:, b - lo:]] * (b < lo + n_chunk)
            res = pieces[0] if len(pieces) == 1 else jnp.concatenate(pieces, axis=1)
        if lo + n_chunk <= D_MAIN:
            u_ref[:, lo:lo + n_chunk] = res.astype(BF16)
            continue
        u_ref[:, lo:D_MAIN] = res[:, :D_MAIN - lo].astype(BF16)
        pre_all = res[:, D_MAIN - lo:] + b_ref[...]
        L = ML_CHUNK
        r = lax.broadcasted_iota(jnp.int32, (L, L), 0)
        c = lax.broadcasted_iota(jnp.int32, (L, L), 1)
        tri = jnp.where(r >= c, 1.0, 0.0).astype(BF16)
        carry = carry_ref[...]
        for ci in range(tm // L):
            out, carry = _gate_chunk(pre_all[ci * L:(ci + 1) * L], carry, tri)
            go_ref[ci * L:(ci + 1) * L, :] = out
            gt_ref[:, ci * L:(ci + 1) * L] = out.T[:GT_ROWS, :]
        carry_ref[...] = carry


def _inproj(h, nw, w_all, bias, conv_w, layer, bsz, s, tm=1024):
    t = h.shape[0]
    steps = s // tm
    return pl.pallas_call(
        functools.partial(_inproj_kernel, n_chunk=512, steps_per_seq=steps),
        grid=(t // tm,),
        in_specs=[
            pl.BlockSpec((tm, D_MODEL), lambda i: (i, 0)),
            pl.BlockSpec((1, D_MODEL), lambda i: (0, 0)),
            pl.BlockSpec((None, D_MODEL, D_MAIN + N_GATE), lambda i: (layer, 0, 0)),
            pl.BlockSpec((1, N_GATE), lambda i: (0, 0)),
            pl.BlockSpec((CONV_WIDTH, 2 * D_ML), lambda i: (0, 0)),
        ],
        out_specs=[
            pl.BlockSpec((tm, D_MAIN), lambda i: (i, 0)),
            pl.BlockSpec((tm, N_GATE), lambda i: (i, 0)),
            pl.BlockSpec((None, GT_ROWS, tm), lambda i: (i // steps, 0, i % steps)),
        ],
        out_shape=[jax.ShapeDtypeStruct((t, D_MAIN), BF16), jax.ShapeDtypeStruct((t, N_GATE), F32),
                   jax.ShapeDtypeStruct((bsz, GT_ROWS, s), F32)],
        scratch_shapes=[pltpu.VMEM((1, N_GATE), F32), pltpu.VMEM((CONV_TAIL, 2 * D_ML), F32)],
        compiler_params=_cparams("arbitrary"),
        name="inproj",
    )(h, nw, w_all, bias, conv_w)


def _head_masks(q2):
    lane = lax.broadcasted_iota(jnp.int32, q2.shape, 1)
    zero = jnp.zeros_like(q2)
    return jnp.where(lane < HEAD_DIM, q2, zero), jnp.where(lane >= HEAD_DIM, q2, zero)


def _stack_heads(q2):
    return jnp.concatenate(_head_masks(q2), axis=0)


def _unstack_heads(x, rows):
    lane = lax.broadcasted_iota(jnp.int32, (rows, LANES), 1)
    return jnp.where(lane < HEAD_DIM, x[:rows], x[rows:])


def _fox_kernel(q_ref, k_ref, v_ref, g_ref, gt_ref, o_ref, ct_all, m_all, acc_all, *, tq, nq):
    p = pl.program_id(1)
    half = tq // 2

    def sweep(blk):
        qrows = slice(blk * tq, (blk + 1) * tq)
        ct_ref, m_ref, acc_ref = ct_all.at[blk], m_all.at[blk], acc_all.at[blk]
        qs = _stack_heads(q_ref[qrows, :] * jnp.asarray(HEAD_DIM ** -0.5, BF16))
        gq = g_ref[qrows, :]
        lane = lax.broadcasted_iota(jnp.int32, (tq, LANES), 1)
        for hh in range(2):
            ct = jnp.sum(jnp.where(lane == G_FOX + 2 * p + hh, gq, 0.0), axis=1, keepdims=True)
            ct_ref[hh * tq:(hh + 1) * tq, :] = jnp.broadcast_to(ct * LOG2E, (tq, LANES))
        m_ref[...] = jnp.full_like(m_ref, NEG)
        acc_ref[...] = jnp.zeros_like(acc_ref)

        def group(start, nk, row0, masked):
            nr = tq - row0
            qsub = qs if row0 == 0 else jnp.concatenate([qs[row0:tq], qs[tq + row0:]], axis=0)
            s = _dot_nt(qsub, k_ref[start:start + nk, :])
            v2 = v_ref[start:start + nk, :]
            one = jnp.ones_like(v2)
            lo_lanes = lax.broadcasted_iota(jnp.int32, (nk, LANES), 1) < HEAD_DIM
            v_ones = (jnp.where(lo_lanes, v2, one), jnp.where(lo_lanes, one, v2))
            if masked:
                rq = lax.broadcasted_iota(jnp.int32, (nr, nk), 0)
                rk = lax.broadcasted_iota(jnp.int32, (nr, nk), 1)
                causal = jnp.where(rk <= rq, 0.0, NEG)
            for hh in range(2):
                rows = slice(hh * tq + row0, (hh + 1) * tq)
                cs = gt_ref[pl.ds(G_FOX + 2 * p + hh, 1), start:start + nk] * LOG2E
                r = s[hh * nr:(hh + 1) * nr] * LOG2E - cs
                if masked:
                    r = r + causal
                ct = ct_ref[rows, :]
                m_prev = m_ref[rows, :]
                m_next = jnp.maximum(m_prev, jnp.max(r, axis=1, keepdims=True) + ct)
                pr = jnp.exp2(r - jnp.tile(m_next - ct, (1, nk // LANES)))
                alpha = jnp.exp2(m_prev - m_next)
                acc_ref[rows, :] = alpha * acc_ref[rows, :] + _dot(pr.astype(BF16), v_ones[hh])
                m_ref[rows, :] = m_next

        for j in range(blk):
            group(j * tq, tq, 0, False)
        group(blk * tq, half, 0, True)
        group(blk * tq + half, half, half, True)
        acc = acc_ref[...]
        o_ref[qrows, :] = _unstack_heads(acc / pltpu.roll(acc, HEAD_DIM, 1), tq).astype(BF16)

    for blk in range(nq):
        sweep(blk)


def _fox(u, g, gt, bsz, s, tq=512):
    nq = s // tq
    stat = pltpu.VMEM((nq, 2 * tq, LANES), F32)
    return pl.pallas_call(
        functools.partial(_fox_kernel, tq=tq, nq=nq),
        grid=(bsz, FOX_HEADS // 2),
        in_specs=[
            pl.BlockSpec((s, PAIR), lambda b, p: (b, FOX_Q + p)),
            pl.BlockSpec((s, PAIR), lambda b, p: (b, FOX_K + p)),
            pl.BlockSpec((s, PAIR), lambda b, p: (b, FOX_V + p)),
            pl.BlockSpec((s, N_GATE), lambda b, p: (b, 0)),
            pl.BlockSpec((None, 8, s), lambda b, p: (b, 0, 0)),
        ],
        out_specs=pl.BlockSpec((s, PAIR), lambda b, p: (b, p)),
        out_shape=jax.ShapeDtypeStruct((bsz * s, D_FOX), BF16),
        scratch_shapes=[stat, stat, stat],
        compiler_params=_cparams("parallel", "parallel"),
        name="fox",
    )(u, u, u, g, gt)


def _sb_kernel(q_ref, k_ref, v_ref, o_ref, qs_all, run_all, acc_all, *, tq, tg, nq):
    n_diag = tq // tg

    def sweep(blk):
        qrows = slice(blk * tq, (blk + 1) * tq)
        qs_ref, run_ref, acc_ref = qs_all.at[blk], run_all.at[blk], acc_all.at[blk]
        qs_ref[...] = _stack_heads(q_ref[qrows, :] * jnp.asarray(HEAD_DIM ** -0.5, BF16))
        r = lax.broadcasted_iota(jnp.int32, (2 * SB_SUB, 2 * SB_SUB), 0)
        c = lax.broadcasted_iota(jnp.int32, (2 * SB_SUB, 2 * SB_SUB), 1)
        suffix = jnp.where((c >= SB_SUB) | (jnp.where(r >= SB_SUB, r - SB_SUB, r) >= c), 1.0, 0.0).astype(BF16)
        run_ref[...] = jnp.zeros_like(run_ref)
        acc_ref[...] = jnp.zeros_like(acc_ref)

        def group(g, diag):
            row0 = tq - (g + 1) * tg if diag else 0
            nr = tq - row0
            heads = lambda ref: ref[...] if row0 == 0 else jnp.concatenate([ref[row0:tq, :], ref[tq + row0:, :]], axis=0)
            start = (blk + 1) * tq - (g + 1) * tg
            if not isinstance(start, int):
                start = pl.multiple_of(start, tg)
            z = _dot_nt(heads(qs_ref), k_ref[pl.ds(start, tg), :])
            sp = jnp.maximum(z, 0.0) + jnp.log(1.0 + jnp.exp(-jnp.abs(z)))
            if diag:
                rq = lax.broadcasted_iota(jnp.int32, (2 * nr, tg), 0)
                rk = lax.broadcasted_iota(jnp.int32, (2 * nr, tg), 1)
                valid = rk < jnp.where(rq >= nr, rq - nr, rq)
                sp = jnp.where(valid, sp, 0.0)
            hi = sp.astype(BF16)
            lo = (sp - hi.astype(F32)).astype(BF16)
            offs = heads(run_ref)
            parts = [None] * (tg // SB_SUB)
            for sb in reversed(range(tg // SB_SUB)):
                sl = slice(sb * SB_SUB, (sb + 1) * SB_SUB)
                rs = _dot(jnp.concatenate([hi[:, sl], lo[:, sl]], axis=1), suffix)
                a = jnp.exp(z[:, sl] - rs[:, :SB_SUB] - offs)
                if diag:
                    a = jnp.where(valid[:, sl], a, 0.0)
                parts[sb] = a.astype(BF16)
                offs = offs + rs[:, SB_SUB:]
            pv = _dot(jnp.concatenate(parts, axis=1), v_ref[pl.ds(start, tg), :])
            for hh in range(2):
                rows = slice(hh * tq + row0, (hh + 1) * tq)
                run_ref[rows, :] = offs[hh * nr:(hh + 1) * nr]
                acc_ref[rows, :] += pv[hh * nr:(hh + 1) * nr]
            return jnp.min(offs)

        for g in range(n_diag):
            low = group(g, True)
        n_groups = (blk + 1) * n_diag
        first = n_diag
        if blk > 0:
            low = group(n_diag, False)
            first += 1
        if first < n_groups:
            def cond(carry):
                g, low = carry
                return (g < n_groups) & (low < SB_EXIT)

            def body(carry):
                g, _ = carry
                return g + 1, group(g, False)

            lax.while_loop(cond, body, (jnp.int32(first), low))
        o_ref[qrows, :] = _unstack_heads(acc_ref[...], tq).astype(BF16)

    for blk in range(nq):
        sweep(blk)


def _sb(u, bsz, s, tq=512, tg=256):
    nq = s // tq
    stat = pltpu.VMEM((nq, 2 * tq, LANES), F32)
    return pl.pallas_call(
        functools.partial(_sb_kernel, tq=tq, tg=tg, nq=nq),
        grid=(bsz, SB_HEADS // 2),
        in_specs=[
            pl.BlockSpec((s, PAIR), lambda b, p: (b, SB_Q + p)),
            pl.BlockSpec((s, PAIR), lambda b, p: (b, SB_K + p)),
            pl.BlockSpec((s, PAIR), lambda b, p: (b, SB_V + p)),
        ],
        out_specs=pl.BlockSpec((s, PAIR), lambda b, p: (b, p)),
        out_shape=jax.ShapeDtypeStruct((bsz * s, D_SB), BF16),
        scratch_shapes=[pltpu.VMEM((nq, 2 * tq, PAIR), BF16), stat, stat],
        compiler_params=_cparams("parallel", "parallel"),
        name="sb",
    )(u, u, u)


def _mlstm_kernel(q_ref, k_ref, v_ref, og_ref, g_ref, gt_ref, nw_ref, o_ref, e_ref, *, s):
    L = ML_CHUNK
    nc = s // L

    ci = lax.broadcasted_iota(jnp.int32, (LANES, N_EXP), 0)
    cj = lax.broadcasted_iota(jnp.int32, (LANES, N_EXP), 1)
    src_col = jnp.where(cj < EXP_B, G_MLP + lax.shift_right_logical(cj, 7),
                        jnp.where(cj < EXP_A, G_MLB + lax.shift_right_logical(cj - EXP_B, 6),
                                  G_MLA + lax.shift_right_logical(cj - EXP_A, 6)))
    e_ref[...] = jnp.where(ci == src_col, 1.0, 0.0).astype(BF16)

    lane = lax.broadcasted_iota(jnp.int32, (L, LANES), 1)
    lo_lanes = lane < HEAD_DIM
    r_ll = lax.broadcasted_iota(jnp.int32, (L, L), 0)
    c_ll = lax.broadcasted_iota(jnp.int32, (L, L), 1)
    tri = r_ll >= c_ll
    blockdiag = (r_ll < HEAD_DIM) == (c_ll < HEAD_DIM)
    ones = jnp.ones((L, LANES), BF16)

    def chunk(c, carry):
        start = pl.multiple_of(c * L, L)
        rep = _dot_exact(g_ref[pl.ds(start, L), :], e_ref[...], lhs_exact=False)
        gtc = gt_ref[:, pl.ds(start, L)]
        new_carry = []
        for p in range(ML_HEADS // 2):
            c2, nm, m2 = carry[p]
            sl = slice(p * PAIR, (p + 1) * PAIR)
            q2 = q_ref[pl.ds(start, L), sl]
            k2 = k_ref[pl.ds(start, L), sl]
            v_one = jnp.concatenate([v_ref[pl.ds(start, L), sl], ones], axis=1)
            b2 = rep[:, EXP_B + p * PAIR:EXP_B + (p + 1) * PAIR]
            a2 = rep[:, EXP_A + p * PAIR:EXP_A + (p + 1) * PAIR]
            pm = [rep[:, (2 * p + hh) * LANES:(2 * p + hh + 1) * LANES] for hh in range(2)]
            pm2 = jnp.where(lo_lanes, pm[0], pm[1])
            qk = _dot_nt(_stack_heads(q2), k2)
            sc = []
            for hh in range(2):
                a_row = gtc[G_MLA + 2 * p + hh:G_MLA + 2 * p + hh + 1, :]
                sc.append(qk[hh * L:(hh + 1) * L] * jnp.exp(jnp.where(tri, a_row - pm[hh], -jnp.inf)))
            r = _dot(jnp.concatenate(sc, axis=0).astype(BF16), v_one)
            num_loc = jnp.where(lo_lanes, r[:L, :LANES], r[L:, :LANES])
            den_loc = jnp.where(lo_lanes, r[:L, LANES:], r[L:, LANES:])
            m_loc = b2 + pm2
            inter = _dot(q2, jnp.concatenate([c2, nm], axis=1).astype(BF16))
            it2 = b2 + m2
            m_t = jnp.maximum(it2, m_loc)
            f_loc = jnp.exp(m_loc - m_t)
            w_int = jnp.exp(it2 - m_t)
            num = num_loc * f_loc + w_int * inter[:, :LANES]
            den = den_loc * f_loc + w_int * inter[:, LANES:]
            hp = num / jnp.maximum(jnp.abs(den), jnp.exp(-m_t))
            sq = hp * hp
            ms0 = jnp.sum(jnp.where(lo_lanes, sq, 0.0), axis=1, keepdims=True)
            ms1 = jnp.sum(jnp.where(lo_lanes, 0.0, sq), axis=1, keepdims=True)
            msq = jnp.where(lo_lanes, ms0, ms1) * (1.0 / HEAD_DIM)
            hp = hp * lax.rsqrt(msq + RMS_EPS) * nw_ref[:, sl]
            gate = jax.nn.sigmoid(og_ref[pl.ds(start, L), sl].astype(F32))
            o_ref[pl.ds(start, L), sl] = (gate * hp).astype(BF16)
            g2 = b2[L - 1:L, :]
            pl2 = pm2[L - 1:L, :]
            kw = (k2.astype(F32) * jnp.exp(a2 - pl2)).astype(BF16)
            upd = _dot_tn(kw, v_one)
            m_new = jnp.maximum(g2 + m2, g2 + pl2)
            decay = jnp.exp(g2 + m2 - m_new)
            scale = jnp.exp(g2 + pl2 - m_new)
            c2 = decay * c2 + scale * jnp.where(blockdiag, upd[:, :LANES], 0.0)
            nm = decay * nm + scale * jnp.where(blockdiag, upd[:, LANES:], 0.0)
            new_carry.append((c2, nm, m_new))
        return tuple(new_carry)

    init = tuple((jnp.zeros((LANES, LANES), F32), jnp.zeros((LANES, LANES), F32), jnp.zeros((1, LANES), F32))
                 for _ in range(ML_HEADS // 2))
    lax.fori_loop(0, nc, chunk, init, unroll=True)


def _mlstm(u, g, gt, norm_w, bsz, s):
    blk = lambda j: pl.BlockSpec((s, D_ML), lambda b: (b, j))
    first = (3 * D_FOX + 3 * D_SB) // D_ML
    return pl.pallas_call(
        functools.partial(_mlstm_kernel, s=s),
        grid=(bsz,),
        in_specs=[
            blk(first), blk(first + 1), blk(first + 2), blk(first + 3),
            pl.BlockSpec((s, N_GATE), lambda b: (b, 0)),
            pl.BlockSpec((None, GT_ROWS, s), lambda b: (b, 0, 0)),
            pl.BlockSpec((1, D_ML), lambda b: (0, 0)),
        ],
        out_specs=pl.BlockSpec((s, D_ML), lambda b: (b, 0)),
        out_shape=jax.ShapeDtypeStruct((bsz * s, D_ML), BF16),
        scratch_shapes=[pltpu.VMEM((LANES, N_EXP), BF16)],
        compiler_params=_cparams("parallel"),
        name="mlstm",
    )(u, u, u, u, g, gt, norm_w)


def _memkv_kernel(x_ref, nw_ref, w_ref, o_ref):
    xn = _rms(x_ref[...], nw_ref[...]).astype(BF16)
    o_ref[...] = _dot(xn, w_ref[...]).astype(BF16)


def _memkv(mem2d, nw, w, tm=512):
    t = mem2d.shape[0]
    depth = w.shape[0]
    return pl.pallas_call(
        _memkv_kernel,
        grid=(depth, t // tm),
        in_specs=[
            pl.BlockSpec((tm, D_MODEL), lambda l, i: (i, 0)),
            pl.BlockSpec((None, 1, D_MODEL), lambda l, i: (l, 0, 0)),
            pl.BlockSpec((None, D_MODEL, 2 * D_X), lambda l, i: (l, 0, 0)),
        ],
        out_specs=pl.BlockSpec((None, tm, 2 * D_X), lambda l, i: (l, i, 0)),
        out_shape=jax.ShapeDtypeStruct((depth, t, 2 * D_X), BF16),
        compiler_params=_cparams("parallel", "parallel"),
        name="memkv",
    )(mem2d, nw, w)


def _cross_kernel(yf_ref, ys_ref, ym_ref, wo_ref, h_ref, nw_ref, wq_ref, kv_ref, wxo_ref, o_ref):
    hf = (h_ref[...] + _dot(yf_ref[...], wo_ref[:D_FOX, :]) + _dot(ys_ref[...], wo_ref[D_FOX:D_FOX + D_SB, :])
          + _dot(ym_ref[...], wo_ref[D_FOX + D_SB:, :]))
    hn = _rms(hf, nw_ref[...]).astype(BF16)
    q = _dot(hn, wq_ref[...]).astype(BF16)
    outs = []
    for hd in range(X_HEADS):
        sl = slice(hd * X_HEAD_DIM, (hd + 1) * X_HEAD_DIM)
        k = kv_ref[:, sl]
        v = kv_ref[:, D_X + hd * X_HEAD_DIM:D_X + (hd + 1) * X_HEAD_DIM]
        s = _dot_nt(q[:, sl], k) * (X_HEAD_DIM ** -0.5)
        m = jnp.max(s, axis=1, keepdims=True)
        e = jnp.exp(s - m)
        pr = e / jnp.sum(e, axis=1, keepdims=True)
        outs.append(_dot(pr.astype(BF16), v).astype(BF16))
    o = jnp.concatenate(outs, axis=1)
    o_ref[...] = hf + _dot(o, wxo_ref[...])


def _cross(yf, ys, ym, wo, h, nw, wq, kv, wxo, layer, bsz, s, mlen, tm=1024):
    nt = s // tm
    row = lambda n: pl.BlockSpec((tm, n), lambda b, i: (b * nt + i, 0))
    wgt = lambda r, c: pl.BlockSpec((None, r, c), lambda b, i: (layer, 0, 0))
    return pl.pallas_call(
        _cross_kernel,
        grid=(bsz, nt),
        in_specs=[
            row(D_FOX), row(D_SB), row(D_ML), wgt(D_MODEL, D_MODEL), row(D_MODEL),
            pl.BlockSpec((1, D_MODEL), lambda b, i: (0, 0)),
            wgt(D_MODEL, D_X),
            pl.BlockSpec((None, mlen, 2 * D_X), lambda b, i: (layer, b, 0)),
            wgt(D_X, D_MODEL),
        ],
        out_specs=row(D_MODEL),
        out_shape=jax.ShapeDtypeStruct((bsz * s, D_MODEL), F32),
        compiler_params=_cparams("parallel", "parallel"),
        name="cross",
    )(yf, ys, ym, wo, h, nw, wq, kv, wxo)


def _ffn_kernel(h_ref, nw_ref, wg_ref, wu_ref, wd_ref, fw_ref, o_ref, acc_ref, *, f_chunk, final_norm):
    hf = h_ref[...]
    hn = _rms(hf, nw_ref[...]).astype(BF16)
    for c in range(D_FF // f_chunk):
        sl = slice(c * f_chunk, (c + 1) * f_chunk)
        g = _dot(hn, wg_ref[:, sl].astype(BF16))
        u = _dot(hn, wu_ref[:, sl].astype(BF16))
        a = (g * jax.nn.sigmoid(g) * u).astype(BF16)
        part = _dot(a, wd_ref[sl, :].astype(BF16))
        if c == 0:
            acc_ref[...] = part
        else:
            acc_ref[...] += part
    out = hf + acc_ref[...]
    if final_norm:
        out = _rms(out, fw_ref[...])
    o_ref[...] = out


def _ffn(h, nw, wg, wu, wd, fw, layer, final_norm, tm=512):
    t = h.shape[0]
    const = lambda shape: pl.BlockSpec(shape, lambda i: (0, 0))
    wgt = lambda r, c: pl.BlockSpec((None, r, c), lambda i: (layer, 0, 0), pipeline_mode=pl.Buffered(1))
    return pl.pallas_call(
        functools.partial(_ffn_kernel, f_chunk=256, final_norm=final_norm),
        grid=(t // tm,),
        in_specs=[
            pl.BlockSpec((tm, D_MODEL), lambda i: (i, 0)),
            const((1, D_MODEL)),
            wgt(D_MODEL, D_FF), wgt(D_MODEL, D_FF), wgt(D_FF, D_MODEL),
            const((1, D_MODEL)),
        ],
        out_specs=pl.BlockSpec((tm, D_MODEL), lambda i: (i, 0)),
        out_shape=jax.ShapeDtypeStruct((t, D_MODEL), F32),
        scratch_shapes=[pltpu.VMEM((tm, D_MODEL), F32)],
        compiler_params=_cparams("parallel"),
        name="ffn",
    )(h, nw, wg, wu, wd, fw)


N_FOX_QKV = 3 * D_FOX
N_REST = 3 * D_SB + 4 * D_ML


N_USED_GATES = FOX_HEADS + 2 * ML_HEADS


def _regroup_kernel(w_ref, o_ref):
    o_ref[:, :N_FOX_QKV] = w_ref[:, :N_FOX_QKV].astype(BF16)
    rest = w_ref[:, N_FOX_QKV + FOX_HEADS:N_FOX_QKV + FOX_HEADS + N_REST]
    o_ref[:, N_FOX_QKV:D_MAIN] = rest.astype(BF16)
    lane = lax.broadcasted_iota(jnp.int32, (w_ref.shape[0], N_USED_GATES), 1)
    gates = jnp.where(lane < FOX_HEADS, w_ref[:, N_FOX_QKV:N_FOX_QKV + N_USED_GATES], w_ref[:, D_MAIN:D_MAIN + N_USED_GATES])
    o_ref[:, D_MAIN:] = jnp.zeros((w_ref.shape[0], N_GATE), BF16)
    o_ref[:, D_MAIN:D_MAIN + N_USED_GATES] = gates.astype(BF16)


def _regroup_w_in(w, tr=256):
    depth, rows, n_in = w.shape
    assert N_FOX_QKV + N_REST == D_MAIN and D_MAIN % LANES == 0 and n_in == D_MAIN + N_USED_GATES
    return pl.pallas_call(
        _regroup_kernel,
        grid=(depth, rows // tr),
        in_specs=[pl.BlockSpec((None, tr, n_in), lambda l, i: (l, i, 0))],
        out_specs=pl.BlockSpec((None, tr, D_MAIN + N_GATE), lambda l, i: (l, i, 0)),
        out_shape=jax.ShapeDtypeStruct((depth, rows, D_MAIN + N_GATE), BF16),
        compiler_params=_cparams("parallel", "parallel"),
        name="regroup",
    )(w)


def kernel(x, mem, norm_mix_w, w_in, fox_f_b, ml_conv_w, ml_i_b, ml_f_b, ml_norm_w, w_out, norm_x_w, mem_norm_w,
           wx_q, wx_kv, wx_o, norm_ffn_w, w_gate, w_up, w_down, final_norm_w):
    bsz, s, _ = x.shape
    mlen = mem.shape[1]
    depth = w_in.shape[0]
    h = x.reshape(bsz * s, D_MODEL)
    mem2d = mem.reshape(bsz * mlen, D_MODEL)
    row = lambda v: v.reshape(1, -1)
    w_all = _regroup_w_in(w_in)
    bias = jnp.concatenate([fox_f_b, ml_i_b, ml_f_b, jnp.zeros((depth, N_GATE - 3 * FOX_HEADS), F32)], axis=1)
    wo, wq, wkv, wxo = (w.astype(BF16) for w in (w_out, wx_q, wx_kv, wx_o))
    kv = _memkv(mem2d, mem_norm_w.reshape(depth, 1, D_MODEL), wkv, tm=min(512, bsz * mlen))
    for l in range(depth):
        u, g, gt = _inproj(h, row(norm_mix_w[l]), w_all, row(bias[l]), ml_conv_w[l], l, bsz, s)
        y_fox = _fox(u, g, gt, bsz, s)
        y_sb = _sb(u, bsz, s)
        y_ml = _mlstm(u, g, gt, row(ml_norm_w[l]), bsz, s)
        h = _cross(y_fox, y_sb, y_ml, wo, h, row(norm_x_w[l]), wq, kv, wxo, l, bsz, s, mlen)
        h = _ffn(h, row(norm_ffn_w[l]), w_gate, w_up, w_down, row(final_norm_w), l, final_norm=(l == depth - 1))
    return h.reshape(bsz, s, D_MODEL)
```

```python
import functools

import jax
import jax.numpy as jnp
from jax import lax
from jax.experimental import pallas as pl
from jax.experimental.pallas import tpu as pltpu

D_MODEL = 1024
HEAD_DIM = 64
FOX_HEADS = 6
SB_HEADS = 4
ML_HEADS = 6
D_FOX = FOX_HEADS * HEAD_DIM
D_SB = SB_HEADS * HEAD_DIM
D_ML = ML_HEADS * HEAD_DIM
ML_CHUNK = 128
CONV_WIDTH = 4
X_HEADS = 4
X_HEAD_DIM = 128
D_X = X_HEADS * X_HEAD_DIM
D_FF = 2816
RMS_EPS = 1e-6

LANES = 128
PAIR = 2 * HEAD_DIM
D_MAIN = 3 * D_FOX + 3 * D_SB + 4 * D_ML
N_GATE = LANES
GT_ROWS = 32
NEG = -1e30
SB_SUB = 128
SB_EXIT = 110.0
LOG2E = 1.4426950408889634
VMEM_LIMIT = 56 * 1024 * 1024

BF16 = jnp.bfloat16
F32 = jnp.float32

FOX_Q, FOX_K, FOX_V = 0, 3, 6
SB_Q, SB_K, SB_V = 9, 11, 13
G_FOX, G_MLA, G_MLB, G_MLP = 0, 6, 12, 18
EXP_B = ML_HEADS * LANES
EXP_A = EXP_B + D_ML
N_EXP = EXP_A + D_ML


def _cparams(*sem):
    return pltpu.CompilerParams(dimension_semantics=sem, vmem_limit_bytes=VMEM_LIMIT)


def _rms(xf, w):
    ms = jnp.mean(xf * xf, axis=-1, keepdims=True)
    return xf * lax.rsqrt(ms + RMS_EPS) * w


def _dot(a, b):
    return jnp.dot(a, b, preferred_element_type=F32)


def _dot_nt(a, b):
    return lax.dot_general(a, b, (((1,), (1,)), ((), ())), preferred_element_type=F32)


def _dot_tn(a, b):
    return lax.dot_general(a, b, (((0,), (0,)), ((), ())), preferred_element_type=F32)


def _log_sigmoid(x):
    return jnp.minimum(x, 0.0) - jnp.log(1.0 + jnp.exp(-jnp.abs(x)))


def _split3(x):
    x1 = x.astype(BF16)
    r1 = x - x1.astype(F32)
    x2 = r1.astype(BF16)
    return x1, x2, (r1 - x2.astype(F32)).astype(BF16)


def _dot_exact(a, b, lhs_exact):
    if lhs_exact:
        x1, x2, x3 = _split3(b)
        return _dot(a, x1) + _dot(a, x2) + _dot(a, x3)
    x1, x2, x3 = _split3(a)
    return _dot(x1, b) + _dot(x2, b) + _dot(x3, b)


def _gate_chunk(pre, carry, tri):
    L = ML_CHUNK
    col = lax.broadcasted_iota(jnp.int32, (L, LANES), 1)
    row = lax.broadcasted_iota(jnp.int32, (L, LANES), 0)
    is_a = (col >= G_MLA) & (col < G_MLB)
    cum = _dot_exact(tri, jnp.where(is_a, 0.0, _log_sigmoid(pre)), lhs_exact=True)
    tot = cum + carry
    a = pre - pltpu.roll(cum, LANES - (G_MLB - G_MLA), 1)
    pm = jnp.where(is_a, a, -jnp.inf)
    k = 1
    while k < L:
        pm = jnp.maximum(pm, jnp.where(row >= k, pltpu.roll(pm, k, 0), -jnp.inf))
        k *= 2
    pm = pltpu.roll(pm, G_MLP - G_MLA, 1)
    out = jnp.where(col < G_MLA, tot,
                    jnp.where(is_a, a, jnp.where(col < G_MLP, cum, jnp.where(col < G_MLP + ML_HEADS, pm, 0.0))))
    return out, tot[L - 1:L, :]


ML_QK0 = 3 * D_FOX + 3 * D_SB
CONV_TAIL = 8


def _conv_silu(x, tail, w, scale):
    row = lax.broadcasted_iota(jnp.int32, tail.shape, 0)
    y = x * w[CONV_WIDTH - 1:CONV_WIDTH, :]
    for k in range(1, CONV_WIDTH):
        sh = pltpu.roll(x, k, 0)
        head = jnp.where(row < k, pltpu.roll(tail, k, 0), sh[:CONV_TAIL])
        sh = jnp.concatenate([head, sh[CONV_TAIL:]], axis=0)
        y = y + sh * w[CONV_WIDTH - 1 - k:CONV_WIDTH - k, :]
    return y * jax.nn.sigmoid(y) * scale


def _inproj_kernel(x_ref, nw_ref, w_ref, b_ref, cw_ref, u_ref, go_ref, gt_ref, carry_ref, tail_ref, *,
                   n_chunk, steps_per_seq):
    @pl.when(pl.program_id(0) % steps_per_seq == 0)
    def _():
        carry_ref[...] = jnp.zeros_like(carry_ref)
        tail_ref[...] = jnp.zeros_like(tail_ref)

    tm = x_ref.shape[0]
    xn = _rms(x_ref[...], nw_ref[...]).astype(BF16)
    n_all = D_MAIN + N_GATE
    qk_col = lax.broadcasted_iota(jnp.int32, (1, 2 * D_ML), 1)
    qk_scale = jnp.where(qk_col < D_ML, 1.0, HEAD_DIM ** -0.5)
    for c in reversed(range(n_all // n_chunk)):
        lo = c * n_chunk
        res = _dot(xn, w_ref[:, lo:lo + n_chunk])
        a, b = max(lo, ML_QK0), min(lo + n_chunk, ML_QK0 + 2 * D_ML)
        if a < b:
            seg = res[:, a - lo:b - lo]
            cols = slice(a - ML_QK0, b - ML_QK0)
            conv = _conv_silu(seg, tail_ref[:, cols], cw_ref[:, cols], qk_scale[:, cols])
            tail_ref[:, cols] = seg[tm - CONV_TAIL:, :]
            pieces = [res[:, :a - lo]] * (a > lo) + [conv] + [res[:, b - lo:]] * (b < lo + n_chunk)
            res = pieces[0] if len(pieces) == 1 else jnp.concatenate(pieces, axis=1)
        if lo + n_chunk <= D_MAIN:
            u_ref[:, lo:lo + n_chunk] = res.astype(BF16)
            continue
        u_ref[:, lo:D_MAIN] = res[:, :D_MAIN - lo].astype(BF16)
        pre_all = res[:, D_MAIN - lo:] + b_ref[...]
        L = ML_CHUNK
        r = lax.broadcasted_iota(jnp.int32, (L, L), 0)
        c = lax.broadcasted_iota(jnp.int32, (L, L), 1)
        tri = jnp.where(r >= c, 1.0, 0.0).astype(BF16)
        carry = carry_ref[...]
        for ci in range(tm // L):
            out, carry = _gate_chunk(pre_all[ci * L:(ci + 1) * L], carry, tri)
            go_ref[ci * L:(ci + 1) * L, :] = out
            gt_ref[:, ci * L:(ci + 1) * L] = out.T[:GT_ROWS, :]
        carry_ref[...] = carry


def _inproj(h, nw, w_all, bias, conv_w, layer, bsz, s, tm=1024):
    t = h.shape[0]
    steps = s // tm
    return pl.pallas_call(
        functools.partial(_inproj_kernel, n_chunk=512, steps_per_seq=steps),
        grid=(t // tm,),
        in_specs=[
            pl.BlockSpec((tm, D_MODEL), lambda i: (i, 0)),
            pl.BlockSpec((1, D_MODEL), lambda i: (0, 0)),
            pl.BlockSpec((None, D_MODEL, D_MAIN + N_GATE), lambda i: (layer, 0, 0)),
            pl.BlockSpec((1, N_GATE), lambda i: (0, 0)),
            pl.BlockSpec((CONV_WIDTH, 2 * D_ML), lambda i: (0, 0)),
        ],
        out_specs=[
            pl.BlockSpec((tm, D_MAIN), lambda i: (i, 0)),
            pl.BlockSpec((tm, N_GATE), lambda i: (i, 0)),
            pl.BlockSpec((None, GT_ROWS, tm), lambda i: (i // steps, 0, i % steps)),
        ],
        out_shape=[jax.ShapeDtypeStruct((t, D_MAIN), BF16), jax.ShapeDtypeStruct((t, N_GATE), F32),
                   jax.ShapeDtypeStruct((bsz, GT_ROWS, s), F32)],
        scratch_shapes=[pltpu.VMEM((1, N_GATE), F32), pltpu.VMEM((CONV_TAIL, 2 * D_ML), F32)],
        compiler_params=_cparams("arbitrary"),
        name="inproj",
    )(h, nw, w_all, bias, conv_w)


def _head_masks(q2):
    lane = lax.broadcasted_iota(jnp.int32, q2.shape, 1)
    zero = jnp.zeros_like(q2)
    return jnp.where(lane < HEAD_DIM, q2, zero), jnp.where(lane >= HEAD_DIM, q2, zero)


def _stack_heads(q2):
    return jnp.concatenate(_head_masks(q2), axis=0)


def _unstack_heads(x, rows):
    lane = lax.broadcasted_iota(jnp.int32, (rows, LANES), 1)
    return jnp.where(lane < HEAD_DIM, x[:rows], x[rows:])


def _fox_kernel(q_ref, k_ref, v_ref, g_ref, gt_ref, o_ref, ct_all, m_all, acc_all, *, tq, nq):
    p = pl.program_id(1)
    half = tq // 2

    def sweep(blk):
        qrows = slice(blk * tq, (blk + 1) * tq)
        ct_ref, m_ref, acc_ref = ct_all.at[blk], m_all.at[blk], acc_all.at[blk]
        qs = _stack_heads(q_ref[qrows, :] * jnp.asarray(HEAD_DIM ** -0.5, BF16))
        gq = g_ref[qrows, :]
        lane = lax.broadcasted_iota(jnp.int32, (tq, LANES), 1)
        for hh in range(2):
            ct = jnp.sum(jnp.where(lane == G_FOX + 2 * p + hh, gq, 0.0), axis=1, keepdims=True)
            ct_ref[hh * tq:(hh + 1) * tq, :] = jnp.broadcast_to(ct * LOG2E, (tq, LANES))
        m_ref[...] = jnp.full_like(m_ref, NEG)
        acc_ref[...] = jnp.zeros_like(acc_ref)

        def group(start, nk, row0, masked):
            nr = tq - row0
            qsub = qs if row0 == 0 else jnp.concatenate([qs[row0:tq], qs[tq + row0:]], axis=0)
            s = _dot_nt(qsub, k_ref[start:start + nk, :])
            v2 = v_ref[start:start + nk, :]
            one = jnp.ones_like(v2)
            lo_lanes = lax.broadcasted_iota(jnp.int32, (nk, LANES), 1) < HEAD_DIM
            v_ones = (jnp.where(lo_lanes, v2, one), jnp.where(lo_lanes, one, v2))
            if masked:
                rq = lax.broadcasted_iota(jnp.int32, (nr, nk), 0)
                rk = lax.broadcasted_iota(jnp.int32, (nr, nk), 1)
                causal = jnp.where(rk <= rq, 0.0, NEG)
            for hh in range(2):
                rows = slice(hh * tq + row0, (hh + 1) * tq)
                cs = gt_ref[pl.ds(G_FOX + 2 * p + hh, 1), start:start + nk] * LOG2E
                r = s[hh * nr:(hh + 1) * nr] * LOG2E - cs
                if masked:
                    r = r + causal
                ct = ct_ref[rows, :]
                m_prev = m_ref[rows, :]
                m_next = jnp.maximum(m_prev, jnp.max(r, axis=1, keepdims=True) + ct)
                pr = jnp.exp2(r - jnp.tile(m_next - ct, (1, nk // LANES)))
                alpha = jnp.exp2(m_prev - m_next)
                acc_ref[rows, :] = alpha * acc_ref[rows, :] + _dot(pr.astype(BF16), v_ones[hh])
                m_ref[rows, :] = m_next

        for j in range(blk):
            group(j * tq, tq, 0, False)
        group(blk * tq, half, 0, True)
        group(blk * tq + half, half, half, True)
        acc = acc_ref[...]
        o_ref[qrows, :] = _unstack_heads(acc / pltpu.roll(acc, HEAD_DIM, 1), tq).astype(BF16)

    for blk in range(nq):
        sweep(blk)


def _fox(u, g, gt, bsz, s, tq=512):
    nq = s // tq
    stat = pltpu.VMEM((nq, 2 * tq, LANES), F32)
    return pl.pallas_call(
        functools.partial(_fox_kernel, tq=tq, nq=nq),
        grid=(bsz, FOX_HEADS // 2),
        in_specs=[
            pl.BlockSpec((s, PAIR), lambda b, p: (b, FOX_Q + p)),
            pl.BlockSpec((s, PAIR), lambda b, p: (b, FOX_K + p)),
            pl.BlockSpec((s, PAIR), lambda b, p: (b, FOX_V + p)),
            pl.BlockSpec((s, N_GATE), lambda b, p: (b, 0)),
            pl.BlockSpec((None, 8, s), lambda b, p: (b, 0, 0)),
        ],
        out_specs=pl.BlockSpec((s, PAIR), lambda b, p: (b, p)),
        out_shape=jax.ShapeDtypeStruct((bsz * s, D_FOX), BF16),
        scratch_shapes=[stat, stat, stat],
        compiler_params=_cparams("parallel", "parallel"),
        name="fox",
    )(u, u, u, g, gt)


def _sb_kernel(q_ref, k_ref, v_ref, o_ref, qs_all, run_all, acc_all, *, tq, tg, nq):
    n_diag = tq // tg

    def sweep(blk):
        qrows = slice(blk * tq, (blk + 1) * tq)
        qs_ref, run_ref, acc_ref = qs_all.at[blk], run_all.at[blk], acc_all.at[blk]
        qs_ref[...] = _stack_heads(q_ref[qrows, :] * jnp.asarray(HEAD_DIM ** -0.5, BF16))
        r = lax.broadcasted_iota(jnp.int32, (2 * SB_SUB, 2 * SB_SUB), 0)
        c = lax.broadcasted_iota(jnp.int32, (2 * SB_SUB, 2 * SB_SUB), 1)
        suffix = jnp.where((c >= SB_SUB) | (jnp.where(r >= SB_SUB, r - SB_SUB, r) >= c), 1.0, 0.0).astype(BF16)
        run_ref[...] = jnp.zeros_like(run_ref)
        acc_ref[...] = jnp.zeros_like(acc_ref)

        def group(g, diag):
            row0 = tq - (g + 1) * tg if diag else 0
            nr = tq - row0
            heads = lambda ref: ref[...] if row0 == 0 else jnp.concatenate([ref[row0:tq, :], ref[tq + row0:, :]], axis=0)
            start = (blk + 1) * tq - (g + 1) * tg
            if not isinstance(start, int):
                start = pl.multiple_of(start, tg)
            z = _dot_nt(heads(qs_ref), k_ref[pl.ds(start, tg), :])
            sp = jnp.maximum(z, 0.0) + jnp.log(1.0 + jnp.exp(-jnp.abs(z)))
            if diag:
                rq = lax.broadcasted_iota(jnp.int32, (2 * nr, tg), 0)
                rk = lax.broadcasted_iota(jnp.int32, (2 * nr, tg), 1)
                valid = rk < jnp.where(rq >= nr, rq - nr, rq)
                sp = jnp.where(valid, sp, 0.0)
            hi = sp.astype(BF16)
            lo = (sp - hi.astype(F32)).astype(BF16)
            offs = heads(run_ref)
            parts = [None] * (tg // SB_SUB)
            for sb in reversed(range(tg // SB_SUB)):
                sl = slice(sb * SB_SUB, (sb + 1) * SB_SUB)
                rs = _dot(jnp.concatenate([hi[:, sl], lo[:, sl]], axis=1), suffix)
                a = jnp.exp(z[:, sl] - rs[:, :SB_SUB] - offs)
                if diag:
                    a = jnp.where(valid[:, sl], a, 0.0)
                parts[sb] = a.astype(BF16)
                offs = offs + rs[:, SB_SUB:]
            pv = _dot(jnp.concatenate(parts, axis=1), v_ref[pl.ds(start, tg), :])
            for hh in range(2):
                rows = slice(hh * tq + row0, (hh + 1) * tq)
                run_ref[rows, :] = offs[hh * nr:(hh + 1) * nr]
                acc_ref[rows, :] += pv[hh * nr:(hh + 1) * nr]
            return jnp.min(offs)

        for g in range(n_diag):
            low = group(g, True)
        n_groups = (blk + 1) * n_diag
        first = n_diag
        if blk > 0:
            low = group(n_diag, False)
            first += 1
        if first < n_groups:
            def cond(carry):
                g, low = carry
                return (g < n_groups) & (low < SB_EXIT)

            def body(carry):
                g, _ = carry
                return g + 1, group(g, False)

            lax.while_loop(cond, body, (jnp.int32(first), low))
        o_ref[qrows, :] = _unstack_heads(acc_ref[...], tq).astype(BF16)

    for blk in range(nq):
        sweep(blk)


def _sb(u, bsz, s, tq=512, tg=256):
    nq = s // tq
    stat = pltpu.VMEM((nq, 2 * tq, LANES), F32)
    return pl.pallas_call(
        functools.partial(_sb_kernel, tq=tq, tg=tg, nq=nq),
        grid=(bsz, SB_HEADS // 2),
        in_specs=[
            pl.BlockSpec((s, PAIR), lambda b, p: (b, SB_Q + p)),
            pl.BlockSpec((s, PAIR), lambda b, p: (b, SB_K + p)),
            pl.BlockSpec((s, PAIR), lambda b, p: (b, SB_V + p)),
        ],
        out_specs=pl.BlockSpec((s, PAIR), lambda b, p: (b, p)),
        out_shape=jax.ShapeDtypeStruct((bsz * s, D_SB), BF16),
        scratch_shapes=[pltpu.VMEM((nq, 2 * tq, PAIR), BF16), stat, stat],
        compiler_params=_cparams("parallel", "parallel"),
        name="sb",
    )(u, u, u)


def _mlstm_kernel(q_ref, k_ref, v_ref, og_ref, g_ref, gt_ref, nw_ref, o_ref, e_ref, *, s):
    L = ML_CHUNK
    nc = s // L

    ci = lax.broadcasted_iota(jnp.int32, (LANES, N_EXP), 0)
    cj = lax.broadcasted_iota(jnp.int32, (LANES, N_EXP), 1)
    src_col = jnp.where(cj < EXP_B, G_MLP + lax.shift_right_logical(cj, 7),
                        jnp.where(cj < EXP_A, G_MLB + lax.shift_right_logical(cj - EXP_B, 6),
                                  G_MLA + lax.shift_right_logical(cj - EXP_A, 6)))
    e_ref[...] = jnp.where(ci == src_col, 1.0, 0.0).astype(BF16)

    lane = lax.broadcasted_iota(jnp.int32, (L, LANES), 1)
    lo_lanes = lane < HEAD_DIM
    r_ll = lax.broadcasted_iota(jnp.int32, (L, L), 0)
    c_ll = lax.broadcasted_iota(jnp.int32, (L, L), 1)
    tri = r_ll >= c_ll
    blockdiag = (r_ll < HEAD_DIM) == (c_ll < HEAD_DIM)
    ones = jnp.ones((L, LANES), BF16)

    def chunk(c, carry):
        start = pl.multiple_of(c * L, L)
        rep = _dot_exact(g_ref[pl.ds(start, L), :], e_ref[...], lhs_exact=False)
        gtc = gt_ref[:, pl.ds(start, L)]
        new_carry = []
        for p in range(ML_HEADS // 2):
            c2, nm, m2 = carry[p]
            sl = slice(p * PAIR, (p + 1) * PAIR)
            q2 = q_ref[pl.ds(start, L), sl]
            k2 = k_ref[pl.ds(start, L), sl]
            v_one = jnp.concatenate([v_ref[pl.ds(start, L), sl], ones], axis=1)
            b2 = rep[:, EXP_B + p * PAIR:EXP_B + (p + 1) * PAIR]
            a2 = rep[:, EXP_A + p * PAIR:EXP_A + (p + 1) * PAIR]
            pm = [rep[:, (2 * p + hh) * LANES:(2 * p + hh + 1) * LANES] for hh in range(2)]
            pm2 = jnp.where(lo_lanes, pm[0], pm[1])
            qk = _dot_nt(_stack_heads(q2), k2)
            sc = []
            for hh in range(2):
                a_row = gtc[G_MLA + 2 * p + hh:G_MLA + 2 * p + hh + 1, :]
                sc.append(qk[hh * L:(hh + 1) * L] * jnp.exp(jnp.where(tri, a_row - pm[hh], -jnp.inf)))
            r = _dot(jnp.concatenate(sc, axis=0).astype(BF16), v_one)
            num_loc = jnp.where(lo_lanes, r[:L, :LANES], r[L:, :LANES])
            den_loc = jnp.where(lo_lanes, r[:L, LANES:], r[L:, LANES:])
            m_loc = b2 + pm2
            inter = _dot(q2, jnp.concatenate([c2, nm], axis=1).astype(BF16))
            it2 = b2 + m2
            m_t = jnp.maximum(it2, m_loc)
            f_loc = jnp.exp(m_loc - m_t)
            w_int = jnp.exp(it2 - m_t)
            num = num_loc * f_loc + w_int * inter[:, :LANES]
            den = den_loc * f_loc + w_int * inter[:, LANES:]
            hp = num / jnp.maximum(jnp.abs(den), jnp.exp(-m_t))
            sq = hp * hp
            ms0 = jnp.sum(jnp.where(lo_lanes, sq, 0.0), axis=1, keepdims=True)
            ms1 = jnp.sum(jnp.where(lo_lanes, 0.0, sq), axis=1, keepdims=True)
            msq = jnp.where(lo_lanes, ms0, ms1) * (1.0 / HEAD_DIM)
            hp = hp * lax.rsqrt(msq + RMS_EPS) * nw_ref[:, sl]
            gate = jax.nn.sigmoid(og_ref[pl.ds(start, L), sl].astype(F32))
            o_ref[pl.ds(start, L), sl] = (gate * hp).astype(BF16)
            g2 = b2[L - 1:L, :]
            pl2 = pm2[L - 1:L, :]
            kw = (k2.astype(F32) * jnp.exp(a2 - pl2)).astype(BF16)
            upd = _dot_tn(kw, v_one)
            m_new = jnp.maximum(g2 + m2, g2 + pl2)
            decay = jnp.exp(g2 + m2 - m_new)
            scale = jnp.exp(g2 + pl2 - m_new)
            c2 = decay * c2 + scale * jnp.where(blockdiag, upd[:, :LANES], 0.0)
            nm = decay * nm + scale * jnp.where(blockdiag, upd[:, LANES:], 0.0)
            new_carry.append((c2, nm, m_new))
        return tuple(new_carry)

    init = tuple((jnp.zeros((LANES, LANES), F32), jnp.zeros((LANES, LANES), F32), jnp.zeros((1, LANES), F32))
                 for _ in range(ML_HEADS // 2))
    lax.fori_loop(0, nc, chunk, init, unroll=True)


def _mlstm(u, g, gt, norm_w, bsz, s):
    blk = lambda j: pl.BlockSpec((s, D_ML), lambda b: (b, j))
    first = (3 * D_FOX + 3 * D_SB) // D_ML
    return pl.pallas_call(
        functools.partial(_mlstm_kernel, s=s),
        grid=(bsz,),
        in_specs=[
            blk(first), blk(first + 1), blk(first + 2), blk(first + 3),
            pl.BlockSpec((s, N_GATE), lambda b: (b, 0)),
            pl.BlockSpec((None, GT_ROWS, s), lambda b: (b, 0, 0)),
            pl.BlockSpec((1, D_ML), lambda b: (0, 0)),
        ],
        out_specs=pl.BlockSpec((s, D_ML), lambda b: (b, 0)),
        out_shape=jax.ShapeDtypeStruct((bsz * s, D_ML), BF16),
        scratch_shapes=[pltpu.VMEM((LANES, N_EXP), BF16)],
        compiler_params=_cparams("parallel"),
        name="mlstm",
    )(u, u, u, u, g, gt, norm_w)


def _memkv_kernel(x_ref, nw_ref, w_ref, o_ref):
    xn = _rms(x_ref[...], nw_ref[...]).astype(BF16)
    o_ref[...] = _dot(xn, w_ref[...]).astype(BF16)


def _memkv(mem2d, nw, w, tm=512):
    t = mem2d.shape[0]
    depth = w.shape[0]
    return pl.pallas_call(
        _memkv_kernel,
        grid=(depth, t // tm),
        in_specs=[
            pl.BlockSpec((tm, D_MODEL), lambda l, i: (i, 0)),
            pl.BlockSpec((None, 1, D_MODEL), lambda l, i: (l, 0, 0)),
            pl.BlockSpec((None, D_MODEL, 2 * D_X), lambda l, i: (l, 0, 0)),
        ],
        out_specs=pl.BlockSpec((None, tm, 2 * D_X), lambda l, i: (l, i, 0)),
        out_shape=jax.ShapeDtypeStruct((depth, t, 2 * D_X), BF16),
        compiler_params=_cparams("parallel", "parallel"),
        name="memkv",
    )(mem2d, nw, w)


def _cross_kernel(yf_ref, ys_ref, ym_ref, wo_ref, h_ref, nw_ref, wq_ref, kv_ref, wxo_ref, o_ref):
    hf = (h_ref[...] + _dot(yf_ref[...], wo_ref[:D_FOX, :]) + _dot(ys_ref[...], wo_ref[D_FOX:D_FOX + D_SB, :])
          + _dot(ym_ref[...], wo_ref[D_FOX + D_SB:, :]))
    hn = _rms(hf, nw_ref[...]).astype(BF16)
    q = _dot(hn, wq_ref[...]).astype(BF16)
    outs = []
    for hd in range(X_HEADS):
        sl = slice(hd * X_HEAD_DIM, (hd + 1) * X_HEAD_DIM)
        k = kv_ref[:, sl]
        v = kv_ref[:, D_X + hd * X_HEAD_DIM:D_X + (hd + 1) * X_HEAD_DIM]
        s = _dot_nt(q[:, sl], k) * (X_HEAD_DIM ** -0.5 * LOG2E)
        e = jnp.exp2(s - jnp.max(s, axis=1, keepdims=True))
        o_h = _dot(e.astype(BF16), v) / jnp.sum(e, axis=1, keepdims=True)
        outs.append(o_h.astype(BF16))
    o = jnp.concatenate(outs, axis=1)
    o_ref[...] = hf + _dot(o, wxo_ref[...])


def _cross(yf, ys, ym, wo, h, nw, wq, kv, wxo, layer, bsz, s, mlen, tm=1024):
    nt = s // tm
    row = lambda n: pl.BlockSpec((tm, n), lambda b, i: (b * nt + i, 0))
    wgt = lambda r, c: pl.BlockSpec((None, r, c), lambda b, i: (layer, 0, 0))
    return pl.pallas_call(
        _cross_kernel,
        grid=(bsz, nt),
        in_specs=[
            row(D_FOX), row(D_SB), row(D_ML), wgt(D_MODEL, D_MODEL), row(D_MODEL),
            pl.BlockSpec((1, D_MODEL), lambda b, i: (0, 0)),
            wgt(D_MODEL, D_X),
            pl.BlockSpec((None, mlen, 2 * D_X), lambda b, i: (layer, b, 0)),
            wgt(D_X, D_MODEL),
        ],
        out_specs=row(D_MODEL),
        out_shape=jax.ShapeDtypeStruct((bsz * s, D_MODEL), F32),
        compiler_params=_cparams("parallel", "parallel"),
        name="cross",
    )(yf, ys, ym, wo, h, nw, wq, kv, wxo)


def _ffn_kernel(h_ref, nw_ref, wg_ref, wu_ref, wd_ref, fw_ref, o_ref, acc_ref, *, f_chunk, final_norm):
    hf = h_ref[...]
    hn = _rms(hf, nw_ref[...]).astype(BF16)
    for c in range(D_FF // f_chunk):
        sl = slice(c * f_chunk, (c + 1) * f_chunk)
        g = _dot(hn, wg_ref[:, sl].astype(BF16))
        u = _dot(hn, wu_ref[:, sl].astype(BF16))
        a = (g * jax.nn.sigmoid(g) * u).astype(BF16)
        part = _dot(a, wd_ref[sl, :].astype(BF16))
        if c == 0:
            acc_ref[...] = part
        else:
            acc_ref[...] += part
    out = hf + acc_ref[...]
    if final_norm:
        out = _rms(out, fw_ref[...])
    o_ref[...] = out


def _ffn(h, nw, wg, wu, wd, fw, layer, final_norm, tm=512):
    t = h.shape[0]
    const = lambda shape: pl.BlockSpec(shape, lambda i: (0, 0))
    wgt = lambda r, c: pl.BlockSpec((None, r, c), lambda i: (layer, 0, 0), pipeline_mode=pl.Buffered(1))
    return pl.pallas_call(
        functools.partial(_ffn_kernel, f_chunk=256, final_norm=final_norm),
        grid=(t // tm,),
        in_specs=[
            pl.BlockSpec((tm, D_MODEL), lambda i: (i, 0)),
            const((1, D_MODEL)),
            wgt(D_MODEL, D_FF), wgt(D_MODEL, D_FF), wgt(D_FF, D_MODEL),
            const((1, D_MODEL)),
        ],
        out_specs=pl.BlockSpec((tm, D_MODEL), lambda i: (i, 0)),
        out_shape=jax.ShapeDtypeStruct((t, D_MODEL), F32),
        scratch_shapes=[pltpu.VMEM((tm, D_MODEL), F32)],
        compiler_params=_cparams("parallel"),
        name="ffn",
    )(h, nw, wg, wu, wd, fw)


N_FOX_QKV = 3 * D_FOX
N_REST = 3 * D_SB + 4 * D_ML


N_USED_GATES = FOX_HEADS + 2 * ML_HEADS


def _regroup_kernel(w_ref, o_ref):
    o_ref[:, :N_FOX_QKV] = w_ref[:, :N_FOX_QKV].astype(BF16)
    rest = w_ref[:, N_FOX_QKV + FOX_HEADS:N_FOX_QKV + FOX_HEADS + N_REST]
    o_ref[:, N_FOX_QKV:D_MAIN] = rest.astype(BF16)
    lane = lax.broadcasted_iota(jnp.int32, (w_ref.shape[0], N_USED_GATES), 1)
    gates = jnp.where(lane < FOX_HEADS, w_ref[:, N_FOX_QKV:N_FOX_QKV + N_USED_GATES], w_ref[:, D_MAIN:D_MAIN + N_USED_GATES])
    o_ref[:, D_MAIN:] = jnp.zeros((w_ref.shape[0], N_GATE), BF16)
    o_ref[:, D_MAIN:D_MAIN + N_USED_GATES] = gates.astype(BF16)


def _regroup_w_in(w, tr=256):
    depth, rows, n_in = w.shape
    assert N_FOX_QKV + N_REST == D_MAIN and D_MAIN % LANES == 0 and n_in == D_MAIN + N_USED_GATES
    return pl.pallas_call(
        _regroup_kernel,
        grid=(depth, rows // tr),
        in_specs=[pl.BlockSpec((None, tr, n_in), lambda l, i: (l, i, 0))],
        out_specs=pl.BlockSpec((None, tr, D_MAIN + N_GATE), lambda l, i: (l, i, 0)),
        out_shape=jax.ShapeDtypeStruct((depth, rows, D_MAIN + N_GATE), BF16),
        compiler_params=_cparams("parallel", "parallel"),
        name="regroup",
    )(w)


def kernel(x, mem, norm_mix_w, w_in, fox_f_b, ml_conv_w, ml_i_b, ml_f_b, ml_norm_w, w_out, norm_x_w, mem_norm_w,
           wx_q, wx_kv, wx_o, norm_ffn_w, w_gate, w_up, w_down, final_norm_w):
    bsz, s, _ = x.shape
    mlen = mem.shape[1]
    depth = w_in.shape[0]
    h = x.reshape(bsz * s, D_MODEL)
    mem2d = mem.reshape(bsz * mlen, D_MODEL)
    row = lambda v: v.reshape(1, -1)
    w_all = _regroup_w_in(w_in)
    bias = jnp.concatenate([fox_f_b, ml_i_b, ml_f_b, jnp.zeros((depth, N_GATE - 3 * FOX_HEADS), F32)], axis=1)
    wo, wq, wkv, wxo = (w.astype(BF16) for w in (w_out, wx_q, wx_kv, wx_o))
    kv = _memkv(mem2d, mem_norm_w.reshape(depth, 1, D_MODEL), wkv, tm=min(512, bsz * mlen))
    for l in range(depth):
        u, g, gt = _inproj(h, row(norm_mix_w[l]), w_all, row(bias[l]), ml_conv_w[l], l, bsz, s)
        y_fox = _fox(u, g, gt, bsz, s)
        y_sb = _sb(u, bsz, s)
        y_ml = _mlstm(u, g, gt, row(ml_norm_w[l]), bsz, s)
        h = _cross(y_fox, y_sb, y_ml, wo, h, row(norm_x_w[l]), wq, kv, wxo, l, bsz, s, mlen)
        h = _ffn(h, row(norm_ffn_w[l]), w_gate, w_up, w_down, row(final_norm_w), l, final_norm=(l == depth - 1))
    return h.reshape(bsz, s, D_MODEL)
```
